```python
import math
import jax
import jax.numpy as jnp
from jax import lax
import numpy as np

D_MODEL = 2048
BATCH = 4
SEQ = 4096
DEPTH = 2

GRID_W = 64
CTX_LEN = 256
HEAD_DIM = 128
NA_HEADS = 4
DN_HEADS = 4
FT_GROUPS = 4
SG_GROUPS = 4
N_BRANCH = 4
W_NA = NA_HEADS * HEAD_DIM
W_DN = DN_HEADS * HEAD_DIM
W_FT = FT_GROUPS * HEAD_DIM
W_SG = SG_GROUPS * HEAD_DIM
W_BR = W_NA
NA_WIN_R = 8
NA_WIN_C = 16
DN_CHUNK = 64
DN_CONV = 5
SG_CHUNK = 128
D_FF = ((8 * D_MODEL + 3 * 256 - 1) // (3 * 256)) * 256
N_MOD = 6
IN_SPLITS = (W_NA, W_NA, W_NA, 3 * W_DN, W_DN, 2 * DN_HEADS, 2 * DN_HEADS, W_FT, W_SG, W_SG)
D_IN = sum(IN_SPLITS)
EPS = 1e-6
NEG_INF = -1e30

kernel_name = "hybrid_dit_natten_deltanet_fnet_gmlp"


def _rmsnorm(x, w):
    xf = x.astype(jnp.float32)
    y = xf * lax.rsqrt(jnp.mean(xf * xf, axis=-1, keepdims=True) + EPS)
    return (y * w.astype(jnp.float32)).astype(x.dtype)


def _layernorm(x, w):
    xf = x.astype(jnp.float32)
    mu = jnp.mean(xf, axis=-1, keepdims=True)
    var = jnp.mean(jnp.square(xf - mu), axis=-1, keepdims=True)
    return ((xf - mu) * lax.rsqrt(var + EPS) * w.astype(jnp.float32)).astype(x.dtype)


def _l2norm(x):
    return x * lax.rsqrt(jnp.sum(x * x, axis=-1, keepdims=True) + EPS)


def _heads(x, n):
    return x.reshape(x.shape[0], x.shape[1], n, -1)


def _dwconv_centred(x, w):
    k = w.shape[0]
    return lax.conv_general_dilated(
        x, w[:, None, :].astype(x.dtype), window_strides=(1,), padding=[(k // 2, k // 2)],
        dimension_numbers=('NWC', 'WIO', 'NWC'), feature_group_count=x.shape[-1])


def _swiglu(h, w1, w3, w2):
    return (jax.nn.silu(h @ w1) * (h @ w3)) @ w2


def _neighbourhood_attention(q, k, v, k_ctx, v_ctx, rpb):
    B, S, H, dh = q.shape
    rows = S // GRID_W
    kr = min(NA_WIN_R, rows)
    scale = dh ** -0.5
    r = np.arange(rows)
    row_idx = np.clip(r - kr // 2, 0, rows - kr)[:, None] + np.arange(kr)[None, :]
    col = np.arange(GRID_W)
    col_start = np.clip(col - NA_WIN_C // 2, 0, GRID_W - NA_WIN_C)
    in_win = (col[None, :] >= col_start[:, None]) & (col[None, :] < col_start[:, None] + NA_WIN_C)
    mask = np.tile(in_win, (1, kr))
    dr = row_idx - r[:, None]
    dc = np.clip(col[None, :] - col[:, None], 1 - NA_WIN_C, NA_WIN_C - 1)
    bias = rpb[:, dr[:, None, :, None] + NA_WIN_R - 1, dc[None, :, None, :] + NA_WIN_C - 1]
    bias = bias.reshape(H, rows, GRID_W, kr * GRID_W).astype(jnp.float32)

    qg = q.reshape(B, rows, GRID_W, H, dh)
    kb = k.reshape(B, rows, GRID_W, H, dh)[:, row_idx].reshape(B, rows, kr * GRID_W, H, dh)
    vb = v.reshape(B, rows, GRID_W, H, dh)[:, row_idx].reshape(B, rows, kr * GRID_W, H, dh)
    s_loc = jnp.einsum('brqhd,brkhd->bhrqk', qg, kb, preferred_element_type=jnp.float32) * scale + bias[None]
    s_loc = jnp.where(mask, s_loc, NEG_INF)
    s_ctx = jnp.einsum('brqhd,blhd->bhrql', qg, k_ctx, preferred_element_type=jnp.float32) * scale
    p = jax.nn.softmax(jnp.concatenate([s_loc, s_ctx], axis=-1), axis=-1).astype(v.dtype)
    nk = kr * GRID_W
    o = (jnp.einsum('bhrqk,brkhd->brqhd', p[..., :nk], vb)
         + jnp.einsum('bhrql,blhd->brqhd', p[..., nk:], v_ctx))
    return o.reshape(B, S, H * dh)


def _dense_attention(q, k, v):
    B, L, H, dh = q.shape
    s = jnp.einsum('bqhd,bkhd->bhqk', q, k, preferred_element_type=jnp.float32) * dh ** -0.5
    p = jax.nn.softmax(s, axis=-1).astype(v.dtype)
    return jnp.einsum('bhqk,bkhd->bqhd', p, v).reshape(B, L, H * dh)


def _gated_delta_chunked(q, k, v, g, beta, s0):
    B, T, H, dk = q.shape
    dv = v.shape[-1]
    C = DN_CHUNK
    n = T // C

    def chunks(a):
        a = a.reshape((B, n, C, H) + a.shape[3:])
        return jnp.moveaxis(a, (1, 3), (0, 2))

    q = chunks(q) * dk ** -0.5
    k = chunks(k)
    v = chunks(v)
    beta = chunks(beta)
    gc = jnp.cumsum(chunks(g), axis=-1)
    incl = np.tril(np.ones((C, C), bool))
    strict = np.tril(np.ones((C, C), bool), -1)
    decay = jnp.exp(jnp.where(incl, gc[..., :, None] - gc[..., None, :], -jnp.inf))
    k_beta = k * beta[..., None]
    a_mat = jnp.where(strict, jnp.einsum('nbhid,nbhjd->nbhij', k_beta, k) * decay, 0.0)
    eye = jnp.eye(C, dtype=q.dtype)
    t_mat = lax.linalg.triangular_solve(eye + a_mat, jnp.broadcast_to(eye, a_mat.shape),
                                        left_side=True, lower=True, unit_diagonal=True)
    u = t_mat @ (v * beta[..., None])
    w = t_mat @ (k_beta * jnp.exp(gc)[..., None])
    qk = jnp.where(incl, jnp.einsum('nbhid,nbhjd->nbhij', q, k) * decay, 0.0)

    def step(s, inp):
        q_c, k_c, u_c, w_c, g_c, qk_c = inp
        v_new = u_c - w_c @ s
        o_c = (q_c * jnp.exp(g_c)[..., None]) @ s + qk_c @ v_new
        g_last = g_c[..., -1:]
        s = (s * jnp.exp(g_last)[..., None]
             + jnp.einsum('bhcd,bhce->bhde', k_c * jnp.exp(g_last - g_c)[..., None], v_new))
        return s, o_c

    s_fin, o = lax.scan(step, s0, (q, k, u, w, gc, qk))
    o = jnp.moveaxis(o, (0, 2), (1, 3)).reshape(B, T, H, dv)
    return o, s_fin


def _bidir_deltanet(qkv, a, b, conv_w, a_log, dt_bias, s0_fwd, s0_bwd):
    f32 = jnp.float32
    qkv = jax.nn.silu(_dwconv_centred(qkv, conv_w)).astype(f32)
    q, k, v = jnp.split(qkv, 3, axis=-1)
    q = _l2norm(_heads(q, DN_HEADS))
    k = _l2norm(_heads(k, DN_HEADS))
    v = _heads(v, DN_HEADS)
    g = -jnp.exp(a_log.astype(f32)) * jax.nn.softplus(_heads(a.astype(f32), 2) + dt_bias.astype(f32))
    beta = jax.nn.sigmoid(_heads(b.astype(f32), 2))
    o_f, s_f = _gated_delta_chunked(q, k, v, g[:, :, 0], beta[:, :, 0], s0_fwd)
    o_b, s_b = _gated_delta_chunked(q[:, ::-1], k[:, ::-1], v[:, ::-1], g[:, ::-1, 1], beta[:, ::-1, 1], s0_bwd)
    return o_f + o_b[:, ::-1], s_f, s_b


def _gated_rmsnorm(o, z, w):
    B, T = z.shape[0], z.shape[1]
    y = _rmsnorm(o, w) * jax.nn.silu(_heads(z, DN_HEADS).astype(jnp.float32))
    return y.reshape(B, T, W_DN).astype(z.dtype)


def _fourier(f):
    B, T, _ = f.shape
    ff = f.astype(jnp.float32).reshape(B, T, FT_GROUPS, HEAD_DIM)
    y = jnp.fft.fftn(ff, axes=(1, 3), norm='ortho').real
    return y.reshape(B, T, W_FT).astype(f.dtype)


def _spatial_gating(u, v, w_s, b_s, v_norm):
    B, T, _ = u.shape
    n = T // SG_CHUNK
    u = jax.nn.gelu(u)
    v = _layernorm(jax.nn.gelu(v), v_norm)
    vc = v.reshape(B, n, SG_CHUNK, SG_GROUPS, HEAD_DIM)
    mix = jnp.einsum('gts,bnsgd->bntgd', w_s, vc) + b_s.T[:, :, None]
    return u * mix.reshape(B, T, W_SG)


def _hybrid_mixer(h, hc, w_in, na_qnorm, na_knorm, na_rpb, dn_conv, dn_a_log, dn_dt_bias, dn_onorm,
                  sg_w, sg_b, sg_vnorm, w_gate, b_gate, w_branch, w_out, need_ctx):
    offsets = [int(o) for o in np.cumsum(IN_SPLITS)[:-1]]
    na_q, na_k, na_v, dn_qkv, dn_z, dn_a, dn_b, ft, sg_u, sg_v = jnp.split(h @ w_in, offsets, axis=-1)
    cna_q, cna_k, cna_v, cdn_qkv, cdn_z, cdn_a, cdn_b, cft, csg_u, csg_v = jnp.split(hc @ w_in, offsets, axis=-1)

    k_c = _rmsnorm(_heads(cna_k, NA_HEADS), na_knorm)
    v_c = _heads(cna_v, NA_HEADS)
    y_na = _neighbourhood_attention(_rmsnorm(_heads(na_q, NA_HEADS), na_qnorm),
                                    _rmsnorm(_heads(na_k, NA_HEADS), na_knorm),
                                    _heads(na_v, NA_HEADS), k_c, v_c, na_rpb)

    zero = jnp.zeros((hc.shape[0], DN_HEADS, HEAD_DIM, HEAD_DIM), jnp.float32)
    o_dn_c, s_f, s_b = _bidir_deltanet(cdn_qkv, cdn_a, cdn_b, dn_conv, dn_a_log, dn_dt_bias, zero, zero)
    o_dn, _, _ = _bidir_deltanet(dn_qkv, dn_a, dn_b, dn_conv, dn_a_log, dn_dt_bias, s_f, s_b)
    y_dn = _gated_rmsnorm(o_dn, dn_z, dn_onorm)

    def merge(hh, branches):
        merged = None
        for i, y in enumerate(branches):
            term = jax.nn.sigmoid(hh @ w_gate[i] + b_gate[i]) * (y @ w_branch[i])
            merged = term if merged is None else merged + term
        return merged @ w_out

    y = merge(h, (y_na, y_dn, _fourier(ft), _spatial_gating(sg_u, sg_v, sg_w, sg_b, sg_vnorm)))
    if not need_ctx:
        return y, None
    y_na_c = _dense_attention(_rmsnorm(_heads(cna_q, NA_HEADS), na_qnorm), k_c, v_c)
    yc = merge(hc, (y_na_c, _gated_rmsnorm(o_dn_c, cdn_z, dn_onorm), _fourier(cft),
                    _spatial_gating(csg_u, csg_v, sg_w, sg_b, sg_vnorm)))
    return y, yc


def setup_inputs(seed: int = 0) -> dict:
    key = jax.random.key(seed)
    ks = iter(jax.random.split(key, 40))
    f32 = jnp.float32

    def nrm(shape, scale):
        return jax.random.normal(next(ks), shape, f32) * scale

    def gain(shape):
        return 1.0 + nrm(shape, 0.02)

    dt = jnp.exp(jax.random.uniform(next(ks), (DEPTH, 2, DN_HEADS), f32, math.log(1e-3), math.log(1e-1)))
    return {
        "x": nrm((BATCH, SEQ, D_MODEL), 1.0),
        "c": nrm((BATCH, D_MODEL), 1.0),
        "ctx": nrm((BATCH, CTX_LEN, D_MODEL), 1.0),
        "c_ctx": nrm((D_MODEL,), 1.0),
        "w_ada": nrm((DEPTH, D_MODEL, N_MOD * D_MODEL), 0.5 * D_MODEL ** -0.5),
        "b_ada": nrm((DEPTH, N_MOD * D_MODEL), 0.02),
        "norm1_w": gain((DEPTH, D_MODEL)),
        "norm2_w": gain((DEPTH, D_MODEL)),
        "w_in": nrm((DEPTH, D_MODEL, D_IN), D_MODEL ** -0.5),
        "na_qnorm": gain((DEPTH, HEAD_DIM)),
        "na_knorm": gain((DEPTH, HEAD_DIM)),
        "na_rpb": nrm((DEPTH, NA_HEADS, 2 * NA_WIN_R - 1, 2 * NA_WIN_C - 1), 0.5),
        "dn_conv": nrm((DEPTH, DN_CONV, 3 * W_DN), DN_CONV ** -0.5),
        "dn_a_log": jnp.log(jax.random.uniform(next(ks), (DEPTH, 2, DN_HEADS), f32, 1.0, 16.0)),
        "dn_dt_bias": dt + jnp.log(-jnp.expm1(-dt)),
        "dn_onorm": gain((DEPTH, HEAD_DIM)),
        "sg_w": nrm((DEPTH, SG_GROUPS, SG_CHUNK, SG_CHUNK), SG_CHUNK ** -0.5),
        "sg_b": nrm((DEPTH, SG_GROUPS, SG_CHUNK), 0.02),
        "sg_vnorm": gain((DEPTH, W_SG)),
        "w_gate": nrm((DEPTH, N_BRANCH, D_MODEL, D_MODEL), D_MODEL ** -0.5),
        "b_gate": nrm((DEPTH, N_BRANCH, D_MODEL), 0.02),
        "w_branch": nrm((DEPTH, N_BRANCH, W_BR, D_MODEL), W_BR ** -0.5),
        "w_out": nrm((DEPTH, D_MODEL, D_MODEL), D_MODEL ** -0.5),
        "ffn_w1": nrm((DEPTH, D_MODEL, D_FF), D_MODEL ** -0.5),
        "ffn_w3": nrm((DEPTH, D_MODEL, D_FF), D_MODEL ** -0.5),
        "ffn_w2": nrm((DEPTH, D_FF, D_MODEL), D_FF ** -0.5),
    }


def reference(x, c, ctx, c_ctx, w_ada, b_ada, norm1_w, norm2_w, w_in, na_qnorm, na_knorm, na_rpb,
              dn_conv, dn_a_log, dn_dt_bias, dn_onorm, sg_w, sg_b, sg_vnorm, w_gate, b_gate, w_branch,
              w_out, ffn_w1, ffn_w3, ffn_w2):
    for l in range(DEPTH):
        need_ctx = l < DEPTH - 1
        mod_l = (jax.nn.silu(c) @ w_ada[l] + b_ada[l])[:, None, :]
        mod_c = (jax.nn.silu(c_ctx) @ w_ada[l] + b_ada[l])[None, None, :]
        sh1, sc1, g1, sh2, sc2, g2 = jnp.split(mod_l, N_MOD, axis=-1)
        csh1, csc1, cg1, csh2, csc2, cg2 = jnp.split(mod_c, N_MOD, axis=-1)

        h = _rmsnorm(x, norm1_w[l]) * (1.0 + sc1) + sh1
        hc = _rmsnorm(ctx, norm1_w[l]) * (1.0 + csc1) + csh1
        y, yc = _hybrid_mixer(h, hc, w_in[l], na_qnorm[l], na_knorm[l], na_rpb[l], dn_conv[l], dn_a_log[l],
                              dn_dt_bias[l], dn_onorm[l], sg_w[l], sg_b[l], sg_vnorm[l], w_gate[l], b_gate[l],
                              w_branch[l], w_out[l], need_ctx)
        x = x + g1 * y
        h = _rmsnorm(x, norm2_w[l]) * (1.0 + sc2) + sh2
        x = x + g2 * _swiglu(h, ffn_w1[l], ffn_w3[l], ffn_w2[l])
        if need_ctx:
            ctx = ctx + cg1 * yc
            hc = _rmsnorm(ctx, norm2_w[l]) * (1.0 + csc2) + csh2
            ctx = ctx + cg2 * _swiglu(hc, ffn_w1[l], ffn_w3[l], ffn_w2[l])
    return x
```

```python
import functools

import numpy as np
import jax
import jax.numpy as jnp
from jax import lax
from jax.experimental import pallas as pl
from jax.experimental.pallas import tpu as pltpu

F32 = jnp.float32
BF16 = jnp.bfloat16
SDS = jax.ShapeDtypeStruct

LANES = 128
HEAD_DIM = 128
N_HEADS = 4
W_BR = N_HEADS * HEAD_DIM
N_BRANCH = 4
GRID_W = 64
NA_WIN_R = 8
NA_WIN_C = 16
DN_CHUNK = 64
DN_CONV = 5
SG_CHUNK = 128
N_MOD = 6
EPS = 1e-6
NEG_INF = -1e30
MOD_ROWS = 8
N_AB = 4 * N_HEADS
VMEM_LIMIT_BYTES = 56 * 1024 * 1024

C_NA_Q, C_NA_K, C_NA_V = 0, W_BR, 2 * W_BR
C_DN_QKV = 3 * W_BR
C_DN_Z = 6 * W_BR
C_FT = 7 * W_BR
C_SG_U = 8 * W_BR
C_SG_V = 9 * W_BR
N_MAIN = 10 * W_BR
C_AB_SRC = 7 * W_BR


def _cp(*sem):
    return pltpu.CompilerParams(dimension_semantics=sem, vmem_limit_bytes=VMEM_LIMIT_BYTES)


def _tile(n, pref, mult=8):
    if n <= pref:
        return n
    for t in range(pref - pref % mult, 0, -mult):
        if n % t == 0:
            return t
    return n


def _dot(a, b):
    return jnp.dot(a, b, preferred_element_type=F32)


def _dot_nt(a, b):
    return lax.dot_general(a, b, (((1,), (1,)), ((), ())), preferred_element_type=F32)


def _dot_tn(a, b):
    return lax.dot_general(a, b, (((0,), (0,)), ((), ())), preferred_element_type=F32)


def _split(a):
    hi = a.astype(BF16)
    return hi, (a - hi.astype(F32)).astype(BF16)


def _dot3(a, b):
    ah, al = _split(a)
    bh, bl = _split(b)
    return _dot(ah, bh) + (_dot(ah, bl) + _dot(al, bh))


def _silu(x):
    return x * jax.nn.sigmoid(x)


def _rms(x, w):
    return x * lax.rsqrt(jnp.mean(x * x, axis=-1, keepdims=True) + EPS) * w


def _mod_row(ref, row):
    b = pl.program_id(0) if row is None else row
    return ref[pl.ds(b, 1), :]


def _norm_mod(x, nw_ref, sh_ref, sc_ref, row):
    return (_rms(x, nw_ref[...]) * (1.0 + _mod_row(sc_ref, row)) + _mod_row(sh_ref, row)).astype(BF16)


def _ada_kernel(c_ref, w_ref, b_ref, o_ref):
    s = _silu(c_ref[...]).astype(BF16)
    o_ref[0] = _dot(s, w_ref[0].astype(BF16)) + b_ref[0]


def _ada_mod(cc, w_ada, b_ada):
    depth, d, n = w_ada.shape
    tn = _tile(n, 1024, LANES)
    return pl.pallas_call(
        _ada_kernel,
        grid=(depth, n // tn),
        in_specs=[
            pl.BlockSpec((MOD_ROWS, d), lambda l, j: (0, 0)),
            pl.BlockSpec((1, d, tn), lambda l, j: (l, 0, j)),
            pl.BlockSpec((1, 1, tn), lambda l, j: (l, 0, j)),
        ],
        out_specs=pl.BlockSpec((1, MOD_ROWS, tn), lambda l, j: (l, 0, j)),
        out_shape=SDS((depth, MOD_ROWS, n), F32),
        compiler_params=_cp("parallel", "parallel"),
        name="ada_mod",
    )(cc, w_ada, b_ada.reshape(depth, 1, n))


def _in_proj_kernel(x_ref, nw_ref, sh_ref, sc_ref, w_ref, wab_ref, o_ref, oab_ref, h_ref, hs_ref, *, row):
    @pl.when(pl.program_id(2) == 0)
    def _():
        h = _norm_mod(x_ref[0], nw_ref, sh_ref, sc_ref, row)
        hs_ref[...] = h
        h_ref[0] = h
        oab_ref[0] = _dot(h, wab_ref[...])

    o_ref[0] = _dot(hs_ref[...], w_ref[...])


def _in_proj(x, nw, mod, w_main, w_ab, row):
    bx, t, d = x.shape
    n = w_main.shape[1]
    tm, tn = _tile(t, 1024), _tile(n, 1024, LANES)
    return pl.pallas_call(
        functools.partial(_in_proj_kernel, row=row),
        grid=(bx, t // tm, n // tn),
        in_specs=[
            pl.BlockSpec((1, tm, d), lambda b, i, j: (b, i, 0)),
            pl.BlockSpec((1, d), lambda b, i, j: (0, 0)),
            pl.BlockSpec((MOD_ROWS, d), lambda b, i, j: (0, 0)),
            pl.BlockSpec((MOD_ROWS, d), lambda b, i, j: (0, 1)),
            pl.BlockSpec((d, tn), lambda b, i, j: (0, j)),
            pl.BlockSpec((d, LANES), lambda b, i, j: (0, 0)),
        ],
        out_specs=[
            pl.BlockSpec((1, tm, tn), lambda b, i, j: (b, i, j)),
            pl.BlockSpec((1, tm, LANES), lambda b, i, j: (b, i, 0)),
            pl.BlockSpec((1, tm, d), lambda b, i, j: (b, i, 0)),
        ],
        out_shape=[SDS((bx, t, n), F32), SDS((bx, t, LANES), F32), SDS((bx, t, d), BF16)],
        scratch_shapes=[pltpu.VMEM((tm, d), BF16)],
        compiler_params=_cp("parallel", "parallel", "arbitrary"),
        name="in_proj",
    )(x, nw, mod, mod, w_main, w_ab)


def _na_kernel(q_ref, k_ref, v_ref, kc_ref, vc_ref, qn_ref, kn_ref, bias_ref, o_ref,
               qs, ks, vs, kcs, vcs, *, rows):
    qs[...] = _rms(q_ref[0], qn_ref[...]).astype(BF16)
    ks[...] = _rms(k_ref[0], kn_ref[...]).astype(BF16)
    vs[...] = v_ref[0].astype(BF16)
    kcs[...] = _rms(kc_ref[0], kn_ref[...]).astype(BF16)
    vcs[...] = vc_ref[0].astype(BF16)
    scale = HEAD_DIM ** -0.5
    band = NA_WIN_R * GRID_W

    def body(r, carry):
        start = jnp.clip(r - NA_WIN_R // 2, 0, rows - NA_WIN_R)
        r0 = pl.multiple_of(r * GRID_W, GRID_W)
        k0 = pl.multiple_of(start * GRID_W, GRID_W)
        q = qs[pl.ds(r0, GRID_W), :]
        s = _dot_nt(q, ks[pl.ds(k0, band), :]) * scale + bias_ref[0, r - start]
        sc = _dot_nt(q, kcs[...]) * scale
        m = jnp.maximum(jnp.max(s, axis=-1, keepdims=True), jnp.max(sc, axis=-1, keepdims=True))
        p = jnp.exp(s - m)
        pc = jnp.exp(sc - m)
        denom = jnp.sum(p, axis=-1, keepdims=True) + jnp.sum(pc, axis=-1, keepdims=True)
        o = _dot(p.astype(BF16), vs[pl.ds(k0, band), :]) + _dot(pc.astype(BF16), vcs[...])
        o_ref[0, pl.ds(r0, GRID_W), :] = (o / denom).astype(o_ref.dtype)
        return carry

    lax.fori_loop(0, rows, body, 0, unroll=2)


def _na_bias_table(rpb):
    typ = np.arange(NA_WIN_R)[:, None]
    j = np.arange(NA_WIN_R)[None, :]
    dr = j - typ + NA_WIN_R - 1
    col = np.arange(GRID_W)
    dc = np.clip(col[None, :] - col[:, None], 1 - NA_WIN_C, NA_WIN_C - 1) + NA_WIN_C - 1
    col_start = np.clip(col - NA_WIN_C // 2, 0, GRID_W - NA_WIN_C)
    in_win = (col[None, :] >= col_start[:, None]) & (col[None, :] < col_start[:, None] + NA_WIN_C)
    b = rpb[:, dr[:, None, :, None], dc[None, :, None, :]]
    b = jnp.where(in_win[None, None, :, None, :], b.astype(F32), NEG_INF)
    return b.reshape(rpb.shape[0], NA_WIN_R, GRID_W, NA_WIN_R * GRID_W)


def _neighbourhood_attention(pm, pm_c, qn, kn, rpb):
    b, t, _ = pm.shape
    ctx_len = pm_c.shape[1]
    rows = t // GRID_W
    assert rows >= NA_WIN_R and rows * GRID_W == t
    band = NA_WIN_R * GRID_W
    hb = lambda off: (lambda bi, h: (bi, 0, off // HEAD_DIM + h))
    return pl.pallas_call(
        functools.partial(_na_kernel, rows=rows),
        grid=(b, N_HEADS),
        in_specs=[
            pl.BlockSpec((1, t, HEAD_DIM), hb(C_NA_Q)),
            pl.BlockSpec((1, t, HEAD_DIM), hb(C_NA_K)),
            pl.BlockSpec((1, t, HEAD_DIM), hb(C_NA_V)),
            pl.BlockSpec((1, ctx_len, HEAD_DIM), hb(C_NA_K)),
            pl.BlockSpec((1, ctx_len, HEAD_DIM), hb(C_NA_V)),
            pl.BlockSpec((1, HEAD_DIM), lambda bi, h: (0, 0)),
            pl.BlockSpec((1, HEAD_DIM), lambda bi, h: (0, 0)),
            pl.BlockSpec((1, NA_WIN_R, GRID_W, band), lambda bi, h: (h, 0, 0, 0)),
        ],
        out_specs=pl.BlockSpec((1, t, HEAD_DIM), lambda bi, h: (bi, 0, h)),
        out_shape=SDS((b, t, W_BR), BF16),
        scratch_shapes=[pltpu.VMEM((t, HEAD_DIM), BF16)] * 3 + [pltpu.VMEM((ctx_len, HEAD_DIM), BF16)] * 2,
        compiler_params=_cp("parallel", "parallel"),
        name="natten",
    )(pm, pm, pm, pm_c, pm_c, qn, kn, _na_bias_table(rpb))


def _ctx_attn_kernel(q_ref, k_ref, v_ref, qn_ref, kn_ref, o_ref):
    q = _rms(q_ref[0], qn_ref[...]).astype(BF16)
    k = _rms(k_ref[0], kn_ref[...]).astype(BF16)
    s = _dot_nt(q, k) * HEAD_DIM ** -0.5
    p = jnp.exp(s - jnp.max(s, axis=-1, keepdims=True))
    o = _dot(p.astype(BF16), v_ref[0].astype(BF16)) / jnp.sum(p, axis=-1, keepdims=True)
    o_ref[0] = o.astype(o_ref.dtype)


def _dense_attention(pm_c, qn, kn):
    b, ctx_len, _ = pm_c.shape
    hb = lambda off: (lambda bi, h: (bi, 0, off // HEAD_DIM + h))
    return pl.pallas_call(
        _ctx_attn_kernel,
        grid=(b, N_HEADS),
        in_specs=[
            pl.BlockSpec((1, ctx_len, HEAD_DIM), hb(C_NA_Q)),
            pl.BlockSpec((1, ctx_len, HEAD_DIM), hb(C_NA_K)),
            pl.BlockSpec((1, ctx_len, HEAD_DIM), hb(C_NA_V)),
            pl.BlockSpec((1, HEAD_DIM), lambda bi, h: (0, 0)),
            pl.BlockSpec((1, HEAD_DIM), lambda bi, h: (0, 0)),
        ],
        out_specs=pl.BlockSpec((1, ctx_len, HEAD_DIM), lambda bi, h: (bi, 0, h)),
        out_shape=SDS((b, ctx_len, W_BR), BF16),
        compiler_params=_cp("parallel", "parallel"),
        name="ctx_attn",
    )(pm_c, pm_c, pm_c, qn, kn)


def _dn_conv_kernel(x_ref, w_ref, o_ref, *, n_norm_tiles):
    x = x_ref[0]
    t = x.shape[0]
    ridx = lax.broadcasted_iota(jnp.int32, x.shape, 0)
    half = DN_CONV // 2
    acc = x * w_ref[half:half + 1, :]
    for kk in range(DN_CONV):
        d = kk - half
        if d == 0:
            continue
        xs = pltpu.roll(x, (-d) % t, axis=0)
        valid = (ridx < t - d) if d > 0 else (ridx >= -d)
        acc = acc + jnp.where(valid, xs, 0.0) * w_ref[kk:kk + 1, :]
    y = _silu(acc)
    unit = y * lax.rsqrt(jnp.sum(y * y, axis=-1, keepdims=True) + EPS)
    o_ref[0] = jnp.where(pl.program_id(1) < n_norm_tiles, unit, y)


def _dn_conv(pm, conv_w):
    b, t, _ = pm.shape
    n_tiles = 3 * N_HEADS
    return pl.pallas_call(
        functools.partial(_dn_conv_kernel, n_norm_tiles=2 * N_HEADS),
        grid=(b, n_tiles),
        in_specs=[
            pl.BlockSpec((1, t, HEAD_DIM), lambda bi, j: (bi, 0, C_DN_QKV // HEAD_DIM + j)),
            pl.BlockSpec((DN_CONV, HEAD_DIM), lambda bi, j: (0, j)),
        ],
        out_specs=pl.BlockSpec((1, t, HEAD_DIM), lambda bi, j: (bi, 0, j)),
        out_shape=SDS((b, t, 3 * W_BR), F32),
        compiler_params=_cp("parallel", "parallel"),
        name="dn_conv",
    )(pm, conv_w)


def _softplus(x):
    return jnp.maximum(x, 0.0) + jnp.log1p(jnp.exp(-jnp.abs(x)))


def _dn_kernel(qf_ref, kf_ref, vf_ref, qb_ref, kb_ref, vb_ref, abf_ref, abb_ref, abtf_ref, abtb_ref,
               prow_ref, pcol_ref, s0_ref, of_ref, ob_ref, sfin_ref, s_ref):
    c = pl.program_id(1)

    @pl.when(c == 0)
    def _():
        s_ref[...] = s0_ref[0]

    cs = DN_CHUNK
    ii = lax.broadcasted_iota(jnp.int32, (cs, cs), 0)
    jj = lax.broadcasted_iota(jnp.int32, (cs, cs), 1)
    eye = (ii == jj).astype(F32)
    scale = HEAD_DIM ** -0.5
    hi = lax.Precision.HIGHEST

    for d in range(2):
        incl = (jj <= ii) if d == 0 else (jj >= ii)
        strict = (jj < ii) if d == 0 else (jj > ii)
        tri = incl.astype(F32)
        q_ref, k_ref, v_ref = (qf_ref, kf_ref, vf_ref) if d == 0 else (qb_ref, kb_ref, vb_ref)
        ab = (abf_ref if d == 0 else abb_ref)[0]
        abt = (abtf_ref if d == 0 else abtb_ref)[0, 0]
        g_col = -jnp.exp(prow_ref[0:1, :]) * _softplus(ab + prow_ref[1:2, :])
        g_row = -jnp.exp(pcol_ref[:, 0:1]) * _softplus(abt + pcol_ref[:, 1:2])
        gc_col = jnp.dot(tri, g_col, preferred_element_type=F32, precision=hi)
        gc_row = lax.dot_general(g_row, tri, (((1,), (1,)), ((), ())),
                                 preferred_element_type=F32, precision=hi)
        beta_all = jax.nn.sigmoid(ab)
        o_dst = of_ref if d == 0 else ob_ref
        for h in range(N_HEADS):
            col = d * N_HEADS + h
            sl = slice(h * HEAD_DIM, (h + 1) * HEAD_DIM)
            gcc = gc_col[:, col:col + 1]
            gcr = gc_row[col:col + 1, :]
            beta = beta_all[:, 2 * N_HEADS + col:2 * N_HEADS + col + 1]
            decay = jnp.exp(jnp.where(incl, gcc - gcr, -jnp.inf))
            q = q_ref[0, :, sl] * scale
            k = k_ref[0, :, sl]
            v = v_ref[0, :, sl]
            kbeta = k * beta
            kbf = k.astype(BF16)
            a_mat = jnp.where(strict, _dot_nt(kbeta.astype(BF16), kbf) * decay, 0.0)
            inv = eye - a_mat
            pw = a_mat
            for _ in range(5):
                pw = _dot3(pw, pw)
                inv = inv + _dot3(inv, pw)
            eg = jnp.exp(gcc)
            rhs = jnp.concatenate([v * beta, kbeta * eg], axis=1).astype(BF16)
            uw = _dot(inv.astype(BF16), rhs)
            u, w = uw[:, :HEAD_DIM], uw[:, HEAD_DIM:]
            qk = jnp.where(incl, _dot_nt(q.astype(BF16), kbf) * decay, 0.0)
            s_old = s_ref[col]
            ws_qs = _dot(jnp.concatenate([w, q * eg], axis=0).astype(BF16), s_old.astype(BF16))
            v_new = u - ws_qs[:cs]
            o = ws_qs[cs:] + _dot(qk.astype(BF16), v_new.astype(BF16))
            g_last = gcc[cs - 1:cs, :] if d == 0 else gcc[0:1, :]
            kd = k * jnp.exp(g_last - gcc)
            s_ref[col] = s_old * jnp.exp(g_last) + _dot_tn(kd.astype(BF16), v_new.astype(BF16))
            o_dst[0, :, sl] = o

    @pl.when(c == pl.num_programs(1) - 1)
    def _():
        sfin_ref[0] = s_ref[...]


def _deltanet(act, pab, a_log, dt_bias, s0):
    b, t, _ = act.shape
    n = t // DN_CHUNK
    assert n * DN_CHUNK == t
    abt = jnp.swapaxes(pab[..., :N_AB].reshape(b, n, DN_CHUNK, N_AB), 2, 3)
    al, dt = a_log.reshape(-1).astype(F32), dt_bias.reshape(-1).astype(F32)
    prow = jnp.zeros((2, LANES), F32).at[0, :al.size].set(al).at[1, :dt.size].set(dt)
    pcol = jnp.zeros((N_AB, 2), F32).at[:al.size, 0].set(al).at[:dt.size, 1].set(dt)
    fwd = lambda part: (lambda bi, c: (bi, c, part))
    bwd = lambda part: (lambda bi, c: (bi, n - 1 - c, part))
    qkv_spec = lambda imap: pl.BlockSpec((1, DN_CHUNK, W_BR), imap)
    n_state = 2 * N_HEADS
    return pl.pallas_call(
        _dn_kernel,
        grid=(b, n),
        in_specs=[
            qkv_spec(fwd(0)), qkv_spec(fwd(1)), qkv_spec(fwd(2)),
            qkv_spec(bwd(0)), qkv_spec(bwd(1)), qkv_spec(bwd(2)),
            pl.BlockSpec((1, DN_CHUNK, LANES), fwd(0)),
            pl.BlockSpec((1, DN_CHUNK, LANES), bwd(0)),
            pl.BlockSpec((1, 1, N_AB, DN_CHUNK), lambda bi, c: (bi, c, 0, 0)),
            pl.BlockSpec((1, 1, N_AB, DN_CHUNK), lambda bi, c: (bi, n - 1 - c, 0, 0)),
            pl.BlockSpec((2, LANES), lambda bi, c: (0, 0)),
            pl.BlockSpec((N_AB, 2), lambda bi, c: (0, 0)),
            pl.BlockSpec((1, n_state, HEAD_DIM, HEAD_DIM), lambda bi, c: (bi, 0, 0, 0)),
        ],
        out_specs=[
            pl.BlockSpec((1, DN_CHUNK, W_BR), fwd(0)),
            pl.BlockSpec((1, DN_CHUNK, W_BR), bwd(0)),
            pl.BlockSpec((1, n_state, HEAD_DIM, HEAD_DIM), lambda bi, c: (bi, 0, 0, 0)),
        ],
        out_shape=[SDS((b, t, W_BR), F32), SDS((b, t, W_BR), F32),
                   SDS((b, n_state, HEAD_DIM, HEAD_DIM), F32)],
        scratch_shapes=[pltpu.VMEM((n_state, HEAD_DIM, HEAD_DIM), F32)],
        compiler_params=_cp("parallel", "arbitrary"),
        name="deltanet",
    )(act, act, act, act, act, act, pab, pab, abt, abt, prow, pcol, s0)


def _dn_out_kernel(of_ref, ob_ref, z_ref, w_ref, y_ref):
    o = of_ref[0] + ob_ref[0]
    z = z_ref[0]
    for h in range(N_HEADS):
        sl = slice(h * HEAD_DIM, (h + 1) * HEAD_DIM)
        y_ref[0, :, sl] = (_rms(o[:, sl], w_ref[...]) * _silu(z[:, sl])).astype(y_ref.dtype)


def _dn_out(o_f, o_b, pm, onorm):
    b, t, _ = o_f.shape
    tm = _tile(t, 1024)
    return pl.pallas_call(
        _dn_out_kernel,
        grid=(b, t // tm),
        in_specs=[
            pl.BlockSpec((1, tm, W_BR), lambda bi, i: (bi, i, 0)),
            pl.BlockSpec((1, tm, W_BR), lambda bi, i: (bi, i, 0)),
            pl.BlockSpec((1, tm, W_BR), lambda bi, i: (bi, i, C_DN_Z // W_BR)),
            pl.BlockSpec((1, HEAD_DIM), lambda bi, i: (0, 0)),
        ],
        out_specs=pl.BlockSpec((1, tm, W_BR), lambda bi, i: (bi, i, 0)),
        out_shape=SDS((b, t, W_BR), BF16),
        compiler_params=_cp("parallel", "parallel"),
        name="dn_out",
    )(o_f, o_b, pm, onorm)


def _fourier_kernel(x_ref, c1_ref, s1_ref, c2_ref, s2_ref, cc_ref, sc_ref, twc_ref, tws_ref, o_ref,
                    bre_ref, bim_ref, *, n1, n2, scale):
    c1, s1 = c1_ref[...].astype(BF16), s1_ref[...].astype(BF16)
    c2, s2 = c2_ref[...].astype(BF16), s2_ref[...].astype(BF16)
    cc, sc = cc_ref[...].astype(BF16), sc_ref[...].astype(BF16)

    def stage1(t2, carry):
        xs = x_ref[0, pl.ds(t2, n1, stride=n2), :].astype(BF16)
        ar = _dot(c1, xs)
        ai = -_dot(s1, xs)
        cw, sw = twc_ref[t2], tws_ref[t2]
        bre_ref[pl.ds(t2, n1, stride=n2), :] = ar * cw + ai * sw
        bim_ref[pl.ds(t2, n1, stride=n2), :] = ai * cw - ar * sw
        return carry

    lax.fori_loop(0, n2, stage1, 0, unroll=4)

    def stage2(f1, carry):
        r0 = pl.multiple_of(f1 * n2, n2)
        br = bre_ref[pl.ds(r0, n2), :].astype(BF16)
        bi = bim_ref[pl.ds(r0, n2), :].astype(BF16)
        yr = _dot(c2, br) + _dot(s2, bi)
        yi = _dot(c2, bi) - _dot(s2, br)
        z = _dot(yr.astype(BF16), cc) + _dot(yi.astype(BF16), sc)
        o_ref[0, pl.ds(f1, n2, stride=n1), :] = z * scale
        return carry

    lax.fori_loop(0, n1, stage2, 0, unroll=4)


def _dft_mats(n):
    ang = 2.0 * np.pi * ((np.arange(n)[:, None] * np.arange(n)[None, :]) % n) / n
    return jnp.asarray(np.cos(ang), F32), jnp.asarray(np.sin(ang), F32)


def _fourier(pm):
    b, t, _ = pm.shape
    n1 = 1 << ((t.bit_length() - 1 + 1) // 2)
    n2 = t // n1
    assert n1 * n2 == t and n1 % 8 == 0 and n2 % 8 == 0
    c1, s1 = _dft_mats(n1)
    c2, s2 = _dft_mats(n2)
    cc, sc = _dft_mats(HEAD_DIM)
    ang = 2.0 * np.pi * ((np.arange(n2)[:, None] * np.arange(n1)[None, :]) % t) / t
    twc = jnp.broadcast_to(jnp.asarray(np.cos(ang), F32)[:, :, None], (n2, n1, HEAD_DIM))
    tws = jnp.broadcast_to(jnp.asarray(np.sin(ang), F32)[:, :, None], (n2, n1, HEAD_DIM))
    const = lambda shape: pl.BlockSpec(shape, lambda bi, g: (0,) * len(shape))
    return pl.pallas_call(
        functools.partial(_fourier_kernel, n1=n1, n2=n2, scale=float((t * HEAD_DIM) ** -0.5)),
        grid=(b, N_HEADS),
        in_specs=[
            pl.BlockSpec((1, t, HEAD_DIM), lambda bi, g: (bi, 0, C_FT // HEAD_DIM + g)),
            const((n1, n1)), const((n1, n1)), const((n2, n2)), const((n2, n2)),
            const((HEAD_DIM, HEAD_DIM)), const((HEAD_DIM, HEAD_DIM)),
            const((n2, n1, HEAD_DIM)), const((n2, n1, HEAD_DIM)),
        ],
        out_specs=pl.BlockSpec((1, t, HEAD_DIM), lambda bi, g: (bi, 0, g)),
        out_shape=SDS((b, t, W_BR), F32),
        scratch_shapes=[pltpu.VMEM((t, HEAD_DIM), F32)] * 2,
        compiler_params=_cp("parallel", "parallel"),
        name="fourier",
    )(pm, c1, s1, c2, s2, cc, sc, twc, tws)


def _sgu_kernel(u_ref, v_ref, vn_ref, ws_ref, bs_ref, o_ref):
    u = jax.nn.gelu(u_ref[0])
    v = jax.nn.gelu(v_ref[0])
    mu = jnp.mean(v, axis=-1, keepdims=True)
    var = jnp.mean(jnp.square(v - mu), axis=-1, keepdims=True)
    vb = ((v - mu) * lax.rsqrt(var + EPS) * vn_ref[...]).astype(BF16)
    for ch in range(u.shape[0] // SG_CHUNK):
        rs = slice(ch * SG_CHUNK, (ch + 1) * SG_CHUNK)
        for g in range(N_HEADS):
            sl = slice(g * HEAD_DIM, (g + 1) * HEAD_DIM)
            mix = _dot(ws_ref[g], vb[rs, sl]) + bs_ref[:, g:g + 1]
            o_ref[0, rs, sl] = (u[rs, sl] * mix).astype(o_ref.dtype)


def _spatial_gating(pm, ws, bs_t, vnorm):
    b, t, _ = pm.shape
    tm = _tile(t, 4 * SG_CHUNK, SG_CHUNK)
    assert tm % SG_CHUNK == 0
    return pl.pallas_call(
        _sgu_kernel,
        grid=(b, t // tm),
        in_specs=[
            pl.BlockSpec((1, tm, W_BR), lambda bi, i: (bi, i, C_SG_U // W_BR)),
            pl.BlockSpec((1, tm, W_BR), lambda bi, i: (bi, i, C_SG_V // W_BR)),
            pl.BlockSpec((1, W_BR), lambda bi, i: (0, 0)),
            pl.BlockSpec((N_HEADS, SG_CHUNK, SG_CHUNK), lambda bi, i: (0, 0, 0)),
            pl.BlockSpec((SG_CHUNK, N_HEADS), lambda bi, i: (0, 0)),
        ],
        out_specs=pl.BlockSpec((1, tm, W_BR), lambda bi, i: (bi, i, 0)),
        out_shape=SDS((b, t, W_BR), BF16),
        compiler_params=_cp("parallel", "parallel"),
        name="sgu",
    )(pm, pm, vnorm, ws, bs_t)


def _merge_kernel(h_ref, y0_ref, y1_ref, y2_ref, y3_ref, wg_ref, bg_ref, wb_ref, o_ref):
    h = h_ref[0]
    acc = None
    for i, y_ref in enumerate((y0_ref, y1_ref, y2_ref, y3_ref)):
        gate = jax.nn.sigmoid(_dot(h, wg_ref[i]) + bg_ref[i])
        term = gate * _dot(y_ref[0].astype(BF16), wb_ref[i])
        acc = term if acc is None else acc + term
    o_ref[0] = acc.astype(o_ref.dtype)


def _merge(h, ys, wg, bg, wb):
    bx, t, d = h.shape
    tm, tn = _tile(t, 1024), _tile(d, 512, LANES)
    y_spec = pl.BlockSpec((1, tm, W_BR), lambda b, i, j: (b, i, 0))
    return pl.pallas_call(
        _merge_kernel,
        grid=(bx, t // tm, d // tn),
        in_specs=[
            pl.BlockSpec((1, tm, d), lambda b, i, j: (b, i, 0)),
            y_spec, y_spec, y_spec, y_spec,
            pl.BlockSpec((N_BRANCH, d, tn), lambda b, i, j: (0, 0, j)),
            pl.BlockSpec((N_BRANCH, 1, tn), lambda b, i, j: (0, 0, j)),
            pl.BlockSpec((N_BRANCH, W_BR, tn), lambda b, i, j: (0, 0, j)),
        ],
        out_specs=pl.BlockSpec((1, tm, tn), lambda b, i, j: (b, i, j)),
        out_shape=SDS((bx, t, d), BF16),
        compiler_params=_cp("parallel", "parallel", "arbitrary"),
        name="merge",
    )(h, *ys, wg, bg.reshape(N_BRANCH, 1, d), wb)


def _resid_kernel(a_ref, w_ref, x_ref, g_ref, o_ref, *, row):
    o_ref[0] = x_ref[0] + _mod_row(g_ref, row) * _dot(a_ref[0], w_ref[...])


def _resid_proj(a, w, x, mod, gate_blk, row):
    bx, t, k = a.shape
    d = w.shape[1]
    tm, tn = _tile(t, 1024 if k <= 2048 else 512), _tile(d, 512, LANES)
    nj = d // tn
    return pl.pallas_call(
        functools.partial(_resid_kernel, row=row),
        grid=(bx, t // tm, nj),
        in_specs=[
            pl.BlockSpec((1, tm, k), lambda b, i, j: (b, i, 0)),
            pl.BlockSpec((k, tn), lambda b, i, j: (0, j)),
            pl.BlockSpec((1, tm, tn), lambda b, i, j: (b, i, j)),
            pl.BlockSpec((MOD_ROWS, tn), lambda b, i, j: (0, gate_blk * nj + j)),
        ],
        out_specs=pl.BlockSpec((1, tm, tn), lambda b, i, j: (b, i, j)),
        out_shape=SDS((bx, t, d), F32),
        compiler_params=_cp("parallel", "parallel", "arbitrary"),
        name="resid_proj",
    )(a, w, x, mod)


def _ffn_up_kernel(x_ref, nw_ref, sh_ref, sc_ref, w1_ref, w3_ref, o_ref, hs_ref, *, row):
    @pl.when(pl.program_id(2) == 0)
    def _():
        hs_ref[...] = _norm_mod(x_ref[0], nw_ref, sh_ref, sc_ref, row)

    h = hs_ref[...]
    o_ref[0] = (_silu(_dot(h, w1_ref[...])) * _dot(h, w3_ref[...])).astype(o_ref.dtype)


def _ffn_up(x, nw, mod, w1, w3, row):
    bx, t, d = x.shape
    f = w1.shape[1]
    tm, tn = _tile(t, 1024), _tile(f, 512, LANES)
    return pl.pallas_call(
        functools.partial(_ffn_up_kernel, row=row),
        grid=(bx, t // tm, f // tn),
        in_specs=[
            pl.BlockSpec((1, tm, d), lambda b, i, j: (b, i, 0)),
            pl.BlockSpec((1, d), lambda b, i, j: (0, 0)),
            pl.BlockSpec((MOD_ROWS, d), lambda b, i, j: (0, 3)),
            pl.BlockSpec((MOD_ROWS, d), lambda b, i, j: (0, 4)),
            pl.BlockSpec((d, tn), lambda b, i, j: (0, j)),
            pl.BlockSpec((d, tn), lambda b, i, j: (0, j)),
        ],
        out_specs=pl.BlockSpec((1, tm, tn), lambda b, i, j: (b, i, j)),
        out_shape=SDS((bx, t, f), BF16),
        scratch_shapes=[pltpu.VMEM((tm, d), BF16)],
        compiler_params=_cp("parallel", "parallel", "arbitrary"),
        name="ffn_up",
    )(x, nw, mod, mod, w1, w3)


def _mixers(pm, pab, pm_c, pab_c, p, need_ctx):
    b = pm.shape[0]
    y_na = _neighbourhood_attention(pm, pm_c, p["qn"], p["kn"], p["rpb"])
    zero = jnp.zeros((b, 2 * N_HEADS, HEAD_DIM, HEAD_DIM), F32)
    of_c, ob_c, s_ctx = _deltanet(_dn_conv(pm_c, p["conv"]), pab_c, p["a_log"], p["dt_bias"], zero)
    of, ob, _ = _deltanet(_dn_conv(pm, p["conv"]), pab, p["a_log"], p["dt_bias"], s_ctx)
    ys = (y_na, _dn_out(of, ob, pm, p["onorm"]), _fourier(pm), _spatial_gating(pm, p["sg_w"], p["sg_bt"], p["sg_vn"]))
    if not need_ctx:
        return ys, None
    ys_c = (_dense_attention(pm_c, p["qn"], p["kn"]), _dn_out(of_c, ob_c, pm_c, p["onorm"]), _fourier(pm_c),
            _spatial_gating(pm_c, p["sg_w"], p["sg_bt"], p["sg_vn"]))
    return ys, ys_c


def kernel(x, c, ctx, c_ctx, w_ada, b_ada, norm1_w, norm2_w, w_in, na_qnorm, na_knorm, na_rpb, dn_conv, dn_a_log, dn_dt_bias, dn_onorm, sg_w, sg_b, sg_vnorm, w_gate, b_gate, w_branch, w_out, ffn_w1, ffn_w3, ffn_w2):
    bsz, seq, d = x.shape
    ctx_len = ctx.shape[1]
    depth = w_ada.shape[0]
    assert bsz + 1 <= MOD_ROWS and d % LANES == 0
    ctx_row = bsz

    cc = jnp.zeros((MOD_ROWS, d), F32).at[:bsz].set(c).at[ctx_row].set(c_ctx)
    mod_all = _ada_mod(cc, w_ada, b_ada)
    xc = ctx.reshape(1, bsz * ctx_len, d)

    for l in range(depth):
        need_ctx = l < depth - 1
        mod = mod_all[l]
        w_main = jnp.concatenate([w_in[l][:, :C_AB_SRC], w_in[l][:, C_AB_SRC + N_AB:]], axis=1).astype(BF16)
        w_ab = jnp.pad(w_in[l][:, C_AB_SRC:C_AB_SRC + N_AB], ((0, 0), (0, LANES - N_AB))).astype(BF16)
        p = dict(qn=na_qnorm[l][None], kn=na_knorm[l][None], rpb=na_rpb[l], conv=dn_conv[l], a_log=dn_a_log[l],
                 dt_bias=dn_dt_bias[l], onorm=dn_onorm[l][None], sg_w=sg_w[l].astype(BF16), sg_bt=sg_b[l].T,
                 sg_vn=sg_vnorm[l][None])
        wg, wb, wo = w_gate[l].astype(BF16), w_branch[l].astype(BF16), w_out[l].astype(BF16)
        w1, w3, w2 = ffn_w1[l].astype(BF16), ffn_w3[l].astype(BF16), ffn_w2[l].astype(BF16)
        n1w, n2w = norm1_w[l][None], norm2_w[l][None]

        pm, pab, h = _in_proj(x, n1w, mod, w_main, w_ab, None)
        pm_c, pab_c, h_c = _in_proj(xc, n1w, mod, w_main, w_ab, ctx_row)
        pm_c = pm_c.reshape(bsz, ctx_len, N_MAIN)
        pab_c = pab_c.reshape(bsz, ctx_len, LANES)
        ys, ys_c = _mixers(pm, pab, pm_c, pab_c, p, need_ctx)

        merged = _merge(h, ys, wg, b_gate[l], wb)
        x = _resid_proj(merged, wo, x, mod, 2, None)
        x = _resid_proj(_ffn_up(x, n2w, mod, w1, w3, None), w2, x, mod, 5, None)
        if need_ctx:
            ys_c = tuple(y.reshape(1, bsz * ctx_len, W_BR) for y in ys_c)
            merged_c = _merge(h_c, ys_c, wg, b_gate[l], wb)
            xc = _resid_proj(merged_c, wo, xc, mod, 2, ctx_row)
            xc = _resid_proj(_ffn_up(xc, n2w, mod, w1, w3, ctx_row), w2, xc, mod, 5, ctx_row)
    return x
```

```python
import functools

import numpy as np
import jax
import jax.numpy as jnp
from jax import lax
from jax.experimental import pallas as pl
from jax.experimental.pallas import tpu as pltpu

F32 = jnp.float32
BF16 = jnp.bfloat16
SDS = jax.ShapeDtypeStruct

LANES = 128
HEAD_DIM = 128
N_HEADS = 4
W_BR = N_HEADS * HEAD_DIM
N_BRANCH = 4
GRID_W = 64
NA_WIN_R = 8
NA_WIN_C = 16
NA_ROWS_PER_STEP = 4
DN_CHUNK = 64
DN_INV_BASE = 8
DN_CONV = 5
SG_CHUNK = 128
FT_UNROLL = 8
N_MOD = 6
EPS = 1e-6
NEG_INF = -1e30
MOD_ROWS = 8
N_AB = 4 * N_HEADS
VMEM_LIMIT_BYTES = 56 * 1024 * 1024

C_NA_Q, C_NA_K, C_NA_V = 0, W_BR, 2 * W_BR
C_DN_QKV = 3 * W_BR
C_DN_Z = 6 * W_BR
C_FT = 7 * W_BR
C_SG_U = 8 * W_BR
C_SG_V = 9 * W_BR
N_MAIN = 10 * W_BR
C_AB_SRC = 7 * W_BR


def _cp(*sem):
    return pltpu.CompilerParams(dimension_semantics=sem, vmem_limit_bytes=VMEM_LIMIT_BYTES)


def _tile(n, pref, mult=8):
    if n <= pref:
        return n
    for t in range(pref - pref % mult, 0, -mult):
        if n % t == 0:
            return t
    return n


def _dot(a, b):
    return jnp.dot(a, b, preferred_element_type=F32)


def _dot_nt(a, b):
    return lax.dot_general(a, b, (((1,), (1,)), ((), ())), preferred_element_type=F32)


def _dot_tn(a, b):
    return lax.dot_general(a, b, (((0,), (0,)), ((), ())), preferred_element_type=F32)


def _silu(x):
    return x * jax.nn.sigmoid(x)


def _rms(x, w):
    return x * lax.rsqrt(jnp.mean(x * x, axis=-1, keepdims=True) + EPS) * w


def _mod_row(ref, row):
    b = pl.program_id(0) if row is None else row
    return ref[pl.ds(b, 1), :]


def _norm_mod(x, nw_ref, sh_ref, sc_ref, row):
    return (_rms(x, nw_ref[...]) * (1.0 + _mod_row(sc_ref, row)) + _mod_row(sh_ref, row)).astype(BF16)


def _ada_kernel(c_ref, w_ref, b_ref, o_ref):
    s = _silu(c_ref[...]).astype(BF16)
    o_ref[0] = _dot(s, w_ref[0].astype(BF16)) + b_ref[0]


def _ada_mod(cc, w_ada, b_ada):
    depth, d, n = w_ada.shape
    tn = _tile(n, 1024, LANES)
    return pl.pallas_call(
        _ada_kernel,
        grid=(depth, n // tn),
        in_specs=[
            pl.BlockSpec((MOD_ROWS, d), lambda l, j: (0, 0)),
            pl.BlockSpec((1, d, tn), lambda l, j: (l, 0, j)),
            pl.BlockSpec((1, 1, tn), lambda l, j: (l, 0, j)),
        ],
        out_specs=pl.BlockSpec((1, MOD_ROWS, tn), lambda l, j: (l, 0, j)),
        out_shape=SDS((depth, MOD_ROWS, n), F32),
        compiler_params=_cp("parallel", "parallel"),
        name="ada_mod",
    )(cc, w_ada, b_ada.reshape(depth, 1, n))


def _in_proj_kernel(x_ref, nw_ref, sh_ref, sc_ref, w_ref, wab_ref, o_ref, oab_ref, h_ref, hs_ref, *, row):
    @pl.when(pl.program_id(2) == 0)
    def _():
        h = _norm_mod(x_ref[0], nw_ref, sh_ref, sc_ref, row)
        hs_ref[...] = h
        h_ref[0] = h
        oab_ref[0] = _dot(h, wab_ref[...])

    o_ref[0] = _dot(hs_ref[...], w_ref[...])


def _in_proj(x, nw, mod, w_main, w_ab, row):
    bx, t, d = x.shape
    n = w_main.shape[1]
    tm, tn = _tile(t, 1024), _tile(n, 1024, LANES)
    return pl.pallas_call(
        functools.partial(_in_proj_kernel, row=row),
        grid=(bx, t // tm, n // tn),
        in_specs=[
            pl.BlockSpec((1, tm, d), lambda b, i, j: (b, i, 0)),
            pl.BlockSpec((1, d), lambda b, i, j: (0, 0)),
            pl.BlockSpec((MOD_ROWS, d), lambda b, i, j: (0, 0)),
            pl.BlockSpec((MOD_ROWS, d), lambda b, i, j: (0, 1)),
            pl.BlockSpec((d, tn), lambda b, i, j: (0, j)),
            pl.BlockSpec((d, LANES), lambda b, i, j: (0, 0)),
        ],
        out_specs=[
            pl.BlockSpec((1, tm, tn), lambda b, i, j: (b, i, j)),
            pl.BlockSpec((1, tm, LANES), lambda b, i, j: (b, i, 0)),
            pl.BlockSpec((1, tm, d), lambda b, i, j: (b, i, 0)),
        ],
        out_shape=[SDS((bx, t, n), F32), SDS((bx, t, LANES), F32), SDS((bx, t, d), BF16)],
        scratch_shapes=[pltpu.VMEM((tm, d), BF16)],
        compiler_params=_cp("parallel", "parallel", "arbitrary"),
        name="in_proj",
    )(x, nw, mod, mod, w_main, w_ab)


def _na_kernel(q_ref, k_ref, v_ref, kc_ref, vc_ref, qn_ref, kn_ref, bias_ref, o_ref,
               qs, ks, vs, kcs, vcs, *, rows):
    qs[...] = (_rms(q_ref[0], qn_ref[...]) * HEAD_DIM ** -0.5).astype(BF16)
    ks[...] = _rms(k_ref[0], kn_ref[...]).astype(BF16)
    vs[...] = v_ref[0].astype(BF16)
    kcs[...] = _rms(kc_ref[0], kn_ref[...]).astype(BF16)
    vcs[...] = vc_ref[0].astype(BF16)
    band = NA_WIN_R * GRID_W
    nr = NA_ROWS_PER_STEP

    def body(g, carry):
        rr = [g * nr + i for i in range(nr)]
        start = [jnp.clip(r - NA_WIN_R // 2, 0, rows - NA_WIN_R) for r in rr]
        r0 = [pl.multiple_of(r * GRID_W, GRID_W) for r in rr]
        k0 = [pl.multiple_of(s * GRID_W, GRID_W) for s in start]
        q = [qs[pl.ds(x, GRID_W), :] for x in r0]
        kb = [ks[pl.ds(x, band), :] for x in k0]
        vb = [vs[pl.ds(x, band), :] for x in k0]
        bias = [bias_ref[0, r - s] for r, s in zip(rr, start)]
        kc, vc = kcs[...], vcs[...]
        s_loc = [_dot_nt(q[i], kb[i]) + bias[i] for i in range(nr)]
        s_ctx = [_dot_nt(q[i], kc) for i in range(nr)]
        m = [jnp.maximum(jnp.max(s_loc[i], axis=-1, keepdims=True), jnp.max(s_ctx[i], axis=-1, keepdims=True))
             for i in range(nr)]
        p = [jnp.exp(s_loc[i] - m[i]) for i in range(nr)]
        pc = [jnp.exp(s_ctx[i] - m[i]) for i in range(nr)]
        denom = [jnp.sum(p[i], axis=-1, keepdims=True) + jnp.sum(pc[i], axis=-1, keepdims=True) for i in range(nr)]
        o = [_dot(p[i].astype(BF16), vb[i]) + _dot(pc[i].astype(BF16), vc) for i in range(nr)]
        for i in range(nr):
            o_ref[0, pl.ds(r0[i], GRID_W), :] = (o[i] / denom[i]).astype(o_ref.dtype)
        return carry

    lax.fori_loop(0, rows // nr, body, 0)


def _na_bias_table(rpb):
    col = np.arange(GRID_W)
    dc = np.clip(col[None, :] - col[:, None], 1 - NA_WIN_C, NA_WIN_C - 1) + NA_WIN_C - 1
    col_start = np.clip(col - NA_WIN_C // 2, 0, GRID_W - NA_WIN_C)
    in_win = (col[None, :] >= col_start[:, None]) & (col[None, :] < col_start[:, None] + NA_WIN_C)
    rows = jnp.stack([rpb[:, NA_WIN_R - 1 - typ:2 * NA_WIN_R - 1 - typ, :] for typ in range(NA_WIN_R)], axis=1)
    onehot = (dc[None] == np.arange(2 * NA_WIN_C - 1)[:, None, None]).astype(np.float32)
    b = jnp.einsum('htjc,cqk->htqjk', rows.astype(F32), onehot, precision=lax.Precision.HIGHEST)
    b = jnp.where(in_win[None, None, :, None, :], b, NEG_INF)
    return b.reshape(rpb.shape[0], NA_WIN_R, GRID_W, NA_WIN_R * GRID_W)


def _neighbourhood_attention(pm, pm_c, qn, kn, rpb):
    b, t, _ = pm.shape
    ctx_len = pm_c.shape[1]
    rows = t // GRID_W
    assert rows >= NA_WIN_R and rows * GRID_W == t and rows % NA_ROWS_PER_STEP == 0
    band = NA_WIN_R * GRID_W
    hb = lambda off: (lambda bi, h: (bi, 0, off // HEAD_DIM + h))
    return pl.pallas_call(
        functools.partial(_na_kernel, rows=rows),
        grid=(b, N_HEADS),
        in_specs=[
            pl.BlockSpec((1, t, HEAD_DIM), hb(C_NA_Q)),
            pl.BlockSpec((1, t, HEAD_DIM), hb(C_NA_K)),
            pl.BlockSpec((1, t, HEAD_DIM), hb(C_NA_V)),
            pl.BlockSpec((1, ctx_len, HEAD_DIM), hb(C_NA_K)),
            pl.BlockSpec((1, ctx_len, HEAD_DIM), hb(C_NA_V)),
            pl.BlockSpec((1, HEAD_DIM), lambda bi, h: (0, 0)),
            pl.BlockSpec((1, HEAD_DIM), lambda bi, h: (0, 0)),
            pl.BlockSpec((1, NA_WIN_R, GRID_W, band), lambda bi, h: (h, 0, 0, 0)),
        ],
        out_specs=pl.BlockSpec((1, t, HEAD_DIM), lambda bi, h: (bi, 0, h)),
        out_shape=SDS((b, t, W_BR), BF16),
        scratch_shapes=[pltpu.VMEM((t, HEAD_DIM), BF16)] * 3 + [pltpu.VMEM((ctx_len, HEAD_DIM), BF16)] * 2,
        compiler_params=_cp("parallel", "parallel"),
        name="natten",
    )(pm, pm, pm, pm_c, pm_c, qn, kn, _na_bias_table(rpb))


def _ctx_attn_kernel(q_ref, k_ref, v_ref, qn_ref, kn_ref, o_ref):
    q = _rms(q_ref[0], qn_ref[...]).astype(BF16)
    k = _rms(k_ref[0], kn_ref[...]).astype(BF16)
    s = _dot_nt(q, k) * HEAD_DIM ** -0.5
    p = jnp.exp(s - jnp.max(s, axis=-1, keepdims=True))
    o = _dot(p.astype(BF16), v_ref[0].astype(BF16)) / jnp.sum(p, axis=-1, keepdims=True)
    o_ref[0] = o.astype(o_ref.dtype)


def _dense_attention(pm_c, qn, kn):
    b, ctx_len, _ = pm_c.shape
    hb = lambda off: (lambda bi, h: (bi, 0, off // HEAD_DIM + h))
    return pl.pallas_call(
        _ctx_attn_kernel,
        grid=(b, N_HEADS),
        in_specs=[
            pl.BlockSpec((1, ctx_len, HEAD_DIM), hb(C_NA_Q)),
            pl.BlockSpec((1, ctx_len, HEAD_DIM), hb(C_NA_K)),
            pl.BlockSpec((1, ctx_len, HEAD_DIM), hb(C_NA_V)),
            pl.BlockSpec((1, HEAD_DIM), lambda bi, h: (0, 0)),
            pl.BlockSpec((1, HEAD_DIM), lambda bi, h: (0, 0)),
        ],
        out_specs=pl.BlockSpec((1, ctx_len, HEAD_DIM), lambda bi, h: (bi, 0, h)),
        out_shape=SDS((b, ctx_len, W_BR), BF16),
        compiler_params=_cp("parallel", "parallel"),
        name="ctx_attn",
    )(pm_c, pm_c, pm_c, qn, kn)


def _dn_conv_kernel(x_ref, w_ref, o_ref, *, n_norm_tiles):
    x = x_ref[0]
    t = x.shape[0]
    ridx = lax.broadcasted_iota(jnp.int32, x.shape, 0)
    half = DN_CONV // 2
    acc = x * w_ref[half:half + 1, :]
    for kk in range(DN_CONV):
        d = kk - half
        if d == 0:
            continue
        xs = pltpu.roll(x, (-d) % t, axis=0)
        valid = (ridx < t - d) if d > 0 else (ridx >= -d)
        acc = acc + jnp.where(valid, xs, 0.0) * w_ref[kk:kk + 1, :]
    y = _silu(acc)
    unit = y * lax.rsqrt(jnp.sum(y * y, axis=-1, keepdims=True) + EPS)
    o_ref[0] = jnp.where(pl.program_id(1) < n_norm_tiles, unit, y)


def _dn_conv(pm, conv_w):
    b, t, _ = pm.shape
    n_tiles = 3 * N_HEADS
    return pl.pallas_call(
        functools.partial(_dn_conv_kernel, n_norm_tiles=2 * N_HEADS),
        grid=(b, n_tiles),
        in_specs=[
            pl.BlockSpec((1, t, HEAD_DIM), lambda bi, j: (bi, 0, C_DN_QKV // HEAD_DIM + j)),
            pl.BlockSpec((DN_CONV, HEAD_DIM), lambda bi, j: (0, j)),
        ],
        out_specs=pl.BlockSpec((1, t, HEAD_DIM), lambda bi, j: (bi, 0, j)),
        out_shape=SDS((b, t, 3 * W_BR), F32),
        compiler_params=_cp("parallel", "parallel"),
        name="dn_conv",
    )(pm, conv_w)


def _softplus(x):
    return jnp.maximum(x, 0.0) + jnp.log1p(jnp.exp(-jnp.abs(x)))


def _block_masks(n):
    ii = lax.broadcasted_iota(jnp.int32, (n, n), 0)
    jj = lax.broadcasted_iota(jnp.int32, (n, n), 1)
    same = lambda s: (ii // s) == (jj // s)
    masks, s = [same(DN_INV_BASE)], DN_INV_BASE
    while s < n:
        masks.append(same(2 * s) & jnp.logical_not(same(s)))
        s *= 2
    return masks


def _unit_triangular_inverses(a_list, eye, masks):
    mm = lambda x, y: _dot(x.astype(BF16), y.astype(BF16))
    n = len(a_list)
    pw = [jnp.where(masks[0], a, 0.0) for a in a_list]
    inv = [eye - p for p in pw]
    s = 2
    while s < DN_INV_BASE:
        pw = [mm(p, p) for p in pw]
        inv = [inv[i] + mm(inv[i], pw[i]) for i in range(n)]
        s *= 2
    for m in masks[1:]:
        t = [mm(jnp.where(m, a_list[i], 0.0), inv[i]) for i in range(n)]
        inv = [inv[i] - mm(inv[i], t[i]) for i in range(n)]
    return inv


def _dn_kernel(qf_ref, kf_ref, vf_ref, qb_ref, kb_ref, vb_ref, abf_ref, abb_ref, abtf_ref, abtb_ref,
               prow_ref, pcol_ref, s0_ref, of_ref, ob_ref, sfin_ref, s_ref):
    c = pl.program_id(1)

    @pl.when(c == 0)
    def _():
        s_ref[...] = s0_ref[0]

    cs = DN_CHUNK
    ii = lax.broadcasted_iota(jnp.int32, (cs, cs), 0)
    jj = lax.broadcasted_iota(jnp.int32, (cs, cs), 1)
    eye = (ii == jj).astype(F32)
    blk_masks = _block_masks(cs)
    scale = HEAD_DIM ** -0.5
    hi = lax.Precision.HIGHEST

    nch = 2 * N_HEADS
    chains = [(d, h) for d in range(2) for h in range(N_HEADS)]
    incl, strict, gc_col, gc_row, beta_all = [], [], [], [], []
    for d in range(2):
        incl.append((jj <= ii) if d == 0 else (jj >= ii))
        strict.append((jj < ii) if d == 0 else (jj > ii))
        tri = incl[d].astype(F32)
        ab = (abf_ref if d == 0 else abb_ref)[0]
        abt = (abtf_ref if d == 0 else abtb_ref)[0, 0]
        g_col = -jnp.exp(prow_ref[0:1, :]) * _softplus(ab + prow_ref[1:2, :])
        g_row = -jnp.exp(pcol_ref[:, 0:1]) * _softplus(abt + pcol_ref[:, 1:2])
        gc_col.append(jnp.dot(tri, g_col, preferred_element_type=F32, precision=hi))
        gc_row.append(lax.dot_general(g_row, tri, (((1,), (1,)), ((), ())),
                                      preferred_element_type=F32, precision=hi))
        beta_all.append(jax.nn.sigmoid(ab))
    qkv_refs = ((qf_ref, kf_ref, vf_ref), (qb_ref, kb_ref, vb_ref))
    sl = [slice(h * HEAD_DIM, (h + 1) * HEAD_DIM) for h in range(N_HEADS)]
    q = [qkv_refs[d][0][0, :, sl[h]] * scale for d, h in chains]
    k = [qkv_refs[d][1][0, :, sl[h]] for d, h in chains]
    v = [qkv_refs[d][2][0, :, sl[h]] for d, h in chains]
    s_old = [s_ref[j] for j in range(nch)]
    gcc = [gc_col[d][:, j:j + 1] for j, (d, h) in enumerate(chains)]
    gcr = [gc_row[d][j:j + 1, :] for j, (d, h) in enumerate(chains)]
    beta = [beta_all[d][:, nch + j:nch + j + 1] for j, (d, h) in enumerate(chains)]
    decay = [jnp.exp(jnp.where(incl[d], gcc[j] - gcr[j], -jnp.inf)) for j, (d, h) in enumerate(chains)]
    eg = [jnp.exp(x) for x in gcc]
    g_last = [gcc[j][cs - 1:cs, :] if d == 0 else gcc[j][0:1, :] for j, (d, h) in enumerate(chains)]
    kbf = [x.astype(BF16) for x in k]
    kbeta = [k[j] * beta[j] for j in range(nch)]
    a_mat = [jnp.where(strict[d], _dot_nt(kbeta[j].astype(BF16), kbf[j]) * decay[j], 0.0)
             for j, (d, h) in enumerate(chains)]
    qk = [jnp.where(incl[d], _dot_nt(q[j].astype(BF16), kbf[j]) * decay[j], 0.0) for j, (d, h) in enumerate(chains)]
    inv = _unit_triangular_inverses(a_mat, eye, blk_masks)
    rhs = [jnp.concatenate([v[j] * beta[j], kbeta[j] * eg[j]], axis=1).astype(BF16) for j in range(nch)]
    uw = [_dot(inv[j].astype(BF16), rhs[j]) for j in range(nch)]
    ws_qs = [_dot(jnp.concatenate([uw[j][:, HEAD_DIM:], q[j] * eg[j]], axis=0).astype(BF16), s_old[j].astype(BF16))
             for j in range(nch)]
    v_new = [uw[j][:, :HEAD_DIM] - ws_qs[j][:cs] for j in range(nch)]
    vnb = [x.astype(BF16) for x in v_new]
    o = [ws_qs[j][cs:] + _dot(qk[j].astype(BF16), vnb[j]) for j in range(nch)]
    kd = [(k[j] * jnp.exp(g_last[j] - gcc[j])).astype(BF16) for j in range(nch)]
    s_new = [s_old[j] * jnp.exp(g_last[j]) + _dot_tn(kd[j], vnb[j]) for j in range(nch)]
    for j, (d, h) in enumerate(chains):
        s_ref[j] = s_new[j]
        (of_ref if d == 0 else ob_ref)[0, :, sl[h]] = o[j]

    @pl.when(c == pl.num_programs(1) - 1)
    def _():
        sfin_ref[0] = s_ref[...]


def _deltanet(act, pab, a_log, dt_bias, s0):
    b, t, _ = act.shape
    n = t // DN_CHUNK
    assert n * DN_CHUNK == t
    abt = jnp.swapaxes(pab[..., :N_AB].reshape(b, n, DN_CHUNK, N_AB), 2, 3)
    al, dt = a_log.reshape(-1).astype(F32), dt_bias.reshape(-1).astype(F32)
    prow = jnp.zeros((2, LANES), F32).at[0, :al.size].set(al).at[1, :dt.size].set(dt)
    pcol = jnp.zeros((N_AB, 2), F32).at[:al.size, 0].set(al).at[:dt.size, 1].set(dt)
    fwd = lambda part: (lambda bi, c: (bi, c, part))
    bwd = lambda part: (lambda bi, c: (bi, n - 1 - c, part))
    qkv_spec = lambda imap: pl.BlockSpec((1, DN_CHUNK, W_BR), imap)
    n_state = 2 * N_HEADS
    return pl.pallas_call(
        _dn_kernel,
        grid=(b, n),
        in_specs=[
            qkv_spec(fwd(0)), qkv_spec(fwd(1)), qkv_spec(fwd(2)),
            qkv_spec(bwd(0)), qkv_spec(bwd(1)), qkv_spec(bwd(2)),
            pl.BlockSpec((1, DN_CHUNK, LANES), fwd(0)),
            pl.BlockSpec((1, DN_CHUNK, LANES), bwd(0)),
            pl.BlockSpec((1, 1, N_AB, DN_CHUNK), lambda bi, c: (bi, c, 0, 0)),
            pl.BlockSpec((1, 1, N_AB, DN_CHUNK), lambda bi, c: (bi, n - 1 - c, 0, 0)),
            pl.BlockSpec((2, LANES), lambda bi, c: (0, 0)),
            pl.BlockSpec((N_AB, 2), lambda bi, c: (0, 0)),
            pl.BlockSpec((1, n_state, HEAD_DIM, HEAD_DIM), lambda bi, c: (bi, 0, 0, 0)),
        ],
        out_specs=[
            pl.BlockSpec((1, DN_CHUNK, W_BR), fwd(0)),
            pl.BlockSpec((1, DN_CHUNK, W_BR), bwd(0)),
            pl.BlockSpec((1, n_state, HEAD_DIM, HEAD_DIM), lambda bi, c: (bi, 0, 0, 0)),
        ],
        out_shape=[SDS((b, t, W_BR), F32), SDS((b, t, W_BR), F32),
                   SDS((b, n_state, HEAD_DIM, HEAD_DIM), F32)],
        scratch_shapes=[pltpu.VMEM((n_state, HEAD_DIM, HEAD_DIM), F32)],
        compiler_params=_cp("parallel", "arbitrary"),
        name="deltanet",
    )(act, act, act, act, act, act, pab, pab, abt, abt, prow, pcol, s0)


def _dn_out_kernel(of_ref, ob_ref, z_ref, w_ref, y_ref):
    o = of_ref[0] + ob_ref[0]
    z = z_ref[0]
    for h in range(N_HEADS):
        sl = slice(h * HEAD_DIM, (h + 1) * HEAD_DIM)
        y_ref[0, :, sl] = (_rms(o[:, sl], w_ref[...]) * _silu(z[:, sl])).astype(y_ref.dtype)


def _dn_out(o_f, o_b, pm, onorm):
    b, t, _ = o_f.shape
    tm = _tile(t, 1024)
    return pl.pallas_call(
        _dn_out_kernel,
        grid=(b, t // tm),
        in_specs=[
            pl.BlockSpec((1, tm, W_BR), lambda bi, i: (bi, i, 0)),
            pl.BlockSpec((1, tm, W_BR), lambda bi, i: (bi, i, 0)),
            pl.BlockSpec((1, tm, W_BR), lambda bi, i: (bi, i, C_DN_Z // W_BR)),
            pl.BlockSpec((1, HEAD_DIM), lambda bi, i: (0, 0)),
        ],
        out_specs=pl.BlockSpec((1, tm, W_BR), lambda bi, i: (bi, i, 0)),
        out_shape=SDS((b, t, W_BR), BF16),
        compiler_params=_cp("parallel", "parallel"),
        name="dn_out",
    )(o_f, o_b, pm, onorm)


def _fourier_kernel(x_ref, m1_ref, m2_ref, m3_ref, twc_ref, tws_ref, o_ref, bre_ref, bim_ref, *, n1, n2, scale):
    m1, m2, m3 = m1_ref[...].astype(BF16), m2_ref[...].astype(BF16), m3_ref[...].astype(BF16)
    nu = FT_UNROLL

    def stage1(i, carry):
        t2 = [i * nu + u for u in range(nu)]
        xs = [x_ref[0, pl.ds(t, n1, stride=n2), :].astype(BF16) for t in t2]
        a = [_dot(m1, x) for x in xs]
        cw = [twc_ref[t] for t in t2]
        sw = [tws_ref[t] for t in t2]
        bre = [a[u][:n1] * cw[u] - a[u][n1:] * sw[u] for u in range(nu)]
        bim = [-(a[u][n1:] * cw[u] + a[u][:n1] * sw[u]) for u in range(nu)]
        for u in range(nu):
            bre_ref[pl.ds(t2[u], n1, stride=n2), :] = bre[u]
            bim_ref[pl.ds(t2[u], n1, stride=n2), :] = bim[u]
        return carry

    lax.fori_loop(0, n2 // nu, stage1, 0)

    def stage2(i, carry):
        f1 = [i * nu + u for u in range(nu)]
        r0 = [pl.multiple_of(f * n2, n2) for f in f1]
        b = [jnp.concatenate([bre_ref[pl.ds(r, n2), :], bim_ref[pl.ds(r, n2), :]], axis=0).astype(BF16) for r in r0]
        y = [_dot(m2, x) for x in b]
        z = [_dot(jnp.concatenate([yy[:n2], yy[n2:]], axis=1).astype(BF16), m3) for yy in y]
        for u in range(nu):
            o_ref[0, pl.ds(f1[u], n2, stride=n1), :] = z[u] * scale
        return carry

    lax.fori_loop(0, n1 // nu, stage2, 0)


def _dft_mats(n):
    ang = 2.0 * np.pi * ((np.arange(n)[:, None] * np.arange(n)[None, :]) % n) / n
    return np.cos(ang), np.sin(ang)


def _fourier(pm):
    b, t, _ = pm.shape
    n1 = 1 << ((t.bit_length() - 1 + 1) // 2)
    n2 = t // n1
    assert n1 * n2 == t and n1 % FT_UNROLL == 0 and n2 % FT_UNROLL == 0
    c1, s1 = _dft_mats(n1)
    c2, s2 = _dft_mats(n2)
    cc, sc = _dft_mats(HEAD_DIM)
    m1 = jnp.asarray(np.concatenate([c1, s1], axis=0), F32)
    m2 = jnp.asarray(np.block([[c2, s2], [-s2, c2]]), F32)
    m3 = jnp.asarray(np.concatenate([cc, sc], axis=0), F32)
    ang = 2.0 * np.pi * ((np.arange(n2)[:, None] * np.arange(n1)[None, :]) % t) / t
    twc = jnp.broadcast_to(jnp.asarray(np.cos(ang), F32)[:, :, None], (n2, n1, HEAD_DIM))
    tws = jnp.broadcast_to(jnp.asarray(np.sin(ang), F32)[:, :, None], (n2, n1, HEAD_DIM))
    const = lambda shape: pl.BlockSpec(shape, lambda bi, g: (0,) * len(shape))
    return pl.pallas_call(
        functools.partial(_fourier_kernel, n1=n1, n2=n2, scale=float((t * HEAD_DIM) ** -0.5)),
        grid=(b, N_HEADS),
        in_specs=[
            pl.BlockSpec((1, t, HEAD_DIM), lambda bi, g: (bi, 0, C_FT // HEAD_DIM + g)),
            const(m1.shape), const(m2.shape), const(m3.shape),
            const((n2, n1, HEAD_DIM)), const((n2, n1, HEAD_DIM)),
        ],
        out_specs=pl.BlockSpec((1, t, HEAD_DIM), lambda bi, g: (bi, 0, g)),
        out_shape=SDS((b, t, W_BR), F32),
        scratch_shapes=[pltpu.VMEM((t, HEAD_DIM), F32)] * 2,
        compiler_params=_cp("parallel", "parallel"),
        name="fourier",
    )(pm, m1, m2, m3, twc, tws)


def _sgu_kernel(u_ref, v_ref, vn_ref, ws_ref, bs_ref, o_ref):
    u = jax.nn.gelu(u_ref[0])
    v = jax.nn.gelu(v_ref[0])
    mu = jnp.mean(v, axis=-1, keepdims=True)
    var = jnp.mean(jnp.square(v - mu), axis=-1, keepdims=True)
    vb = ((v - mu) * lax.rsqrt(var + EPS) * vn_ref[...]).astype(BF16)
    out = []
    for ch in range(u.shape[0] // SG_CHUNK):
        rs = slice(ch * SG_CHUNK, (ch + 1) * SG_CHUNK)
        mix = [_dot(ws_ref[g], vb[rs, g * HEAD_DIM:(g + 1) * HEAD_DIM]) + bs_ref[:, g:g + 1] for g in range(N_HEADS)]
        out.append((u[rs, :] * jnp.concatenate(mix, axis=1)).astype(o_ref.dtype))
    for ch, y in enumerate(out):
        o_ref[0, ch * SG_CHUNK:(ch + 1) * SG_CHUNK, :] = y


def _spatial_gating(pm, ws, bs_t, vnorm):
    b, t, _ = pm.shape
    tm = _tile(t, 4 * SG_CHUNK, SG_CHUNK)
    assert tm % SG_CHUNK == 0
    return pl.pallas_call(
        _sgu_kernel,
        grid=(b, t // tm),
        in_specs=[
            pl.BlockSpec((1, tm, W_BR), lambda bi, i: (bi, i, C_SG_U // W_BR)),
            pl.BlockSpec((1, tm, W_BR), lambda bi, i: (bi, i, C_SG_V // W_BR)),
            pl.BlockSpec((1, W_BR), lambda bi, i: (0, 0)),
            pl.BlockSpec((N_HEADS, SG_CHUNK, SG_CHUNK), lambda bi, i: (0, 0, 0)),
            pl.BlockSpec((SG_CHUNK, N_HEADS), lambda bi, i: (0, 0)),
        ],
        out_specs=pl.BlockSpec((1, tm, W_BR), lambda bi, i: (bi, i, 0)),
        out_shape=SDS((b, t, W_BR), BF16),
        compiler_params=_cp("parallel", "parallel"),
        name="sgu",
    )(pm, pm, vnorm, ws, bs_t)


def _merge_kernel(h_ref, y0_ref, y1_ref, y2_ref, y3_ref, wg_ref, bg_ref, wb_ref, o_ref):
    h = h_ref[0]
    acc = None
    for i, y_ref in enumerate((y0_ref, y1_ref, y2_ref, y3_ref)):
        gate = jax.nn.sigmoid(_dot(h, wg_ref[i]) + bg_ref[i])
        term = gate * _dot(y_ref[0].astype(BF16), wb_ref[i])
        acc = term if acc is None else acc + term
    o_ref[0] = acc.astype(o_ref.dtype)


def _merge(h, ys, wg, bg, wb):
    bx, t, d = h.shape
    tm, tn = _tile(t, 1024), _tile(d, 512, LANES)
    y_spec = pl.BlockSpec((1, tm, W_BR), lambda b, i, j: (b, i, 0))
    return pl.pallas_call(
        _merge_kernel,
        grid=(bx, t // tm, d // tn),
        in_specs=[
            pl.BlockSpec((1, tm, d), lambda b, i, j: (b, i, 0)),
            y_spec, y_spec, y_spec, y_spec,
            pl.BlockSpec((N_BRANCH, d, tn), lambda b, i, j: (0, 0, j)),
            pl.BlockSpec((N_BRANCH, 1, tn), lambda b, i, j: (0, 0, j)),
            pl.BlockSpec((N_BRANCH, W_BR, tn), lambda b, i, j: (0, 0, j)),
        ],
        out_specs=pl.BlockSpec((1, tm, tn), lambda b, i, j: (b, i, j)),
        out_shape=SDS((bx, t, d), BF16),
        compiler_params=_cp("parallel", "parallel", "arbitrary"),
        name="merge",
    )(h, *ys, wg, bg.reshape(N_BRANCH, 1, d), wb)


def _resid_kernel(a_ref, w_ref, x_ref, g_ref, o_ref, *, row):
    o_ref[0] = x_ref[0] + _mod_row(g_ref, row) * _dot(a_ref[0], w_ref[...])


def _resid_proj(a, w, x, mod, gate_blk, row):
    bx, t, k = a.shape
    d = w.shape[1]
    tm, tn = _tile(t, 1024 if k <= 2048 else 512), _tile(d, 512, LANES)
    nj = d // tn
    return pl.pallas_call(
        functools.partial(_resid_kernel, row=row),
        grid=(bx, t // tm, nj),
        in_specs=[
            pl.BlockSpec((1, tm, k), lambda b, i, j: (b, i, 0)),
            pl.BlockSpec((k, tn), lambda b, i, j: (0, j)),
            pl.BlockSpec((1, tm, tn), lambda b, i, j: (b, i, j)),
            pl.BlockSpec((MOD_ROWS, tn), lambda b, i, j: (0, gate_blk * nj + j)),
        ],
        out_specs=pl.BlockSpec((1, tm, tn), lambda b, i, j: (b, i, j)),
        out_shape=SDS((bx, t, d), F32),
        compiler_params=_cp("parallel", "parallel", "arbitrary"),
        name="resid_proj",
    )(a, w, x, mod)


def _ffn_up_kernel(x_ref, nw_ref, sh_ref, sc_ref, w1_ref, w3_ref, o_ref, hs_ref, *, row):
    @pl.when(pl.program_id(2) == 0)
    def _():
        hs_ref[...] = _norm_mod(x_ref[0], nw_ref, sh_ref, sc_ref, row)

    h = hs_ref[...]
    o_ref[0] = (_silu(_dot(h, w1_ref[...])) * _dot(h, w3_ref[...])).astype(o_ref.dtype)


def _ffn_up(x, nw, mod, w1, w3, row):
    bx, t, d = x.shape
    f = w1.shape[1]
    tm, tn = _tile(t, 1024), _tile(f, 512, LANES)
    return pl.pallas_call(
        functools.partial(_ffn_up_kernel, row=row),
        grid=(bx, t // tm, f // tn),
        in_specs=[
            pl.BlockSpec((1, tm, d), lambda b, i, j: (b, i, 0)),
            pl.BlockSpec((1, d), lambda b, i, j: (0, 0)),
            pl.BlockSpec((MOD_ROWS, d), lambda b, i, j: (0, 3)),
            pl.BlockSpec((MOD_ROWS, d), lambda b, i, j: (0, 4)),
            pl.BlockSpec((d, tn), lambda b, i, j: (0, j)),
            pl.BlockSpec((d, tn), lambda b, i, j: (0, j)),
        ],
        out_specs=pl.BlockSpec((1, tm, tn), lambda b, i, j: (b, i, j)),
        out_shape=SDS((bx, t, f), BF16),
        scratch_shapes=[pltpu.VMEM((tm, d), BF16)],
        compiler_params=_cp("parallel", "parallel", "arbitrary"),
        name="ffn_up",
    )(x, nw, mod, mod, w1, w3)


def _mixers(pm, pab, pm_c, pab_c, p, need_ctx):
    b = pm.shape[0]
    y_na = _neighbourhood_attention(pm, pm_c, p["qn"], p["kn"], p["rpb"])
    zero = jnp.zeros((b, 2 * N_HEADS, HEAD_DIM, HEAD_DIM), F32)
    of_c, ob_c, s_ctx = _deltanet(_dn_conv(pm_c, p["conv"]), pab_c, p["a_log"], p["dt_bias"], zero)
    of, ob, _ = _deltanet(_dn_conv(pm, p["conv"]), pab, p["a_log"], p["dt_bias"], s_ctx)
    ys = (y_na, _dn_out(of, ob, pm, p["onorm"]), _fourier(pm), _spatial_gating(pm, p["sg_w"], p["sg_bt"], p["sg_vn"]))
    if not need_ctx:
        return ys, None
    ys_c = (_dense_attention(pm_c, p["qn"], p["kn"]), _dn_out(of_c, ob_c, pm_c, p["onorm"]), _fourier(pm_c),
            _spatial_gating(pm_c, p["sg_w"], p["sg_bt"], p["sg_vn"]))
    return ys, ys_c


def kernel(x, c, ctx, c_ctx, w_ada, b_ada, norm1_w, norm2_w, w_in, na_qnorm, na_knorm, na_rpb, dn_conv, dn_a_log, dn_dt_bias, dn_onorm, sg_w, sg_b, sg_vnorm, w_gate, b_gate, w_branch, w_out, ffn_w1, ffn_w3, ffn_w2):
    bsz, seq, d = x.shape
    ctx_len = ctx.shape[1]
    depth = w_ada.shape[0]
    assert bsz + 1 <= MOD_ROWS and d % LANES == 0
    ctx_row = bsz

    cc = jnp.zeros((MOD_ROWS, d), F32).at[:bsz].set(c).at[ctx_row].set(c_ctx)
    mod_all = _ada_mod(cc, w_ada, b_ada)
    xc = ctx.reshape(1, bsz * ctx_len, d)

    for l in range(depth):
        need_ctx = l < depth - 1
        mod = mod_all[l]
        w_main = jnp.concatenate([w_in[l][:, :C_AB_SRC], w_in[l][:, C_AB_SRC + N_AB:]], axis=1).astype(BF16)
        w_ab = jnp.pad(w_in[l][:, C_AB_SRC:C_AB_SRC + N_AB], ((0, 0), (0, LANES - N_AB))).astype(BF16)
        p = dict(qn=na_qnorm[l][None], kn=na_knorm[l][None], rpb=na_rpb[l], conv=dn_conv[l], a_log=dn_a_log[l],
                 dt_bias=dn_dt_bias[l], onorm=dn_onorm[l][None], sg_w=sg_w[l].astype(BF16), sg_bt=sg_b[l].T,
                 sg_vn=sg_vnorm[l][None])
        wg, wb, wo = w_gate[l].astype(BF16), w_branch[l].astype(BF16), w_out[l].astype(BF16)
        w1, w3, w2 = ffn_w1[l].astype(BF16), ffn_w3[l].astype(BF16), ffn_w2[l].astype(BF16)
        n1w, n2w = norm1_w[l][None], norm2_w[l][None]

        pm, pab, h = _in_proj(x, n1w, mod, w_main, w_ab, None)
        pm_c, pab_c, h_c = _in_proj(xc, n1w, mod, w_main, w_ab, ctx_row)
        pm_c = pm_c.reshape(bsz, ctx_len, N_MAIN)
        pab_c = pab_c.reshape(bsz, ctx_len, LANES)
        ys, ys_c = _mixers(pm, pab, pm_c, pab_c, p, need_ctx)

        merged = _merge(h, ys, wg, b_gate[l], wb)
        x = _resid_proj(merged, wo, x, mod, 2, None)
        x = _resid_proj(_ffn_up(x, n2w, mod, w1, w3, None), w2, x, mod, 5, None)
        if need_ctx:
            ys_c = tuple(y.reshape(1, bsz * ctx_len, W_BR) for y in ys_c)
            merged_c = _merge(h_c, ys_c, wg, b_gate[l], wb)
            xc = _resid_proj(merged_c, wo, xc, mod, 2, ctx_row)
            xc = _resid_proj(_ffn_up(xc, n2w, mod, w1, w3, ctx_row), w2, xc, mod, 5, ctx_row)
    return x
```

```python
import functools

import numpy as np
import jax
import jax.numpy as jnp
from jax import lax
from jax.experimental import pallas as pl
from jax.experimental.pallas import tpu as pltpu

F32 = jnp.float32
BF16 = jnp.bfloat16
SDS = jax.ShapeDtypeStruct

LANES = 128
HEAD_DIM = 128
N_HEADS = 4
W_BR = N_HEADS * HEAD_DIM
N_BRANCH = 4
GRID_W = 64
NA_WIN_R = 8
NA_WIN_C = 16
NA_ROWS_PER_STEP = 8
DN_CHUNK = 64
DN_BATCH_PER_STEP = 2
DN_INV_BASE = 8
DN_CONV = 5
SG_CHUNK = 128
FT_UNROLL = 8
N_MOD = 6
EPS = 1e-6
NEG_INF = -1e30
MOD_ROWS = 8
N_AB = 4 * N_HEADS
VMEM_LIMIT_BYTES = 56 * 1024 * 1024
RESIDENT_WEIGHT_BYTES = 8 * 1024 * 1024

C_NA_Q, C_NA_K, C_NA_V = 0, W_BR, 2 * W_BR
C_DN_QKV = 3 * W_BR
C_DN_Z = 6 * W_BR
C_FT = 7 * W_BR
C_SG_U = 8 * W_BR
C_SG_V = 9 * W_BR
N_MAIN = 10 * W_BR
C_AB_SRC = 7 * W_BR


def _cp(*sem):
    return pltpu.CompilerParams(dimension_semantics=sem, vmem_limit_bytes=VMEM_LIMIT_BYTES)


def _tile(n, pref, mult=8):
    if n <= pref:
        return n
    for t in range(pref - pref % mult, 0, -mult):
        if n % t == 0:
            return t
    return n


def _dot(a, b):
    return jnp.dot(a, b, preferred_element_type=F32)


def _dot_nt(a, b):
    return lax.dot_general(a, b, (((1,), (1,)), ((), ())), preferred_element_type=F32)


def _dot_tn(a, b):
    return lax.dot_general(a, b, (((0,), (0,)), ((), ())), preferred_element_type=F32)


def _silu(x):
    return x * jax.nn.sigmoid(x)


def _rms(x, w):
    return x * lax.rsqrt(jnp.mean(x * x, axis=-1, keepdims=True) + EPS) * w


def _mod_row(ref, row):
    b = pl.program_id(0) if row is None else row
    return ref[pl.ds(b, 1), :]


def _norm_mod(x, nw_ref, sh_ref, sc_ref, row):
    return (_rms(x, nw_ref[...]) * (1.0 + _mod_row(sc_ref, row)) + _mod_row(sh_ref, row)).astype(BF16)


def _ada_kernel(c_ref, w_ref, b_ref, o_ref):
    s = _silu(c_ref[...]).astype(BF16)
    o_ref[0] = _dot(s, w_ref[0].astype(BF16)) + b_ref[0]


def _ada_mod(cc, w_ada, b_ada):
    depth, d, n = w_ada.shape
    tn = _tile(n, 1024, LANES)
    return pl.pallas_call(
        _ada_kernel,
        grid=(depth, n // tn),
        in_specs=[
            pl.BlockSpec((MOD_ROWS, d), lambda l, j: (0, 0)),
            pl.BlockSpec((1, d, tn), lambda l, j: (l, 0, j)),
            pl.BlockSpec((1, 1, tn), lambda l, j: (l, 0, j)),
        ],
        out_specs=pl.BlockSpec((1, MOD_ROWS, tn), lambda l, j: (l, 0, j)),
        out_shape=SDS((depth, MOD_ROWS, n), F32),
        compiler_params=_cp("parallel", "parallel"),
        name="ada_mod",
    )(cc, w_ada, b_ada.reshape(depth, 1, n))


def _in_proj_kernel(x_ref, nw_ref, sh_ref, sc_ref, w_ref, wab_ref, o_ref, oab_ref, h_ref, hs_ref, *, row):
    @pl.when(pl.program_id(2) == 0)
    def _():
        h = _norm_mod(x_ref[0], nw_ref, sh_ref, sc_ref, row)
        hs_ref[...] = h
        h_ref[0] = h
        oab_ref[0] = _dot(h, wab_ref[...])

    o_ref[0] = _dot(hs_ref[...], w_ref[...]).astype(o_ref.dtype)


def _in_proj(x, nw, mod, w_main, w_ab, row):
    bx, t, d = x.shape
    n = w_main.shape[1]
    tm, tn = _tile(t, 1024), _tile(n, 1024, LANES)
    return pl.pallas_call(
        functools.partial(_in_proj_kernel, row=row),
        grid=(bx, t // tm, n // tn),
        in_specs=[
            pl.BlockSpec((1, tm, d), lambda b, i, j: (b, i, 0)),
            pl.BlockSpec((1, d), lambda b, i, j: (0, 0)),
            pl.BlockSpec((MOD_ROWS, d), lambda b, i, j: (0, 0)),
            pl.BlockSpec((MOD_ROWS, d), lambda b, i, j: (0, 1)),
            pl.BlockSpec((d, tn), lambda b, i, j: (0, j)),
            pl.BlockSpec((d, LANES), lambda b, i, j: (0, 0)),
        ],
        out_specs=[
            pl.BlockSpec((1, tm, tn), lambda b, i, j: (b, i, j)),
            pl.BlockSpec((1, tm, LANES), lambda b, i, j: (b, i, 0)),
            pl.BlockSpec((1, tm, d), lambda b, i, j: (b, i, 0)),
        ],
        out_shape=[SDS((bx, t, n), BF16), SDS((bx, t, LANES), F32), SDS((bx, t, d), BF16)],
        scratch_shapes=[pltpu.VMEM((tm, d), BF16)],
        compiler_params=_cp("parallel", "parallel", "arbitrary"),
        name="in_proj",
    )(x, nw, mod, mod, w_main, w_ab)


def _na_kernel(q_ref, k_ref, v_ref, kc_ref, vc_ref, qn_ref, kn_ref, bias_ref, o_ref,
               qs, ks, vs, kcs, vcs, *, rows):
    qs[...] = (_rms(q_ref[0].astype(F32), qn_ref[...]) * HEAD_DIM ** -0.5).astype(BF16)
    ks[...] = _rms(k_ref[0].astype(F32), kn_ref[...]).astype(BF16)
    vs[...] = v_ref[0].astype(BF16)
    kcs[...] = _rms(kc_ref[0].astype(F32), kn_ref[...]).astype(BF16)
    vcs[...] = vc_ref[0].astype(BF16)
    band = NA_WIN_R * GRID_W
    nr = NA_ROWS_PER_STEP

    def body(g, carry):
        rr = [g * nr + i for i in range(nr)]
        start = [jnp.clip(r - NA_WIN_R // 2, 0, rows - NA_WIN_R) for r in rr]
        r0 = [pl.multiple_of(r * GRID_W, GRID_W) for r in rr]
        k0 = [pl.multiple_of(s * GRID_W, GRID_W) for s in start]
        q = [qs[pl.ds(x, GRID_W), :] for x in r0]
        kb = [ks[pl.ds(x, band), :] for x in k0]
        vb = [vs[pl.ds(x, band), :] for x in k0]
        bias = [bias_ref[0, r - s] for r, s in zip(rr, start)]
        kc, vc = kcs[...], vcs[...]
        s_loc = [_dot_nt(q[i], kb[i]) + bias[i] for i in range(nr)]
        s_ctx = [_dot_nt(q[i], kc) for i in range(nr)]
        m = [jnp.maximum(jnp.max(s_loc[i], axis=-1, keepdims=True), jnp.max(s_ctx[i], axis=-1, keepdims=True))
             for i in range(nr)]
        p = [jnp.exp(s_loc[i] - m[i]) for i in range(nr)]
        pc = [jnp.exp(s_ctx[i] - m[i]) for i in range(nr)]
        denom = [jnp.sum(p[i], axis=-1, keepdims=True) + jnp.sum(pc[i], axis=-1, keepdims=True) for i in range(nr)]
        o = [_dot(p[i].astype(BF16), vb[i]) + _dot(pc[i].astype(BF16), vc) for i in range(nr)]
        for i in range(nr):
            o_ref[0, pl.ds(r0[i], GRID_W), :] = (o[i] / denom[i]).astype(o_ref.dtype)
        return carry

    lax.fori_loop(0, rows // nr, body, 0)


def _na_bias_table(rpb):
    col = np.arange(GRID_W)
    dc = np.clip(col[None, :] - col[:, None], 1 - NA_WIN_C, NA_WIN_C - 1) + NA_WIN_C - 1
    col_start = np.clip(col - NA_WIN_C // 2, 0, GRID_W - NA_WIN_C)
    in_win = (col[None, :] >= col_start[:, None]) & (col[None, :] < col_start[:, None] + NA_WIN_C)
    rows = jnp.stack([rpb[:, NA_WIN_R - 1 - typ:2 * NA_WIN_R - 1 - typ, :] for typ in range(NA_WIN_R)], axis=1)
    onehot = (dc[None] == np.arange(2 * NA_WIN_C - 1)[:, None, None]).astype(np.float32)
    b = jnp.einsum('htjc,cqk->htqjk', rows.astype(F32), onehot, precision=lax.Precision.HIGHEST)
    b = jnp.where(in_win[None, None, :, None, :], b, NEG_INF)
    return b.reshape(rpb.shape[0], NA_WIN_R, GRID_W, NA_WIN_R * GRID_W)


def _neighbourhood_attention(pm, pm_c, qn, kn, rpb):
    b, t, _ = pm.shape
    ctx_len = pm_c.shape[1]
    rows = t // GRID_W
    assert rows >= NA_WIN_R and rows * GRID_W == t and rows % NA_ROWS_PER_STEP == 0
    band = NA_WIN_R * GRID_W
    hb = lambda off: (lambda bi, h: (bi, 0, off // HEAD_DIM + h))
    return pl.pallas_call(
        functools.partial(_na_kernel, rows=rows),
        grid=(b, N_HEADS),
        in_specs=[
            pl.BlockSpec((1, t, HEAD_DIM), hb(C_NA_Q)),
            pl.BlockSpec((1, t, HEAD_DIM), hb(C_NA_K)),
            pl.BlockSpec((1, t, HEAD_DIM), hb(C_NA_V)),
            pl.BlockSpec((1, ctx_len, HEAD_DIM), hb(C_NA_K)),
            pl.BlockSpec((1, ctx_len, HEAD_DIM), hb(C_NA_V)),
            pl.BlockSpec((1, HEAD_DIM), lambda bi, h: (0, 0)),
            pl.BlockSpec((1, HEAD_DIM), lambda bi, h: (0, 0)),
            pl.BlockSpec((1, NA_WIN_R, GRID_W, band), lambda bi, h: (h, 0, 0, 0)),
        ],
        out_specs=pl.BlockSpec((1, t, HEAD_DIM), lambda bi, h: (bi, 0, h)),
        out_shape=SDS((b, t, W_BR), BF16),
        scratch_shapes=[pltpu.VMEM((t, HEAD_DIM), BF16)] * 3 + [pltpu.VMEM((ctx_len, HEAD_DIM), BF16)] * 2,
        compiler_params=_cp("parallel", "parallel"),
        name="natten",
    )(pm, pm, pm, pm_c, pm_c, qn, kn, _na_bias_table(rpb))


def _ctx_attn_kernel(q_ref, k_ref, v_ref, qn_ref, kn_ref, o_ref):
    q = _rms(q_ref[0].astype(F32), qn_ref[...]).astype(BF16)
    k = _rms(k_ref[0].astype(F32), kn_ref[...]).astype(BF16)
    s = _dot_nt(q, k) * HEAD_DIM ** -0.5
    p = jnp.exp(s - jnp.max(s, axis=-1, keepdims=True))
    o = _dot(p.astype(BF16), v_ref[0].astype(BF16)) / jnp.sum(p, axis=-1, keepdims=True)
    o_ref[0] = o.astype(o_ref.dtype)


def _dense_attention(pm_c, qn, kn):
    b, ctx_len, _ = pm_c.shape
    hb = lambda off: (lambda bi, h: (bi, 0, off // HEAD_DIM + h))
    return pl.pallas_call(
        _ctx_attn_kernel,
        grid=(b, N_HEADS),
        in_specs=[
            pl.BlockSpec((1, ctx_len, HEAD_DIM), hb(C_NA_Q)),
            pl.BlockSpec((1, ctx_len, HEAD_DIM), hb(C_NA_K)),
            pl.BlockSpec((1, ctx_len, HEAD_DIM), hb(C_NA_V)),
            pl.BlockSpec((1, HEAD_DIM), lambda bi, h: (0, 0)),
            pl.BlockSpec((1, HEAD_DIM), lambda bi, h: (0, 0)),
        ],
        out_specs=pl.BlockSpec((1, ctx_len, HEAD_DIM), lambda bi, h: (bi, 0, h)),
        out_shape=SDS((b, ctx_len, W_BR), BF16),
        compiler_params=_cp("parallel", "parallel"),
        name="ctx_attn",
    )(pm_c, pm_c, pm_c, qn, kn)


def _dn_conv_kernel(x_ref, w_ref, o_ref, *, n_norm_tiles):
    x = x_ref[0].astype(F32)
    t = x.shape[0]
    ridx = lax.broadcasted_iota(jnp.int32, x.shape, 0)
    half = DN_CONV // 2
    acc = x * w_ref[half:half + 1, :]
    for kk in range(DN_CONV):
        d = kk - half
        if d == 0:
            continue
        xs = pltpu.roll(x, (-d) % t, axis=0)
        valid = (ridx < t - d) if d > 0 else (ridx >= -d)
        acc = acc + jnp.where(valid, xs, 0.0) * w_ref[kk:kk + 1, :]
    y = _silu(acc)
    unit = y * lax.rsqrt(jnp.sum(y * y, axis=-1, keepdims=True) + EPS)
    o_ref[0] = jnp.where(pl.program_id(1) < n_norm_tiles, unit, y)


def _dn_conv(pm, conv_w):
    b, t, _ = pm.shape
    n_tiles = 3 * N_HEADS
    return pl.pallas_call(
        functools.partial(_dn_conv_kernel, n_norm_tiles=2 * N_HEADS),
        grid=(b, n_tiles),
        in_specs=[
            pl.BlockSpec((1, t, HEAD_DIM), lambda bi, j: (bi, 0, C_DN_QKV // HEAD_DIM + j)),
            pl.BlockSpec((DN_CONV, HEAD_DIM), lambda bi, j: (0, j)),
        ],
        out_specs=pl.BlockSpec((1, t, HEAD_DIM), lambda bi, j: (bi, 0, j)),
        out_shape=SDS((b, t, 3 * W_BR), F32),
        compiler_params=_cp("parallel", "parallel"),
        name="dn_conv",
    )(pm, conv_w)


def _softplus(x):
    return jnp.maximum(x, 0.0) + jnp.log1p(jnp.exp(-jnp.abs(x)))


def _block_masks(n):
    ii = lax.broadcasted_iota(jnp.int32, (n, n), 0)
    jj = lax.broadcasted_iota(jnp.int32, (n, n), 1)
    same = lambda s: (ii // s) == (jj // s)
    masks, s = [same(DN_INV_BASE)], DN_INV_BASE
    while s < n:
        masks.append(same(2 * s) & jnp.logical_not(same(s)))
        s *= 2
    return masks


def _unit_triangular_inverses(a_list, eye, masks):
    mm = lambda x, y: _dot(x.astype(BF16), y.astype(BF16))
    n = len(a_list)
    pw = [jnp.where(masks[0], a, 0.0) for a in a_list]
    inv = [eye - p for p in pw]
    s = 2
    while s < DN_INV_BASE:
        pw = [mm(p, p) for p in pw]
        inv = [inv[i] + mm(inv[i], pw[i]) for i in range(n)]
        s *= 2
    for m in masks[1:]:
        t = [mm(jnp.where(m, a_list[i], 0.0), inv[i]) for i in range(n)]
        inv = [inv[i] - mm(inv[i], t[i]) for i in range(n)]
    return inv


def _dn_kernel(qf_ref, kf_ref, vf_ref, qb_ref, kb_ref, vb_ref, abf_ref, abb_ref, abtf_ref, abtb_ref,
               prow_ref, pcol_ref, s0_ref, of_ref, ob_ref, sfin_ref, s_ref):
    c = pl.program_id(1)

    @pl.when(c == 0)
    def _():
        s_ref[...] = s0_ref[...].reshape(s_ref.shape)

    cs = DN_CHUNK
    ii = lax.broadcasted_iota(jnp.int32, (cs, cs), 0)
    jj = lax.broadcasted_iota(jnp.int32, (cs, cs), 1)
    eye = (ii == jj).astype(F32)
    blk_masks = _block_masks(cs)
    scale = HEAD_DIM ** -0.5
    hi = lax.Precision.HIGHEST

    nb = qf_ref.shape[0]
    nst = 2 * N_HEADS
    chains = [(bb, d, h) for bb in range(nb) for d in range(2) for h in range(N_HEADS)]
    nch = len(chains)
    incl = [(jj <= ii), (jj >= ii)]
    strict = [(jj < ii), (jj > ii)]
    gc_col, gc_row, beta_all = {}, {}, {}
    for bb in range(nb):
        for d in range(2):
            tri = incl[d].astype(F32)
            ab = (abf_ref if d == 0 else abb_ref)[bb]
            abt = (abtf_ref if d == 0 else abtb_ref)[bb, 0]
            g_col = -jnp.exp(prow_ref[0:1, :]) * _softplus(ab + prow_ref[1:2, :])
            g_row = -jnp.exp(pcol_ref[:, 0:1]) * _softplus(abt + pcol_ref[:, 1:2])
            gc_col[bb, d] = jnp.dot(tri, g_col, preferred_element_type=F32, precision=hi)
            gc_row[bb, d] = lax.dot_general(g_row, tri, (((1,), (1,)), ((), ())),
                                            preferred_element_type=F32, precision=hi)
            beta_all[bb, d] = jax.nn.sigmoid(ab)
    qkv_refs = ((qf_ref, kf_ref, vf_ref), (qb_ref, kb_ref, vb_ref))
    sl = [slice(h * HEAD_DIM, (h + 1) * HEAD_DIM) for h in range(N_HEADS)]
    col = [d * N_HEADS + h for bb, d, h in chains]
    q = [qkv_refs[d][0][bb, :, sl[h]] * scale for bb, d, h in chains]
    k = [qkv_refs[d][1][bb, :, sl[h]] for bb, d, h in chains]
    v = [qkv_refs[d][2][bb, :, sl[h]] for bb, d, h in chains]
    s_old = [s_ref[bb * nst + col[j]] for j, (bb, d, h) in enumerate(chains)]
    gcc = [gc_col[bb, d][:, col[j]:col[j] + 1] for j, (bb, d, h) in enumerate(chains)]
    gcr = [gc_row[bb, d][col[j]:col[j] + 1, :] for j, (bb, d, h) in enumerate(chains)]
    beta = [beta_all[bb, d][:, nst + col[j]:nst + col[j] + 1] for j, (bb, d, h) in enumerate(chains)]
    decay = [jnp.exp(jnp.where(incl[d], gcc[j] - gcr[j], -jnp.inf)) for j, (bb, d, h) in enumerate(chains)]
    eg = [jnp.exp(x) for x in gcc]
    g_last = [gcc[j][cs - 1:cs, :] if d == 0 else gcc[j][0:1, :] for j, (bb, d, h) in enumerate(chains)]
    kbf = [x.astype(BF16) for x in k]
    kbeta = [k[j] * beta[j] for j in range(nch)]
    kq = [_dot_nt(jnp.concatenate([kbeta[j], q[j]], axis=0).astype(BF16), kbf[j]) for j in range(nch)]
    a_mat = [jnp.where(strict[d], kq[j][:cs] * decay[j], 0.0) for j, (bb, d, h) in enumerate(chains)]
    qk = [jnp.where(incl[d], kq[j][cs:] * decay[j], 0.0) for j, (bb, d, h) in enumerate(chains)]
    inv = _unit_triangular_inverses(a_mat, eye, blk_masks)
    rhs = [jnp.concatenate([v[j] * beta[j], kbeta[j] * eg[j]], axis=1).astype(BF16) for j in range(nch)]
    uw = [_dot(inv[j].astype(BF16), rhs[j]) for j in range(nch)]
    ws_qs = [_dot(jnp.concatenate([uw[j][:, HEAD_DIM:], q[j] * eg[j]], axis=0).astype(BF16), s_old[j].astype(BF16))
             for j in range(nch)]
    v_new = [uw[j][:, :HEAD_DIM] - ws_qs[j][:cs] for j in range(nch)]
    vnb = [x.astype(BF16) for x in v_new]
    o = [ws_qs[j][cs:] + _dot(qk[j].astype(BF16), vnb[j]) for j in range(nch)]
    kd = [(k[j] * jnp.exp(g_last[j] - gcc[j])).astype(BF16) for j in range(nch)]
    s_new = [s_old[j] * jnp.exp(g_last[j]) + _dot_tn(kd[j], vnb[j]) for j in range(nch)]
    for j, (bb, d, h) in enumerate(chains):
        s_ref[bb * nst + col[j]] = s_new[j]
        (of_ref if d == 0 else ob_ref)[bb, :, sl[h]] = o[j]

    @pl.when(c == pl.num_programs(1) - 1)
    def _():
        sfin_ref[...] = s_ref[...].reshape(sfin_ref.shape)


def _deltanet(act, pab, a_log, dt_bias, s0):
    b, t, _ = act.shape
    n = t // DN_CHUNK
    assert n * DN_CHUNK == t
    abt = jnp.swapaxes(pab[..., :N_AB].reshape(b, n, DN_CHUNK, N_AB), 2, 3)
    al, dt = a_log.reshape(-1).astype(F32), dt_bias.reshape(-1).astype(F32)
    prow = jnp.zeros((2, LANES), F32).at[0, :al.size].set(al).at[1, :dt.size].set(dt)
    pcol = jnp.zeros((N_AB, 2), F32).at[:al.size, 0].set(al).at[:dt.size, 1].set(dt)
    nb = DN_BATCH_PER_STEP if b % DN_BATCH_PER_STEP == 0 else 1
    fwd = lambda part: (lambda bi, c: (bi, c, part))
    bwd = lambda part: (lambda bi, c: (bi, n - 1 - c, part))
    qkv_spec = lambda imap: pl.BlockSpec((nb, DN_CHUNK, W_BR), imap)
    n_state = 2 * N_HEADS
    state_spec = pl.BlockSpec((nb, n_state, HEAD_DIM, HEAD_DIM), lambda bi, c: (bi, 0, 0, 0))
    return pl.pallas_call(
        _dn_kernel,
        grid=(b // nb, n),
        in_specs=[
            qkv_spec(fwd(0)), qkv_spec(fwd(1)), qkv_spec(fwd(2)),
            qkv_spec(bwd(0)), qkv_spec(bwd(1)), qkv_spec(bwd(2)),
            pl.BlockSpec((nb, DN_CHUNK, LANES), fwd(0)),
            pl.BlockSpec((nb, DN_CHUNK, LANES), bwd(0)),
            pl.BlockSpec((nb, 1, N_AB, DN_CHUNK), lambda bi, c: (bi, c, 0, 0)),
            pl.BlockSpec((nb, 1, N_AB, DN_CHUNK), lambda bi, c: (bi, n - 1 - c, 0, 0)),
            pl.BlockSpec((2, LANES), lambda bi, c: (0, 0)),
            pl.BlockSpec((N_AB, 2), lambda bi, c: (0, 0)),
            state_spec,
        ],
        out_specs=[
            pl.BlockSpec((nb, DN_CHUNK, W_BR), fwd(0)),
            pl.BlockSpec((nb, DN_CHUNK, W_BR), bwd(0)),
            state_spec,
        ],
        out_shape=[SDS((b, t, W_BR), F32), SDS((b, t, W_BR), F32),
                   SDS((b, n_state, HEAD_DIM, HEAD_DIM), F32)],
        scratch_shapes=[pltpu.VMEM((nb * n_state, HEAD_DIM, HEAD_DIM), F32)],
        compiler_params=_cp("parallel", "arbitrary"),
        name="deltanet",
    )(act, act, act, act, act, act, pab, pab, abt, abt, prow, pcol, s0)


def _dn_out_kernel(of_ref, ob_ref, z_ref, w_ref, y_ref):
    o = of_ref[0] + ob_ref[0]
    z = z_ref[0].astype(F32)
    for h in range(N_HEADS):
        sl = slice(h * HEAD_DIM, (h + 1) * HEAD_DIM)
        y_ref[0, :, sl] = (_rms(o[:, sl], w_ref[...]) * _silu(z[:, sl])).astype(y_ref.dtype)


def _dn_out(o_f, o_b, pm, onorm):
    b, t, _ = o_f.shape
    tm = _tile(t, 1024)
    return pl.pallas_call(
        _dn_out_kernel,
        grid=(b, t // tm),
        in_specs=[
            pl.BlockSpec((1, tm, W_BR), lambda bi, i: (bi, i, 0)),
            pl.BlockSpec((1, tm, W_BR), lambda bi, i: (bi, i, 0)),
            pl.BlockSpec((1, tm, W_BR), lambda bi, i: (bi, i, C_DN_Z // W_BR)),
            pl.BlockSpec((1, HEAD_DIM), lambda bi, i: (0, 0)),
        ],
        out_specs=pl.BlockSpec((1, tm, W_BR), lambda bi, i: (bi, i, 0)),
        out_shape=SDS((b, t, W_BR), BF16),
        compiler_params=_cp("parallel", "parallel"),
        name="dn_out",
    )(o_f, o_b, pm, onorm)


def _fourier_kernel(x_ref, m1_ref, m2_ref, m3_ref, twc_ref, tws_ref, o_ref, xf_ref, bre_ref, bim_ref, *, n1, n2, scale):
    m1, m2, m3 = m1_ref[...].astype(BF16), m2_ref[...].astype(BF16), m3_ref[...].astype(BF16)
    nu = FT_UNROLL
    xf_ref[...] = x_ref[0].astype(F32)

    def stage1(i, carry):
        t2 = [i * nu + u for u in range(nu)]
        xs = [xf_ref[pl.ds(t, n1, stride=n2), :].astype(BF16) for t in t2]
        a = [_dot(m1, x) for x in xs]
        cw = [twc_ref[t] for t in t2]
        sw = [tws_ref[t] for t in t2]
        bre = [a[u][:n1] * cw[u] - a[u][n1:] * sw[u] for u in range(nu)]
        bim = [-(a[u][n1:] * cw[u] + a[u][:n1] * sw[u]) for u in range(nu)]
        for u in range(nu):
            bre_ref[pl.ds(t2[u], n1, stride=n2), :] = bre[u]
            bim_ref[pl.ds(t2[u], n1, stride=n2), :] = bim[u]
        return carry

    lax.fori_loop(0, n2 // nu, stage1, 0)

    def stage2(i, carry):
        f1 = [i * nu + u for u in range(nu)]
        r0 = [pl.multiple_of(f * n2, n2) for f in f1]
        b = [jnp.concatenate([bre_ref[pl.ds(r, n2), :], bim_ref[pl.ds(r, n2), :]], axis=0).astype(BF16) for r in r0]
        y = [_dot(m2, x) for x in b]
        z = [_dot(jnp.concatenate([yy[:n2], yy[n2:]], axis=1).astype(BF16), m3) for yy in y]
        for u in range(nu):
            o_ref[0, pl.ds(f1[u], n2, stride=n1), :] = z[u] * scale
        return carry

    lax.fori_loop(0, n1 // nu, stage2, 0)


def _dft_mats(n):
    ang = 2.0 * np.pi * ((np.arange(n)[:, None] * np.arange(n)[None, :]) % n) / n
    return np.cos(ang), np.sin(ang)


def _fourier(pm):
    b, t, _ = pm.shape
    n1 = 1 << ((t.bit_length() - 1 + 1) // 2)
    n2 = t // n1
    assert n1 * n2 == t and n1 % FT_UNROLL == 0 and n2 % FT_UNROLL == 0
    c1, s1 = _dft_mats(n1)
    c2, s2 = _dft_mats(n2)
    cc, sc = _dft_mats(HEAD_DIM)
    m1 = jnp.asarray(np.concatenate([c1, s1], axis=0), F32)
    m2 = jnp.asarray(np.block([[c2, s2], [-s2, c2]]), F32)
    m3 = jnp.asarray(np.concatenate([cc, sc], axis=0), F32)
    ang = 2.0 * np.pi * ((np.arange(n2)[:, None] * np.arange(n1)[None, :]) % t) / t
    twc = jnp.broadcast_to(jnp.asarray(np.cos(ang), F32)[:, :, None], (n2, n1, HEAD_DIM))
    tws = jnp.broadcast_to(jnp.asarray(np.sin(ang), F32)[:, :, None], (n2, n1, HEAD_DIM))
    const = lambda shape: pl.BlockSpec(shape, lambda bi, g: (0,) * len(shape))
    return pl.pallas_call(
        functools.partial(_fourier_kernel, n1=n1, n2=n2, scale=float((t * HEAD_DIM) ** -0.5)),
        grid=(b, N_HEADS),
        in_specs=[
            pl.BlockSpec((1, t, HEAD_DIM), lambda bi, g: (bi, 0, C_FT // HEAD_DIM + g)),
            const(m1.shape), const(m2.shape), const(m3.shape),
            const((n2, n1, HEAD_DIM)), const((n2, n1, HEAD_DIM)),
        ],
        out_specs=pl.BlockSpec((1, t, HEAD_DIM), lambda bi, g: (bi, 0, g)),
        out_shape=SDS((b, t, W_BR), F32),
        scratch_shapes=[pltpu.VMEM((t, HEAD_DIM), F32)] * 3,
        compiler_params=_cp("parallel", "parallel"),
        name="fourier",
    )(pm, m1, m2, m3, twc, tws)


def _sgu_kernel(u_ref, v_ref, vn_ref, ws_ref, bs_ref, o_ref):
    u = jax.nn.gelu(u_ref[0].astype(F32))
    v = jax.nn.gelu(v_ref[0].astype(F32))
    mu = jnp.mean(v, axis=-1, keepdims=True)
    var = jnp.mean(jnp.square(v - mu), axis=-1, keepdims=True)
    vb = ((v - mu) * lax.rsqrt(var + EPS) * vn_ref[...]).astype(BF16)
    out = []
    for ch in range(u.shape[0] // SG_CHUNK):
        rs = slice(ch * SG_CHUNK, (ch + 1) * SG_CHUNK)
        mix = [_dot(ws_ref[g], vb[rs, g * HEAD_DIM:(g + 1) * HEAD_DIM]) + bs_ref[:, g:g + 1] for g in range(N_HEADS)]
        out.append((u[rs, :] * jnp.concatenate(mix, axis=1)).astype(o_ref.dtype))
    for ch, y in enumerate(out):
        o_ref[0, ch * SG_CHUNK:(ch + 1) * SG_CHUNK, :] = y


def _spatial_gating(pm, ws, bs_t, vnorm):
    b, t, _ = pm.shape
    tm = _tile(t, 4 * SG_CHUNK, SG_CHUNK)
    assert tm % SG_CHUNK == 0
    return pl.pallas_call(
        _sgu_kernel,
        grid=(b, t // tm),
        in_specs=[
            pl.BlockSpec((1, tm, W_BR), lambda bi, i: (bi, i, C_SG_U // W_BR)),
            pl.BlockSpec((1, tm, W_BR), lambda bi, i: (bi, i, C_SG_V // W_BR)),
            pl.BlockSpec((1, W_BR), lambda bi, i: (0, 0)),
            pl.BlockSpec((N_HEADS, SG_CHUNK, SG_CHUNK), lambda bi, i: (0, 0, 0)),
            pl.BlockSpec((SG_CHUNK, N_HEADS), lambda bi, i: (0, 0)),
        ],
        out_specs=pl.BlockSpec((1, tm, W_BR), lambda bi, i: (bi, i, 0)),
        out_shape=SDS((b, t, W_BR), BF16),
        compiler_params=_cp("parallel", "parallel"),
        name="sgu",
    )(pm, pm, vnorm, ws, bs_t)


def _merge_kernel(h_ref, y0_ref, y1_ref, y2_ref, y3_ref, wg_ref, bg_ref, wb_ref, o_ref):
    h = h_ref[0]
    acc = None
    for i, y_ref in enumerate((y0_ref, y1_ref, y2_ref, y3_ref)):
        gate = jax.nn.sigmoid(_dot(h, wg_ref[i]) + bg_ref[i])
        term = gate * _dot(y_ref[0].astype(BF16), wb_ref[i])
        acc = term if acc is None else acc + term
    o_ref[0] = acc.astype(o_ref.dtype)


def _merge(h, ys, wg, bg, wb):
    bx, t, d = h.shape
    tm, tn = _tile(t, 1024), _tile(d, 512, LANES)
    y_spec = pl.BlockSpec((1, tm, W_BR), lambda b, i, j: (b, i, 0))
    return pl.pallas_call(
        _merge_kernel,
        grid=(bx, t // tm, d // tn),
        in_specs=[
            pl.BlockSpec((1, tm, d), lambda b, i, j: (b, i, 0)),
            y_spec, y_spec, y_spec, y_spec,
            pl.BlockSpec((N_BRANCH, d, tn), lambda b, i, j: (0, 0, j)),
            pl.BlockSpec((N_BRANCH, 1, tn), lambda b, i, j: (0, 0, j)),
            pl.BlockSpec((N_BRANCH, W_BR, tn), lambda b, i, j: (0, 0, j)),
        ],
        out_specs=pl.BlockSpec((1, tm, tn), lambda b, i, j: (b, i, j)),
        out_shape=SDS((bx, t, d), BF16),
        compiler_params=_cp("parallel", "parallel", "arbitrary"),
        name="merge",
    )(h, *ys, wg, bg.reshape(N_BRANCH, 1, d), wb)


def _resid_kernel(a_ref, w_ref, x_ref, g_ref, o_ref, *, row):
    o_ref[0] = x_ref[0] + _mod_row(g_ref, row) * _dot(a_ref[0], w_ref[...])


def _resid_proj(a, w, x, mod, gate_blk, row):
    bx, t, k = a.shape
    d = w.shape[1]
    if k * d * 2 <= RESIDENT_WEIGHT_BYTES:
        tm, tn = _tile(t, 512), d
    else:
        tm, tn = _tile(t, 1024), _tile(d, 512, LANES)
    nj = d // tn
    return pl.pallas_call(
        functools.partial(_resid_kernel, row=row),
        grid=(bx, t // tm, nj),
        in_specs=[
            pl.BlockSpec((1, tm, k), lambda b, i, j: (b, i, 0)),
            pl.BlockSpec((k, tn), lambda b, i, j: (0, j)),
            pl.BlockSpec((1, tm, tn), lambda b, i, j: (b, i, j)),
            pl.BlockSpec((MOD_ROWS, tn), lambda b, i, j: (0, gate_blk * nj + j)),
        ],
        out_specs=pl.BlockSpec((1, tm, tn), lambda b, i, j: (b, i, j)),
        out_shape=SDS((bx, t, d), F32),
        compiler_params=_cp("parallel", "parallel", "arbitrary"),
        name="resid_proj",
    )(a, w, x, mod)


def _ffn_up_kernel(x_ref, nw_ref, sh_ref, sc_ref, w1_ref, w3_ref, o_ref, hs_ref, *, row):
    @pl.when(pl.program_id(2) == 0)
    def _():
        hs_ref[...] = _norm_mod(x_ref[0], nw_ref, sh_ref, sc_ref, row)

    h = hs_ref[...]
    o_ref[0] = (_silu(_dot(h, w1_ref[...])) * _dot(h, w3_ref[...])).astype(o_ref.dtype)


def _ffn_up(x, nw, mod, w1, w3, row):
    bx, t, d = x.shape
    f = w1.shape[1]
    tm, tn = _tile(t, 1024), _tile(f, 512, LANES)
    return pl.pallas_call(
        functools.partial(_ffn_up_kernel, row=row),
        grid=(bx, t // tm, f // tn),
        in_specs=[
            pl.BlockSpec((1, tm, d), lambda b, i, j: (b, i, 0)),
            pl.BlockSpec((1, d), lambda b, i, j: (0, 0)),
            pl.BlockSpec((MOD_ROWS, d), lambda b, i, j: (0, 3)),
            pl.BlockSpec((MOD_ROWS, d), lambda b, i, j: (0, 4)),
            pl.BlockSpec((d, tn), lambda b, i, j: (0, j)),
            pl.BlockSpec((d, tn), lambda b, i, j: (0, j)),
        ],
        out_specs=pl.BlockSpec((1, tm, tn), lambda b, i, j: (b, i, j)),
        out_shape=SDS((bx, t, f), BF16),
        scratch_shapes=[pltpu.VMEM((tm, d), BF16)],
        compiler_params=_cp("parallel", "parallel", "arbitrary"),
        name="ffn_up",
    )(x, nw, mod, mod, w1, w3)


def _mixers(pm, pab, pm_c, pab_c, p, need_ctx):
    b = pm.shape[0]
    y_na = _neighbourhood_attention(pm, pm_c, p["qn"], p["kn"], p["rpb"])
    zero = jnp.zeros((b, 2 * N_HEADS, HEAD_DIM, HEAD_DIM), F32)
    of_c, ob_c, s_ctx = _deltanet(_dn_conv(pm_c, p["conv"]), pab_c, p["a_log"], p["dt_bias"], zero)
    of, ob, _ = _deltanet(_dn_conv(pm, p["conv"]), pab, p["a_log"], p["dt_bias"], s_ctx)
    ys = (y_na, _dn_out(of, ob, pm, p["onorm"]), _fourier(pm), _spatial_gating(pm, p["sg_w"], p["sg_bt"], p["sg_vn"]))
    if not need_ctx:
        return ys, None
    ys_c = (_dense_attention(pm_c, p["qn"], p["kn"]), _dn_out(of_c, ob_c, pm_c, p["onorm"]), _fourier(pm_c),
            _spatial_gating(pm_c, p["sg_w"], p["sg_bt"], p["sg_vn"]))
    return ys, ys_c


def kernel(x, c, ctx, c_ctx, w_ada, b_ada, norm1_w, norm2_w, w_in, na_qnorm, na_knorm, na_rpb, dn_conv, dn_a_log, dn_dt_bias, dn_onorm, sg_w, sg_b, sg_vnorm, w_gate, b_gate, w_branch, w_out, ffn_w1, ffn_w3, ffn_w2):
    bsz, seq, d = x.shape
    ctx_len = ctx.shape[1]
    depth = w_ada.shape[0]
    assert bsz + 1 <= MOD_ROWS and d % LANES == 0
    ctx_row = bsz

    cc = jnp.zeros((MOD_ROWS, d), F32).at[:bsz].set(c).at[ctx_row].set(c_ctx)
    mod_all = _ada_mod(cc, w_ada, b_ada)
    xc = ctx.reshape(1, bsz * ctx_len, d)

    for l in range(depth):
        need_ctx = l < depth - 1
        mod = mod_all[l]
        w_main = jnp.concatenate([w_in[l][:, :C_AB_SRC], w_in[l][:, C_AB_SRC + N_AB:]], axis=1).astype(BF16)
        w_ab = jnp.pad(w_in[l][:, C_AB_SRC:C_AB_SRC + N_AB], ((0, 0), (0, LANES - N_AB))).astype(BF16)
        p = dict(qn=na_qnorm[l][None], kn=na_knorm[l][None], rpb=na_rpb[l], conv=dn_conv[l], a_log=dn_a_log[l],
                 dt_bias=dn_dt_bias[l], onorm=dn_onorm[l][None], sg_w=sg_w[l].astype(BF16), sg_bt=sg_b[l].T,
                 sg_vn=sg_vnorm[l][None])
        wg, wb, wo = w_gate[l].astype(BF16), w_branch[l].astype(BF16), w_out[l].astype(BF16)
        w1, w3, w2 = ffn_w1[l].astype(BF16), ffn_w3[l].astype(BF16), ffn_w2[l].astype(BF16)
        n1w, n2w = norm1_w[l][None], norm2_w[l][None]

        pm, pab, h = _in_proj(x, n1w, mod, w_main, w_ab, None)
        pm_c, pab_c, h_c = _in_proj(xc, n1w, mod, w_main, w_ab, ctx_row)
        pm_c = pm_c.reshape(bsz, ctx_len, N_MAIN)
        pab_c = pab_c.reshape(bsz, ctx_len, LANES)
        ys, ys_c = _mixers(pm, pab, pm_c, pab_c, p, need_ctx)

        merged = _merge(h, ys, wg, b_gate[l], wb)
        x = _resid_proj(merged, wo, x, mod, 2, None)
        x = _resid_proj(_ffn_up(x, n2w, mod, w1, w3, None), w2, x, mod, 5, None)
        if need_ctx:
            ys_c = tuple(y.reshape(1, bsz * ctx_len, W_BR) for y in ys_c)
            merged_c = _merge(h_c, ys_c, wg, b_gate[l], wb)
            xc = _resid_proj(merged_c, wo, xc, mod, 2, ctx_row)
            xc = _resid_proj(_ffn_up(xc, n2w, mod, w1, w3, ctx_row), w2, xc, mod, 5, ctx_row)
    return x
```

```python
import functools

import numpy as np
import jax
import jax.numpy as jnp
from jax import lax
from jax.experimental import pallas as pl
from jax.experimental.pallas import tpu as pltpu

F32 = jnp.float32
BF16 = jnp.bfloat16
SDS = jax.ShapeDtypeStruct

LANES = 128
HEAD_DIM = 128
N_HEADS = 4
W_BR = N_HEADS * HEAD_DIM
N_BRANCH = 4
GRID_W = 64
NA_WIN_R = 8
NA_WIN_C = 16
NA_ROWS_PER_STEP = 8
DN_CHUNK = 64
DN_BATCH_PER_STEP = 4
DN_INV_BASE = 8
DN_CONV = 5
DN_CONV_PAD = 8
SG_CHUNK = 128
FT_UNROLL = 8
FT_PAD = 4
N_MOD = 6
EPS = 1e-6
NEG_INF = -1e30
MOD_ROWS = 8
N_AB = 4 * N_HEADS
VMEM_LIMIT_BYTES = 56 * 1024 * 1024
RESIDENT_WEIGHT_BYTES = 8 * 1024 * 1024

C_NA_Q, C_NA_K, C_NA_V = 0, W_BR, 2 * W_BR
C_DN_QKV = 3 * W_BR
C_DN_Z = 6 * W_BR
C_FT = 7 * W_BR
C_SG_U = 8 * W_BR
C_SG_V = 9 * W_BR
N_MAIN = 10 * W_BR
C_AB_SRC = 7 * W_BR


def _cp(*sem):
    return pltpu.CompilerParams(dimension_semantics=sem, vmem_limit_bytes=VMEM_LIMIT_BYTES)


def _tile(n, pref, mult=8):
    if n <= pref:
        return n
    for t in range(pref - pref % mult, 0, -mult):
        if n % t == 0:
            return t
    return n


def _dot(a, b):
    return jnp.dot(a, b, preferred_element_type=F32)


def _dot_nt(a, b):
    return lax.dot_general(a, b, (((1,), (1,)), ((), ())), preferred_element_type=F32)


def _dot_tn(a, b):
    return lax.dot_general(a, b, (((0,), (0,)), ((), ())), preferred_element_type=F32)


def _silu(x):
    return x * jax.nn.sigmoid(x)


def _rms(x, w):
    return x * lax.rsqrt(jnp.mean(x * x, axis=-1, keepdims=True) + EPS) * w


def _mod_row(ref, row):
    b = pl.program_id(0) if row is None else row
    return ref[pl.ds(b, 1), :]


def _norm_mod(x, nw_ref, sh_ref, sc_ref, row):
    return (_rms(x, nw_ref[...]) * (1.0 + _mod_row(sc_ref, row)) + _mod_row(sh_ref, row)).astype(BF16)


def _ada_kernel(c_ref, w_ref, b_ref, o_ref):
    s = _silu(c_ref[...]).astype(BF16)
    o_ref[0] = _dot(s, w_ref[0].astype(BF16)) + b_ref[0]


def _ada_mod(cc, w_ada, b_ada):
    depth, d, n = w_ada.shape
    tn = _tile(n, 1024, LANES)
    return pl.pallas_call(
        _ada_kernel,
        grid=(depth, n // tn),
        in_specs=[
            pl.BlockSpec((MOD_ROWS, d), lambda l, j: (0, 0)),
            pl.BlockSpec((1, d, tn), lambda l, j: (l, 0, j)),
            pl.BlockSpec((1, 1, tn), lambda l, j: (l, 0, j)),
        ],
        out_specs=pl.BlockSpec((1, MOD_ROWS, tn), lambda l, j: (l, 0, j)),
        out_shape=SDS((depth, MOD_ROWS, n), F32),
        compiler_params=_cp("parallel", "parallel"),
        name="ada_mod",
    )(cc, w_ada, b_ada.reshape(depth, 1, n))


def _in_proj_kernel(x_ref, nw_ref, sh_ref, sc_ref, w_ref, wab_ref, o_ref, oab_ref, h_ref, hs_ref, *, row):
    @pl.when(pl.program_id(2) == 0)
    def _():
        h = _norm_mod(x_ref[0], nw_ref, sh_ref, sc_ref, row)
        hs_ref[...] = h
        h_ref[0] = h
        oab_ref[0] = _dot(h, wab_ref[...])

    o_ref[0] = _dot(hs_ref[...], w_ref[...]).astype(o_ref.dtype)


def _in_proj(x, nw, mod, w_main, w_ab, row):
    bx, t, d = x.shape
    n = w_main.shape[1]
    tm, tn = _tile(t, 1024), _tile(n, 1024, LANES)
    return pl.pallas_call(
        functools.partial(_in_proj_kernel, row=row),
        grid=(bx, t // tm, n // tn),
        in_specs=[
            pl.BlockSpec((1, tm, d), lambda b, i, j: (b, i, 0)),
            pl.BlockSpec((1, d), lambda b, i, j: (0, 0)),
            pl.BlockSpec((MOD_ROWS, d), lambda b, i, j: (0, 0)),
            pl.BlockSpec((MOD_ROWS, d), lambda b, i, j: (0, 1)),
            pl.BlockSpec((d, tn), lambda b, i, j: (0, j)),
            pl.BlockSpec((d, LANES), lambda b, i, j: (0, 0)),
        ],
        out_specs=[
            pl.BlockSpec((1, tm, tn), lambda b, i, j: (b, i, j)),
            pl.BlockSpec((1, tm, LANES), lambda b, i, j: (b, i, 0)),
            pl.BlockSpec((1, tm, d), lambda b, i, j: (b, i, 0)),
        ],
        out_shape=[SDS((bx, t, n), BF16), SDS((bx, t, LANES), F32), SDS((bx, t, d), BF16)],
        scratch_shapes=[pltpu.VMEM((tm, d), BF16)],
        compiler_params=_cp("parallel", "parallel", "arbitrary"),
        name="in_proj",
    )(x, nw, mod, mod, w_main, w_ab)


def _na_kernel(q_ref, k_ref, v_ref, kc_ref, vc_ref, qn_ref, kn_ref, bias_ref, o_ref,
               qs, ks, vs, kcs, vcs, *, rows):
    qs[...] = (_rms(q_ref[0].astype(F32), qn_ref[...]) * HEAD_DIM ** -0.5).astype(BF16)
    ks[...] = _rms(k_ref[0].astype(F32), kn_ref[...]).astype(BF16)
    vs[...] = v_ref[0].astype(BF16)
    kcs[...] = _rms(kc_ref[0].astype(F32), kn_ref[...]).astype(BF16)
    vcs[...] = vc_ref[0].astype(BF16)
    band = NA_WIN_R * GRID_W
    nr = NA_ROWS_PER_STEP

    def body(g, carry):
        rr = [g * nr + i for i in range(nr)]
        start = [jnp.clip(r - NA_WIN_R // 2, 0, rows - NA_WIN_R) for r in rr]
        r0 = [pl.multiple_of(r * GRID_W, GRID_W) for r in rr]
        k0 = [pl.multiple_of(s * GRID_W, GRID_W) for s in start]
        q = [qs[pl.ds(x, GRID_W), :] for x in r0]
        kb = [ks[pl.ds(x, band), :] for x in k0]
        vb = [vs[pl.ds(x, band), :] for x in k0]
        bias = [bias_ref[0, r - s] for r, s in zip(rr, start)]
        kc, vc = kcs[...], vcs[...]
        s_loc = [_dot_nt(q[i], kb[i]) + bias[i] for i in range(nr)]
        s_ctx = [_dot_nt(q[i], kc) for i in range(nr)]
        m = [jnp.maximum(jnp.max(s_loc[i], axis=-1, keepdims=True), jnp.max(s_ctx[i], axis=-1, keepdims=True))
             for i in range(nr)]
        p = [jnp.exp(s_loc[i] - m[i]) for i in range(nr)]
        pc = [jnp.exp(s_ctx[i] - m[i]) for i in range(nr)]
        denom = [jnp.sum(p[i], axis=-1, keepdims=True) + jnp.sum(pc[i], axis=-1, keepdims=True) for i in range(nr)]
        o = [_dot(p[i].astype(BF16), vb[i]) + _dot(pc[i].astype(BF16), vc) for i in range(nr)]
        for i in range(nr):
            o_ref[0, pl.ds(r0[i], GRID_W), :] = (o[i] / denom[i]).astype(o_ref.dtype)
        return carry

    lax.fori_loop(0, rows // nr, body, 0)


def _na_bias_table(rpb):
    col = np.arange(GRID_W)
    dc = np.clip(col[None, :] - col[:, None], 1 - NA_WIN_C, NA_WIN_C - 1) + NA_WIN_C - 1
    col_start = np.clip(col - NA_WIN_C // 2, 0, GRID_W - NA_WIN_C)
    in_win = (col[None, :] >= col_start[:, None]) & (col[None, :] < col_start[:, None] + NA_WIN_C)
    rows = jnp.stack([rpb[:, NA_WIN_R - 1 - typ:2 * NA_WIN_R - 1 - typ, :] for typ in range(NA_WIN_R)], axis=1)
    onehot = (dc[None] == np.arange(2 * NA_WIN_C - 1)[:, None, None]).astype(np.float32)
    b = jnp.einsum('htjc,cqk->htqjk', rows.astype(F32), onehot, precision=lax.Precision.HIGHEST)
    b = jnp.where(in_win[None, None, :, None, :], b, NEG_INF)
    return b.reshape(rpb.shape[0], NA_WIN_R, GRID_W, NA_WIN_R * GRID_W)


def _neighbourhood_attention(pm, pm_c, qn, kn, rpb):
    b, t, _ = pm.shape
    ctx_len = pm_c.shape[1]
    rows = t // GRID_W
    assert rows >= NA_WIN_R and rows * GRID_W == t and rows % NA_ROWS_PER_STEP == 0
    band = NA_WIN_R * GRID_W
    hb = lambda off: (lambda bi, h: (bi, 0, off // HEAD_DIM + h))
    return pl.pallas_call(
        functools.partial(_na_kernel, rows=rows),
        grid=(b, N_HEADS),
        in_specs=[
            pl.BlockSpec((1, t, HEAD_DIM), hb(C_NA_Q)),
            pl.BlockSpec((1, t, HEAD_DIM), hb(C_NA_K)),
            pl.BlockSpec((1, t, HEAD_DIM), hb(C_NA_V)),
            pl.BlockSpec((1, ctx_len, HEAD_DIM), hb(C_NA_K)),
            pl.BlockSpec((1, ctx_len, HEAD_DIM), hb(C_NA_V)),
            pl.BlockSpec((1, HEAD_DIM), lambda bi, h: (0, 0)),
            pl.BlockSpec((1, HEAD_DIM), lambda bi, h: (0, 0)),
            pl.BlockSpec((1, NA_WIN_R, GRID_W, band), lambda bi, h: (h, 0, 0, 0)),
        ],
        out_specs=pl.BlockSpec((1, t, HEAD_DIM), lambda bi, h: (bi, 0, h)),
        out_shape=SDS((b, t, W_BR), BF16),
        scratch_shapes=[pltpu.VMEM((t, HEAD_DIM), BF16)] * 3 + [pltpu.VMEM((ctx_len, HEAD_DIM), BF16)] * 2,
        compiler_params=_cp("parallel", "parallel"),
        name="natten",
    )(pm, pm, pm, pm_c, pm_c, qn, kn, _na_bias_table(rpb))


def _ctx_attn_kernel(q_ref, k_ref, v_ref, qn_ref, kn_ref, o_ref):
    q = _rms(q_ref[0].astype(F32), qn_ref[...]).astype(BF16)
    k = _rms(k_ref[0].astype(F32), kn_ref[...]).astype(BF16)
    s = _dot_nt(q, k) * HEAD_DIM ** -0.5
    p = jnp.exp(s - jnp.max(s, axis=-1, keepdims=True))
    o = _dot(p.astype(BF16), v_ref[0].astype(BF16)) / jnp.sum(p, axis=-1, keepdims=True)
    o_ref[0] = o.astype(o_ref.dtype)


def _dense_attention(pm_c, qn, kn):
    b, ctx_len, _ = pm_c.shape
    hb = lambda off: (lambda bi, h: (bi, 0, off // HEAD_DIM + h))
    return pl.pallas_call(
        _ctx_attn_kernel,
        grid=(b, N_HEADS),
        in_specs=[
            pl.BlockSpec((1, ctx_len, HEAD_DIM), hb(C_NA_Q)),
            pl.BlockSpec((1, ctx_len, HEAD_DIM), hb(C_NA_K)),
            pl.BlockSpec((1, ctx_len, HEAD_DIM), hb(C_NA_V)),
            pl.BlockSpec((1, HEAD_DIM), lambda bi, h: (0, 0)),
            pl.BlockSpec((1, HEAD_DIM), lambda bi, h: (0, 0)),
        ],
        out_specs=pl.BlockSpec((1, ctx_len, HEAD_DIM), lambda bi, h: (bi, 0, h)),
        out_shape=SDS((b, ctx_len, W_BR), BF16),
        compiler_params=_cp("parallel", "parallel"),
        name="ctx_attn",
    )(pm_c, pm_c, pm_c, qn, kn)


def _dn_conv_kernel(x_ref, w_ref, o_ref, xp_ref, *, n_norm_tiles):
    x = x_ref[0].astype(F32)
    t = x.shape[0]
    border = jnp.zeros((DN_CONV_PAD, x.shape[1]), F32)
    xp_ref[0:DN_CONV_PAD, :] = border
    xp_ref[DN_CONV_PAD + t:, :] = border
    xp_ref[DN_CONV_PAD:DN_CONV_PAD + t, :] = x
    half = DN_CONV // 2
    acc = x * w_ref[half:half + 1, :]
    for kk in range(DN_CONV):
        d = kk - half
        if d != 0:
            acc = acc + xp_ref[DN_CONV_PAD + d:DN_CONV_PAD + d + t, :] * w_ref[kk:kk + 1, :]
    y = _silu(acc)
    unit = y * lax.rsqrt(jnp.sum(y * y, axis=-1, keepdims=True) + EPS)
    o_ref[0] = jnp.where(pl.program_id(1) < n_norm_tiles, unit, y)


def _dn_conv(pm, conv_w):
    b, t, _ = pm.shape
    n_tiles = 3 * N_HEADS
    return pl.pallas_call(
        functools.partial(_dn_conv_kernel, n_norm_tiles=2 * N_HEADS),
        grid=(b, n_tiles),
        in_specs=[
            pl.BlockSpec((1, t, HEAD_DIM), lambda bi, j: (bi, 0, C_DN_QKV // HEAD_DIM + j)),
            pl.BlockSpec((DN_CONV, HEAD_DIM), lambda bi, j: (0, j)),
        ],
        out_specs=pl.BlockSpec((1, t, HEAD_DIM), lambda bi, j: (bi, 0, j)),
        out_shape=SDS((b, t, 3 * W_BR), F32),
        scratch_shapes=[pltpu.VMEM((t + 2 * DN_CONV_PAD, HEAD_DIM), F32)],
        compiler_params=_cp("parallel", "parallel"),
        name="dn_conv",
    )(pm, conv_w)


def _softplus(x):
    return jnp.maximum(x, 0.0) + jnp.log1p(jnp.exp(-jnp.abs(x)))


def _block_masks(n):
    ii = lax.broadcasted_iota(jnp.int32, (n, n), 0)
    jj = lax.broadcasted_iota(jnp.int32, (n, n), 1)
    same = lambda s: (ii // s) == (jj // s)
    masks, s = [same(DN_INV_BASE)], DN_INV_BASE
    while s < n:
        masks.append(same(2 * s) & jnp.logical_not(same(s)))
        s *= 2
    return masks


def _unit_triangular_inverses(a_list, eye, masks):
    mm = lambda x, y: _dot(x.astype(BF16), y.astype(BF16))
    n = len(a_list)
    pw = [jnp.where(masks[0], a, 0.0) for a in a_list]
    inv = [eye - p for p in pw]
    s = 2
    while s < DN_INV_BASE:
        pw = [mm(p, p) for p in pw]
        inv = [inv[i] + mm(inv[i], pw[i]) for i in range(n)]
        s *= 2
    for m in masks[1:]:
        t = [mm(jnp.where(m, a_list[i], 0.0), inv[i]) for i in range(n)]
        inv = [inv[i] - mm(inv[i], t[i]) for i in range(n)]
    return inv


def _dn_kernel(qf_ref, kf_ref, vf_ref, qb_ref, kb_ref, vb_ref, abf_ref, abb_ref, abtf_ref, abtb_ref,
               prow_ref, pcol_ref, s0_ref, of_ref, ob_ref, sfin_ref, s_ref):
    c = pl.program_id(1)

    @pl.when(c == 0)
    def _():
        s_ref[...] = s0_ref[...].reshape(s_ref.shape)

    cs = DN_CHUNK
    ii = lax.broadcasted_iota(jnp.int32, (cs, cs), 0)
    jj = lax.broadcasted_iota(jnp.int32, (cs, cs), 1)
    eye = (ii == jj).astype(F32)
    blk_masks = _block_masks(cs)
    scale = HEAD_DIM ** -0.5
    hi = lax.Precision.HIGHEST

    nb = qf_ref.shape[0]
    nst = 2 * N_HEADS
    chains = [(bb, d, h) for bb in range(nb) for d in range(2) for h in range(N_HEADS)]
    nch = len(chains)
    incl = [(jj <= ii), (jj >= ii)]
    strict = [(jj < ii), (jj > ii)]
    gc_col, gc_row, beta_all = {}, {}, {}
    for bb in range(nb):
        for d in range(2):
            tri = incl[d].astype(F32)
            ab = (abf_ref if d == 0 else abb_ref)[bb]
            abt = (abtf_ref if d == 0 else abtb_ref)[bb, 0]
            g_col = -jnp.exp(prow_ref[0:1, :]) * _softplus(ab + prow_ref[1:2, :])
            g_row = -jnp.exp(pcol_ref[:, 0:1]) * _softplus(abt + pcol_ref[:, 1:2])
            gc_col[bb, d] = jnp.dot(tri, g_col, preferred_element_type=F32, precision=hi)
            gc_row[bb, d] = lax.dot_general(g_row, tri, (((1,), (1,)), ((), ())),
                                            preferred_element_type=F32, precision=hi)
            beta_all[bb, d] = jax.nn.sigmoid(ab)
    qkv_refs = ((qf_ref, kf_ref, vf_ref), (qb_ref, kb_ref, vb_ref))
    sl = [slice(h * HEAD_DIM, (h + 1) * HEAD_DIM) for h in range(N_HEADS)]
    col = [d * N_HEADS + h for bb, d, h in chains]
    q = [qkv_refs[d][0][bb, :, sl[h]] * scale for bb, d, h in chains]
    k = [qkv_refs[d][1][bb, :, sl[h]] for bb, d, h in chains]
    v = [qkv_refs[d][2][bb, :, sl[h]] for bb, d, h in chains]
    s_old = [s_ref[bb * nst + col[j]] for j, (bb, d, h) in enumerate(chains)]
    gcc = [gc_col[bb, d][:, col[j]:col[j] + 1] for j, (bb, d, h) in enumerate(chains)]
    gcr = [gc_row[bb, d][col[j]:col[j] + 1, :] for j, (bb, d, h) in enumerate(chains)]
    beta = [beta_all[bb, d][:, nst + col[j]:nst + col[j] + 1] for j, (bb, d, h) in enumerate(chains)]
    decay = [jnp.exp(jnp.where(incl[d], gcc[j] - gcr[j], -jnp.inf)) for j, (bb, d, h) in enumerate(chains)]
    eg = [jnp.exp(x) for x in gcc]
    g_last = [gcc[j][cs - 1:cs, :] if d == 0 else gcc[j][0:1, :] for j, (bb, d, h) in enumerate(chains)]
    kbf = [x.astype(BF16) for x in k]
    kbeta = [k[j] * beta[j] for j in range(nch)]
    kq = [_dot_nt(jnp.concatenate([kbeta[j], q[j]], axis=0).astype(BF16), kbf[j]) for j in range(nch)]
    a_mat = [jnp.where(strict[d], kq[j][:cs] * decay[j], 0.0) for j, (bb, d, h) in enumerate(chains)]
    qk = [jnp.where(incl[d], kq[j][cs:] * decay[j], 0.0) for j, (bb, d, h) in enumerate(chains)]
    inv = _unit_triangular_inverses(a_mat, eye, blk_masks)
    rhs = [jnp.concatenate([v[j] * beta[j], kbeta[j] * eg[j]], axis=1).astype(BF16) for j in range(nch)]
    uw = [_dot(inv[j].astype(BF16), rhs[j]) for j in range(nch)]
    ws_qs = [_dot(jnp.concatenate([uw[j][:, HEAD_DIM:], q[j] * eg[j]], axis=0).astype(BF16), s_old[j].astype(BF16))
             for j in range(nch)]
    v_new = [uw[j][:, :HEAD_DIM] - ws_qs[j][:cs] for j in range(nch)]
    vnb = [x.astype(BF16) for x in v_new]
    o = [ws_qs[j][cs:] + _dot(qk[j].astype(BF16), vnb[j]) for j in range(nch)]
    kd = [(k[j] * jnp.exp(g_last[j] - gcc[j])).astype(BF16) for j in range(nch)]
    s_new = [s_old[j] * jnp.exp(g_last[j]) + _dot_tn(kd[j], vnb[j]) for j in range(nch)]
    for j, (bb, d, h) in enumerate(chains):
        s_ref[bb * nst + col[j]] = s_new[j]
        (of_ref if d == 0 else ob_ref)[bb, :, sl[h]] = o[j]

    @pl.when(c == pl.num_programs(1) - 1)
    def _():
        sfin_ref[...] = s_ref[...].reshape(sfin_ref.shape)


def _deltanet(act, pab, a_log, dt_bias, s0):
    b, t, _ = act.shape
    n = t // DN_CHUNK
    assert n * DN_CHUNK == t
    abt = jnp.swapaxes(pab[..., :N_AB].reshape(b, n, DN_CHUNK, N_AB), 2, 3)
    al, dt = a_log.reshape(-1).astype(F32), dt_bias.reshape(-1).astype(F32)
    prow = jnp.zeros((2, LANES), F32).at[0, :al.size].set(al).at[1, :dt.size].set(dt)
    pcol = jnp.zeros((N_AB, 2), F32).at[:al.size, 0].set(al).at[:dt.size, 1].set(dt)
    nb = DN_BATCH_PER_STEP if b % DN_BATCH_PER_STEP == 0 else 1
    fwd = lambda part: (lambda bi, c: (bi, c, part))
    bwd = lambda part: (lambda bi, c: (bi, n - 1 - c, part))
    qkv_spec = lambda imap: pl.BlockSpec((nb, DN_CHUNK, W_BR), imap)
    n_state = 2 * N_HEADS
    state_spec = pl.BlockSpec((nb, n_state, HEAD_DIM, HEAD_DIM), lambda bi, c: (bi, 0, 0, 0))
    return pl.pallas_call(
        _dn_kernel,
        grid=(b // nb, n),
        in_specs=[
            qkv_spec(fwd(0)), qkv_spec(fwd(1)), qkv_spec(fwd(2)),
            qkv_spec(bwd(0)), qkv_spec(bwd(1)), qkv_spec(bwd(2)),
            pl.BlockSpec((nb, DN_CHUNK, LANES), fwd(0)),
            pl.BlockSpec((nb, DN_CHUNK, LANES), bwd(0)),
            pl.BlockSpec((nb, 1, N_AB, DN_CHUNK), lambda bi, c: (bi, c, 0, 0)),
            pl.BlockSpec((nb, 1, N_AB, DN_CHUNK), lambda bi, c: (bi, n - 1 - c, 0, 0)),
            pl.BlockSpec((2, LANES), lambda bi, c: (0, 0)),
            pl.BlockSpec((N_AB, 2), lambda bi, c: (0, 0)),
            state_spec,
        ],
        out_specs=[
            pl.BlockSpec((nb, DN_CHUNK, W_BR), fwd(0)),
            pl.BlockSpec((nb, DN_CHUNK, W_BR), bwd(0)),
            state_spec,
        ],
        out_shape=[SDS((b, t, W_BR), F32), SDS((b, t, W_BR), F32),
                   SDS((b, n_state, HEAD_DIM, HEAD_DIM), F32)],
        scratch_shapes=[pltpu.VMEM((nb * n_state, HEAD_DIM, HEAD_DIM), F32)],
        compiler_params=_cp("parallel", "arbitrary"),
        name="deltanet",
    )(act, act, act, act, act, act, pab, pab, abt, abt, prow, pcol, s0)


def _dn_out_kernel(of_ref, ob_ref, z_ref, w_ref, y_ref):
    o = of_ref[0] + ob_ref[0]
    z = z_ref[0].astype(F32)
    for h in range(N_HEADS):
        sl = slice(h * HEAD_DIM, (h + 1) * HEAD_DIM)
        y_ref[0, :, sl] = (_rms(o[:, sl], w_ref[...]) * _silu(z[:, sl])).astype(y_ref.dtype)


def _dn_out(o_f, o_b, pm, onorm):
    b, t, _ = o_f.shape
    tm = _tile(t, 1024)
    return pl.pallas_call(
        _dn_out_kernel,
        grid=(b, t // tm),
        in_specs=[
            pl.BlockSpec((1, tm, W_BR), lambda bi, i: (bi, i, 0)),
            pl.BlockSpec((1, tm, W_BR), lambda bi, i: (bi, i, 0)),
            pl.BlockSpec((1, tm, W_BR), lambda bi, i: (bi, i, C_DN_Z // W_BR)),
            pl.BlockSpec((1, HEAD_DIM), lambda bi, i: (0, 0)),
        ],
        out_specs=pl.BlockSpec((1, tm, W_BR), lambda bi, i: (bi, i, 0)),
        out_shape=SDS((b, t, W_BR), BF16),
        compiler_params=_cp("parallel", "parallel"),
        name="dn_out",
    )(o_f, o_b, pm, onorm)


def _fourier_kernel(x_ref, m1_ref, m2_ref, m3_ref, twc_ref, tws_ref, o_ref, xf_ref, bre_ref, bim_ref, zo_ref,
                    *, n1, n2, scale):
    m1, m2, m3 = m1_ref[...].astype(BF16), m2_ref[...].astype(BF16), m3_ref[...].astype(BF16)
    nu = FT_UNROLL
    p2, p1 = n2 + FT_PAD, n1 + FT_PAD
    for t1 in range(n1):
        xf_ref[t1 * p2:t1 * p2 + n2, :] = x_ref[0, t1 * n2:(t1 + 1) * n2, :].astype(F32)

    def stage1(i, carry):
        t2 = [i * nu + u for u in range(nu)]
        xs = [xf_ref[pl.ds(t, n1, stride=p2), :].astype(BF16) for t in t2]
        a = [_dot(m1, x) for x in xs]
        cw = [twc_ref[t] for t in t2]
        sw = [tws_ref[t] for t in t2]
        bre = [a[u][:n1] * cw[u] - a[u][n1:] * sw[u] for u in range(nu)]
        bim = [-(a[u][n1:] * cw[u] + a[u][:n1] * sw[u]) for u in range(nu)]
        for u in range(nu):
            bre_ref[pl.ds(t2[u], n1, stride=p2), :] = bre[u]
            bim_ref[pl.ds(t2[u], n1, stride=p2), :] = bim[u]
        return carry

    lax.fori_loop(0, n2 // nu, stage1, 0)

    def stage2(i, carry):
        f1 = [i * nu + u for u in range(nu)]
        r0 = [pl.multiple_of(f * p2, FT_PAD) for f in f1]
        b = [jnp.concatenate([bre_ref[pl.ds(r, n2), :], bim_ref[pl.ds(r, n2), :]], axis=0).astype(BF16) for r in r0]
        y = [_dot(m2, x) for x in b]
        z = [_dot(jnp.concatenate([yy[:n2], yy[n2:]], axis=1).astype(BF16), m3) for yy in y]
        for u in range(nu):
            zo_ref[pl.ds(f1[u], n2, stride=p1), :] = z[u] * scale
        return carry

    lax.fori_loop(0, n1 // nu, stage2, 0)
    for f2 in range(n2):
        o_ref[0, f2 * n1:(f2 + 1) * n1, :] = zo_ref[f2 * p1:f2 * p1 + n1, :]


def _dft_mats(n):
    ang = 2.0 * np.pi * ((np.arange(n)[:, None] * np.arange(n)[None, :]) % n) / n
    return np.cos(ang), np.sin(ang)


def _fourier(pm):
    b, t, _ = pm.shape
    n1 = 1 << ((t.bit_length() - 1 + 1) // 2)
    n2 = t // n1
    assert n1 * n2 == t and n1 % FT_UNROLL == 0 and n2 % FT_UNROLL == 0
    c1, s1 = _dft_mats(n1)
    c2, s2 = _dft_mats(n2)
    cc, sc = _dft_mats(HEAD_DIM)
    m1 = jnp.asarray(np.concatenate([c1, s1], axis=0), F32)
    m2 = jnp.asarray(np.block([[c2, s2], [-s2, c2]]), F32)
    m3 = jnp.asarray(np.concatenate([cc, sc], axis=0), F32)
    ang = 2.0 * np.pi * ((np.arange(n2)[:, None] * np.arange(n1)[None, :]) % t) / t
    twc = jnp.broadcast_to(jnp.asarray(np.cos(ang), F32)[:, :, None], (n2, n1, HEAD_DIM))
    tws = jnp.broadcast_to(jnp.asarray(np.sin(ang), F32)[:, :, None], (n2, n1, HEAD_DIM))
    const = lambda shape: pl.BlockSpec(shape, lambda bi, g: (0,) * len(shape))
    return pl.pallas_call(
        functools.partial(_fourier_kernel, n1=n1, n2=n2, scale=float((t * HEAD_DIM) ** -0.5)),
        grid=(b, N_HEADS),
        in_specs=[
            pl.BlockSpec((1, t, HEAD_DIM), lambda bi, g: (bi, 0, C_FT // HEAD_DIM + g)),
            const(m1.shape), const(m2.shape), const(m3.shape),
            const((n2, n1, HEAD_DIM)), const((n2, n1, HEAD_DIM)),
        ],
        out_specs=pl.BlockSpec((1, t, HEAD_DIM), lambda bi, g: (bi, 0, g)),
        out_shape=SDS((b, t, W_BR), F32),
        scratch_shapes=[pltpu.VMEM((n1 * (n2 + FT_PAD), HEAD_DIM), F32)] * 3
        + [pltpu.VMEM((n2 * (n1 + FT_PAD), HEAD_DIM), F32)],
        compiler_params=_cp("parallel", "parallel"),
        name="fourier",
    )(pm, m1, m2, m3, twc, tws)


def _sgu_kernel(u_ref, v_ref, vn_ref, ws_ref, bs_ref, o_ref):
    u = jax.nn.gelu(u_ref[0].astype(F32))
    v = jax.nn.gelu(v_ref[0].astype(F32))
    mu = jnp.mean(v, axis=-1, keepdims=True)
    var = jnp.mean(jnp.square(v - mu), axis=-1, keepdims=True)
    vb = ((v - mu) * lax.rsqrt(var + EPS) * vn_ref[...]).astype(BF16)
    out = []
    for ch in range(u.shape[0] // SG_CHUNK):
        rs = slice(ch * SG_CHUNK, (ch + 1) * SG_CHUNK)
        mix = [_dot(ws_ref[g], vb[rs, g * HEAD_DIM:(g + 1) * HEAD_DIM]) + bs_ref[:, g:g + 1] for g in range(N_HEADS)]
        out.append((u[rs, :] * jnp.concatenate(mix, axis=1)).astype(o_ref.dtype))
    for ch, y in enumerate(out):
        o_ref[0, ch * SG_CHUNK:(ch + 1) * SG_CHUNK, :] = y


def _spatial_gating(pm, ws, bs_t, vnorm):
    b, t, _ = pm.shape
    tm = _tile(t, 4 * SG_CHUNK, SG_CHUNK)
    assert tm % SG_CHUNK == 0
    return pl.pallas_call(
        _sgu_kernel,
        grid=(b, t // tm),
        in_specs=[
            pl.BlockSpec((1, tm, W_BR), lambda bi, i: (bi, i, C_SG_U // W_BR)),
            pl.BlockSpec((1, tm, W_BR), lambda bi, i: (bi, i, C_SG_V // W_BR)),
            pl.BlockSpec((1, W_BR), lambda bi, i: (0, 0)),
            pl.BlockSpec((N_HEADS, SG_CHUNK, SG_CHUNK), lambda bi, i: (0, 0, 0)),
            pl.BlockSpec((SG_CHUNK, N_HEADS), lambda bi, i: (0, 0)),
        ],
        out_specs=pl.BlockSpec((1, tm, W_BR), lambda bi, i: (bi, i, 0)),
        out_shape=SDS((b, t, W_BR), BF16),
        compiler_params=_cp("parallel", "parallel"),
        name="sgu",
    )(pm, pm, vnorm, ws, bs_t)


def _merge_kernel(h_ref, y0_ref, y1_ref, y2_ref, y3_ref, wg_ref, bg_ref, wb_ref, o_ref):
    h = h_ref[0]
    acc = None
    for i, y_ref in enumerate((y0_ref, y1_ref, y2_ref, y3_ref)):
        gate = jax.nn.sigmoid(_dot(h, wg_ref[i]) + bg_ref[i])
        term = gate * _dot(y_ref[0].astype(BF16), wb_ref[i])
        acc = term if acc is None else acc + term
    o_ref[0] = acc.astype(o_ref.dtype)


def _merge(h, ys, wg, bg, wb):
    bx, t, d = h.shape
    tm, tn = _tile(t, 1024), _tile(d, 512, LANES)
    y_spec = pl.BlockSpec((1, tm, W_BR), lambda b, i, j: (b, i, 0))
    return pl.pallas_call(
        _merge_kernel,
        grid=(bx, t // tm, d // tn),
        in_specs=[
            pl.BlockSpec((1, tm, d), lambda b, i, j: (b, i, 0)),
            y_spec, y_spec, y_spec, y_spec,
            pl.BlockSpec((N_BRANCH, d, tn), lambda b, i, j: (0, 0, j)),
            pl.BlockSpec((N_BRANCH, 1, tn), lambda b, i, j: (0, 0, j)),
            pl.BlockSpec((N_BRANCH, W_BR, tn), lambda b, i, j: (0, 0, j)),
        ],
        out_specs=pl.BlockSpec((1, tm, tn), lambda b, i, j: (b, i, j)),
        out_shape=SDS((bx, t, d), BF16),
        compiler_params=_cp("parallel", "parallel", "arbitrary"),
        name="merge",
    )(h, *ys, wg, bg.reshape(N_BRANCH, 1, d), wb)


def _resid_kernel(a_ref, w_ref, x_ref, g_ref, o_ref, *, row):
    o_ref[0] = x_ref[0] + _mod_row(g_ref, row) * _dot(a_ref[0], w_ref[...])


def _resid_proj(a, w, x, mod, gate_blk, row):
    bx, t, k = a.shape
    d = w.shape[1]
    if k * d * 2 <= RESIDENT_WEIGHT_BYTES:
        tm, tn = _tile(t, 512), d
    else:
        tm, tn = _tile(t, 1024), _tile(d, 512, LANES)
    nj = d // tn
    return pl.pallas_call(
        functools.partial(_resid_kernel, row=row),
        grid=(bx, t // tm, nj),
        in_specs=[
            pl.BlockSpec((1, tm, k), lambda b, i, j: (b, i, 0)),
            pl.BlockSpec((k, tn), lambda b, i, j: (0, j)),
            pl.BlockSpec((1, tm, tn), lambda b, i, j: (b, i, j)),
            pl.BlockSpec((MOD_ROWS, tn), lambda b, i, j: (0, gate_blk * nj + j)),
        ],
        out_specs=pl.BlockSpec((1, tm, tn), lambda b, i, j: (b, i, j)),
        out_shape=SDS((bx, t, d), F32),
        compiler_params=_cp("parallel", "parallel", "arbitrary"),
        name="resid_proj",
    )(a, w, x, mod)


def _ffn_up_kernel(x_ref, nw_ref, sh_ref, sc_ref, w1_ref, w3_ref, o_ref, hs_ref, *, row):
    @pl.when(pl.program_id(2) == 0)
    def _():
        hs_ref[...] = _norm_mod(x_ref[0], nw_ref, sh_ref, sc_ref, row)

    h = hs_ref[...]
    o_ref[0] = (_silu(_dot(h, w1_ref[...])) * _dot(h, w3_ref[...])).astype(o_ref.dtype)


def _ffn_up(x, nw, mod, w1, w3, row):
    bx, t, d = x.shape
    f = w1.shape[1]
    tm, tn = _tile(t, 1024), _tile(f, 512, LANES)
    return pl.pallas_call(
        functools.partial(_ffn_up_kernel, row=row),
        grid=(bx, t // tm, f // tn),
        in_specs=[
            pl.BlockSpec((1, tm, d), lambda b, i, j: (b, i, 0)),
            pl.BlockSpec((1, d), lambda b, i, j: (0, 0)),
            pl.BlockSpec((MOD_ROWS, d), lambda b, i, j: (0, 3)),
            pl.BlockSpec((MOD_ROWS, d), lambda b, i, j: (0, 4)),
            pl.BlockSpec((d, tn), lambda b, i, j: (0, j)),
            pl.BlockSpec((d, tn), lambda b, i, j: (0, j)),
        ],
        out_specs=pl.BlockSpec((1, tm, tn), lambda b, i, j: (b, i, j)),
        out_shape=SDS((bx, t, f), BF16),
        scratch_shapes=[pltpu.VMEM((tm, d), BF16)],
        compiler_params=_cp("parallel", "parallel", "arbitrary"),
        name="ffn_up",
    )(x, nw, mod, mod, w1, w3)


def _mixers(pm, pab, pm_c, pab_c, p, need_ctx):
    b = pm.shape[0]
    y_na = _neighbourhood_attention(pm, pm_c, p["qn"], p["kn"], p["rpb"])
    zero = jnp.zeros((b, 2 * N_HEADS, HEAD_DIM, HEAD_DIM), F32)
    of_c, ob_c, s_ctx = _deltanet(_dn_conv(pm_c, p["conv"]), pab_c, p["a_log"], p["dt_bias"], zero)
    of, ob, _ = _deltanet(_dn_conv(pm, p["conv"]), pab, p["a_log"], p["dt_bias"], s_ctx)
    ys = (y_na, _dn_out(of, ob, pm, p["onorm"]), _fourier(pm), _spatial_gating(pm, p["sg_w"], p["sg_bt"], p["sg_vn"]))
    if not need_ctx:
        return ys, None
    ys_c = (_dense_attention(pm_c, p["qn"], p["kn"]), _dn_out(of_c, ob_c, pm_c, p["onorm"]), _fourier(pm_c),
            _spatial_gating(pm_c, p["sg_w"], p["sg_bt"], p["sg_vn"]))
    return ys, ys_c


def kernel(x, c, ctx, c_ctx, w_ada, b_ada, norm1_w, norm2_w, w_in, na_qnorm, na_knorm, na_rpb, dn_conv, dn_a_log, dn_dt_bias, dn_onorm, sg_w, sg_b, sg_vnorm, w_gate, b_gate, w_branch, w_out, ffn_w1, ffn_w3, ffn_w2):
    bsz, seq, d = x.shape
    ctx_len = ctx.shape[1]
    depth = w_ada.shape[0]
    assert bsz + 1 <= MOD_ROWS and d % LANES == 0
    ctx_row = bsz

    cc = jnp.zeros((MOD_ROWS, d), F32).at[:bsz].set(c).at[ctx_row].set(c_ctx)
    mod_all = _ada_mod(cc, w_ada, b_ada)
    xc = ctx.reshape(1, bsz * ctx_len, d)

    for l in range(depth):
        need_ctx = l < depth - 1
        mod = mod_all[l]
        w_in_l = w_in[l].astype(BF16)
        w_main = jnp.concatenate([w_in_l[:, :C_AB_SRC], w_in_l[:, C_AB_SRC + N_AB:]], axis=1)
        w_ab = jnp.pad(w_in_l[:, C_AB_SRC:C_AB_SRC + N_AB], ((0, 0), (0, LANES - N_AB)))
        p = dict(qn=na_qnorm[l][None], kn=na_knorm[l][None], rpb=na_rpb[l], conv=dn_conv[l], a_log=dn_a_log[l],
                 dt_bias=dn_dt_bias[l], onorm=dn_onorm[l][None], sg_w=sg_w[l].astype(BF16), sg_bt=sg_b[l].T,
                 sg_vn=sg_vnorm[l][None])
        wg, wb, wo = w_gate[l].astype(BF16), w_branch[l].astype(BF16), w_out[l].astype(BF16)
        w1, w3, w2 = ffn_w1[l].astype(BF16), ffn_w3[l].astype(BF16), ffn_w2[l].astype(BF16)
        n1w, n2w = norm1_w[l][None], norm2_w[l][None]

        pm, pab, h = _in_proj(x, n1w, mod, w_main, w_ab, None)
        pm_c, pab_c, h_c = _in_proj(xc, n1w, mod, w_main, w_ab, ctx_row)
        pm_c = pm_c.reshape(bsz, ctx_len, N_MAIN)
        pab_c = pab_c.reshape(bsz, ctx_len, LANES)
        ys, ys_c = _mixers(pm, pab, pm_c, pab_c, p, need_ctx)

        merged = _merge(h, ys, wg, b_gate[l], wb)
        x = _resid_proj(merged, wo, x, mod, 2, None)
        x = _resid_proj(_ffn_up(x, n2w, mod, w1, w3, None), w2, x, mod, 5, None)
        if need_ctx:
            ys_c = tuple(y.reshape(1, bsz * ctx_len, W_BR) for y in ys_c)
            merged_c = _merge(h_c, ys_c, wg, b_gate[l], wb)
            xc = _resid_proj(merged_c, wo, xc, mod, 2, ctx_row)
            xc = _resid_proj(_ffn_up(xc, n2w, mod, w1, w3, ctx_row), w2, xc, mod, 5, ctx_row)
    return x
```

```python
import functools

import numpy as np
import jax
import jax.numpy as jnp
from jax import lax
from jax.experimental import pallas as pl
from jax.experimental.pallas import tpu as pltpu

F32 = jnp.float32
BF16 = jnp.bfloat16
SDS = jax.ShapeDtypeStruct

LANES = 128
HEAD_DIM = 128
N_HEADS = 4
W_BR = N_HEADS * HEAD_DIM
N_BRANCH = 4
GRID_W = 64
NA_WIN_R = 8
NA_WIN_C = 16
NA_ROWS_PER_STEP = 8
DN_CHUNK = 64
DN_BATCH_PER_STEP = 4
DN_INV_BASE = 8
DN_CONV = 5
DN_CONV_PAD = 8
DN_CONV_WIDE_TILE_ROWS = 1024
SG_CHUNK = 128
FT_UNROLL = 8
FT_PAD = 4
N_MOD = 6
EPS = 1e-6
NEG_INF = -1e30
MOD_ROWS = 8
N_AB = 4 * N_HEADS
VMEM_LIMIT_BYTES = 56 * 1024 * 1024
RESIDENT_WEIGHT_BYTES = 8 * 1024 * 1024

C_NA_Q, C_NA_K, C_NA_V = 0, W_BR, 2 * W_BR
C_DN_QKV = 3 * W_BR
C_DN_Z = 6 * W_BR
C_FT = 7 * W_BR
C_SG_U = 8 * W_BR
C_SG_V = 9 * W_BR
N_MAIN = 10 * W_BR
C_AB_SRC = 7 * W_BR


def _cp(*sem):
    return pltpu.CompilerParams(dimension_semantics=sem, vmem_limit_bytes=VMEM_LIMIT_BYTES)


def _tile(n, pref, mult=8):
    if n <= pref:
        return n
    for t in range(pref - pref % mult, 0, -mult):
        if n % t == 0:
            return t
    return n


def _dot(a, b):
    return jnp.dot(a, b, preferred_element_type=F32)


def _dot_nt(a, b):
    return lax.dot_general(a, b, (((1,), (1,)), ((), ())), preferred_element_type=F32)


def _dot_tn(a, b):
    return lax.dot_general(a, b, (((0,), (0,)), ((), ())), preferred_element_type=F32)


def _silu(x):
    return x * jax.nn.sigmoid(x)


def _rms(x, w):
    return x * lax.rsqrt(jnp.mean(x * x, axis=-1, keepdims=True) + EPS) * w


def _mod_row(ref, row):
    b = pl.program_id(0) if row is None else row
    return ref[pl.ds(b, 1), :]


def _norm_mod(x, nw_ref, sh_ref, sc_ref, row):
    return (_rms(x, nw_ref[...]) * (1.0 + _mod_row(sc_ref, row)) + _mod_row(sh_ref, row)).astype(BF16)


def _ada_kernel(c_ref, w_ref, b_ref, o_ref):
    s = _silu(c_ref[...]).astype(BF16)
    o_ref[0] = _dot(s, w_ref[0].astype(BF16)) + b_ref[0]


def _ada_mod(cc, w_ada, b_ada):
    depth, d, n = w_ada.shape
    tn = _tile(n, 1024, LANES)
    return pl.pallas_call(
        _ada_kernel,
        grid=(depth, n // tn),
        in_specs=[
            pl.BlockSpec((MOD_ROWS, d), lambda l, j: (0, 0)),
            pl.BlockSpec((1, d, tn), lambda l, j: (l, 0, j)),
            pl.BlockSpec((1, 1, tn), lambda l, j: (l, 0, j)),
        ],
        out_specs=pl.BlockSpec((1, MOD_ROWS, tn), lambda l, j: (l, 0, j)),
        out_shape=SDS((depth, MOD_ROWS, n), F32),
        compiler_params=_cp("parallel", "parallel"),
        name="ada_mod",
    )(cc, w_ada, b_ada.reshape(depth, 1, n))


def _in_proj_kernel(x_ref, nw_ref, sh_ref, sc_ref, w_ref, wab_ref, o_ref, oab_ref, h_ref, hs_ref, *, row):
    @pl.when(pl.program_id(2) == 0)
    def _():
        h = _norm_mod(x_ref[0], nw_ref, sh_ref, sc_ref, row)
        hs_ref[...] = h
        h_ref[0] = h
        oab_ref[0] = _dot(h, wab_ref[0])

    o_ref[0] = _dot(hs_ref[...], w_ref[0]).astype(o_ref.dtype)


def _in_proj(x, nw, mod, w_main, w_ab, l, row):
    bx, t, d = x.shape
    n = w_main.shape[2]
    tm, tn = _tile(t, 1024), _tile(n, 1024, LANES)
    return pl.pallas_call(
        functools.partial(_in_proj_kernel, row=row),
        grid=(bx, t // tm, n // tn),
        in_specs=[
            pl.BlockSpec((1, tm, d), lambda b, i, j: (b, i, 0)),
            pl.BlockSpec((1, d), lambda b, i, j: (0, 0)),
            pl.BlockSpec((MOD_ROWS, d), lambda b, i, j: (0, 0)),
            pl.BlockSpec((MOD_ROWS, d), lambda b, i, j: (0, 1)),
            pl.BlockSpec((1, d, tn), lambda b, i, j: (l, 0, j)),
            pl.BlockSpec((1, d, LANES), lambda b, i, j: (l, 0, 0)),
        ],
        out_specs=[
            pl.BlockSpec((1, tm, tn), lambda b, i, j: (b, i, j)),
            pl.BlockSpec((1, tm, LANES), lambda b, i, j: (b, i, 0)),
            pl.BlockSpec((1, tm, d), lambda b, i, j: (b, i, 0)),
        ],
        out_shape=[SDS((bx, t, n), BF16), SDS((bx, t, LANES), F32), SDS((bx, t, d), BF16)],
        scratch_shapes=[pltpu.VMEM((tm, d), BF16)],
        compiler_params=_cp("parallel", "parallel", "arbitrary"),
        name="in_proj",
    )(x, nw, mod, mod, w_main, w_ab)


def _na_kernel(q_ref, k_ref, v_ref, kc_ref, vc_ref, qn_ref, kn_ref, bias_ref, o_ref,
               qs, ks, vs, kcs, vcs, *, rows):
    qs[...] = (_rms(q_ref[0].astype(F32), qn_ref[...]) * HEAD_DIM ** -0.5).astype(BF16)
    ks[...] = _rms(k_ref[0].astype(F32), kn_ref[...]).astype(BF16)
    vs[...] = v_ref[0].astype(BF16)
    kcs[...] = _rms(kc_ref[0].astype(F32), kn_ref[...]).astype(BF16)
    vcs[...] = vc_ref[0].astype(BF16)
    band = NA_WIN_R * GRID_W
    nr = NA_ROWS_PER_STEP

    def body(g, carry):
        rr = [g * nr + i for i in range(nr)]
        start = [jnp.clip(r - NA_WIN_R // 2, 0, rows - NA_WIN_R) for r in rr]
        r0 = [pl.multiple_of(r * GRID_W, GRID_W) for r in rr]
        k0 = [pl.multiple_of(s * GRID_W, GRID_W) for s in start]
        q = [qs[pl.ds(x, GRID_W), :] for x in r0]
        kb = [ks[pl.ds(x, band), :] for x in k0]
        vb = [vs[pl.ds(x, band), :] for x in k0]
        bias = [bias_ref[0, r - s] for r, s in zip(rr, start)]
        kc, vc = kcs[...], vcs[...]
        s_loc = [_dot_nt(q[i], kb[i]) + bias[i] for i in range(nr)]
        s_ctx = [_dot_nt(q[i], kc) for i in range(nr)]
        m = [jnp.maximum(jnp.max(s_loc[i], axis=-1, keepdims=True), jnp.max(s_ctx[i], axis=-1, keepdims=True))
             for i in range(nr)]
        p = [jnp.exp(s_loc[i] - m[i]) for i in range(nr)]
        pc = [jnp.exp(s_ctx[i] - m[i]) for i in range(nr)]
        denom = [jnp.sum(p[i], axis=-1, keepdims=True) + jnp.sum(pc[i], axis=-1, keepdims=True) for i in range(nr)]
        o = [_dot(p[i].astype(BF16), vb[i]) + _dot(pc[i].astype(BF16), vc) for i in range(nr)]
        for i in range(nr):
            o_ref[0, pl.ds(r0[i], GRID_W), :] = (o[i] / denom[i]).astype(o_ref.dtype)
        return carry

    lax.fori_loop(0, rows // nr, body, 0)


def _na_bias_table(rpb):
    col = np.arange(GRID_W)
    dc = np.clip(col[None, :] - col[:, None], 1 - NA_WIN_C, NA_WIN_C - 1) + NA_WIN_C - 1
    col_start = np.clip(col - NA_WIN_C // 2, 0, GRID_W - NA_WIN_C)
    in_win = (col[None, :] >= col_start[:, None]) & (col[None, :] < col_start[:, None] + NA_WIN_C)
    rows = jnp.stack([rpb[:, NA_WIN_R - 1 - typ:2 * NA_WIN_R - 1 - typ, :] for typ in range(NA_WIN_R)], axis=1)
    onehot = (dc[None] == np.arange(2 * NA_WIN_C - 1)[:, None, None]).astype(np.float32)
    b = jnp.einsum('htjc,cqk->htqjk', rows.astype(F32), onehot, precision=lax.Precision.HIGHEST)
    b = jnp.where(in_win[None, None, :, None, :], b, NEG_INF)
    return b.reshape(rpb.shape[0], NA_WIN_R, GRID_W, NA_WIN_R * GRID_W)


def _neighbourhood_attention(pm, pm_c, qn, kn, rpb):
    b, t, _ = pm.shape
    ctx_len = pm_c.shape[1]
    rows = t // GRID_W
    assert rows >= NA_WIN_R and rows * GRID_W == t and rows % NA_ROWS_PER_STEP == 0
    band = NA_WIN_R * GRID_W
    hb = lambda off: (lambda bi, h: (bi, 0, off // HEAD_DIM + h))
    return pl.pallas_call(
        functools.partial(_na_kernel, rows=rows),
        grid=(b, N_HEADS),
        in_specs=[
            pl.BlockSpec((1, t, HEAD_DIM), hb(C_NA_Q)),
            pl.BlockSpec((1, t, HEAD_DIM), hb(C_NA_K)),
            pl.BlockSpec((1, t, HEAD_DIM), hb(C_NA_V)),
            pl.BlockSpec((1, ctx_len, HEAD_DIM), hb(C_NA_K)),
            pl.BlockSpec((1, ctx_len, HEAD_DIM), hb(C_NA_V)),
            pl.BlockSpec((1, HEAD_DIM), lambda bi, h: (0, 0)),
            pl.BlockSpec((1, HEAD_DIM), lambda bi, h: (0, 0)),
            pl.BlockSpec((1, NA_WIN_R, GRID_W, band), lambda bi, h: (h, 0, 0, 0)),
        ],
        out_specs=pl.BlockSpec((1, t, HEAD_DIM), lambda bi, h: (bi, 0, h)),
        out_shape=SDS((b, t, W_BR), BF16),
        scratch_shapes=[pltpu.VMEM((t, HEAD_DIM), BF16)] * 3 + [pltpu.VMEM((ctx_len, HEAD_DIM), BF16)] * 2,
        compiler_params=_cp("parallel", "parallel"),
        name="natten",
    )(pm, pm, pm, pm_c, pm_c, qn, kn, _na_bias_table(rpb))


def _ctx_attn_kernel(q_ref, k_ref, v_ref, qn_ref, kn_ref, o_ref):
    q = _rms(q_ref[0].astype(F32), qn_ref[...]).astype(BF16)
    k = _rms(k_ref[0].astype(F32), kn_ref[...]).astype(BF16)
    s = _dot_nt(q, k) * HEAD_DIM ** -0.5
    p = jnp.exp(s - jnp.max(s, axis=-1, keepdims=True))
    o = _dot(p.astype(BF16), v_ref[0].astype(BF16)) / jnp.sum(p, axis=-1, keepdims=True)
    o_ref[0] = o.astype(o_ref.dtype)


def _dense_attention(pm_c, qn, kn):
    b, ctx_len, _ = pm_c.shape
    hb = lambda off: (lambda bi, h: (bi, 0, off // HEAD_DIM + h))
    return pl.pallas_call(
        _ctx_attn_kernel,
        grid=(b, N_HEADS),
        in_specs=[
            pl.BlockSpec((1, ctx_len, HEAD_DIM), hb(C_NA_Q)),
            pl.BlockSpec((1, ctx_len, HEAD_DIM), hb(C_NA_K)),
            pl.BlockSpec((1, ctx_len, HEAD_DIM), hb(C_NA_V)),
            pl.BlockSpec((1, HEAD_DIM), lambda bi, h: (0, 0)),
            pl.BlockSpec((1, HEAD_DIM), lambda bi, h: (0, 0)),
        ],
        out_specs=pl.BlockSpec((1, ctx_len, HEAD_DIM), lambda bi, h: (bi, 0, h)),
        out_shape=SDS((b, ctx_len, W_BR), BF16),
        compiler_params=_cp("parallel", "parallel"),
        name="ctx_attn",
    )(pm_c, pm_c, pm_c, qn, kn)


def _dn_conv_kernel(x_ref, w_ref, o_ref, xp_ref, *, n_norm_tiles):
    x = x_ref[0].astype(F32)
    t = x.shape[0]
    border = jnp.zeros((DN_CONV_PAD, x.shape[1]), F32)
    xp_ref[0:DN_CONV_PAD, :] = border
    xp_ref[DN_CONV_PAD + t:, :] = border
    xp_ref[DN_CONV_PAD:DN_CONV_PAD + t, :] = x
    half = DN_CONV // 2
    acc = x * w_ref[half:half + 1, :]
    for kk in range(DN_CONV):
        d = kk - half
        if d != 0:
            acc = acc + xp_ref[DN_CONV_PAD + d:DN_CONV_PAD + d + t, :] * w_ref[kk:kk + 1, :]
    y = _silu(acc)
    normed = pl.program_id(1) < n_norm_tiles
    for h in range(y.shape[1] // HEAD_DIM):
        sl = slice(h * HEAD_DIM, (h + 1) * HEAD_DIM)
        yh = y[:, sl]
        unit = yh * lax.rsqrt(jnp.sum(yh * yh, axis=-1, keepdims=True) + EPS)
        o_ref[0, :, sl] = jnp.where(normed, unit, yh)


def _dn_conv(pm, conv_w):
    b, t, _ = pm.shape
    tw = W_BR if t <= DN_CONV_WIDE_TILE_ROWS else HEAD_DIM
    return pl.pallas_call(
        functools.partial(_dn_conv_kernel, n_norm_tiles=2 * W_BR // tw),
        grid=(b, 3 * W_BR // tw),
        in_specs=[
            pl.BlockSpec((1, t, tw), lambda bi, j: (bi, 0, C_DN_QKV // tw + j)),
            pl.BlockSpec((DN_CONV, tw), lambda bi, j: (0, j)),
        ],
        out_specs=pl.BlockSpec((1, t, tw), lambda bi, j: (bi, 0, j)),
        out_shape=SDS((b, t, 3 * W_BR), F32),
        scratch_shapes=[pltpu.VMEM((t + 2 * DN_CONV_PAD, tw), F32)],
        compiler_params=_cp("parallel", "parallel"),
        name="dn_conv",
    )(pm, conv_w)


def _softplus(x):
    return jnp.maximum(x, 0.0) + jnp.log1p(jnp.exp(-jnp.abs(x)))


def _block_masks(n):
    ii = lax.broadcasted_iota(jnp.int32, (n, n), 0)
    jj = lax.broadcasted_iota(jnp.int32, (n, n), 1)
    same = lambda s: (ii // s) == (jj // s)
    masks, s = [same(DN_INV_BASE)], DN_INV_BASE
    while s < n:
        masks.append(same(2 * s) & jnp.logical_not(same(s)))
        s *= 2
    return masks


def _unit_triangular_inverses(a_list, eye, masks):
    mm = lambda x, y: _dot(x.astype(BF16), y.astype(BF16))
    n = len(a_list)
    pw = [jnp.where(masks[0], a, 0.0) for a in a_list]
    inv = [eye - p for p in pw]
    s = 2
    while s < DN_INV_BASE:
        pw = [mm(p, p) for p in pw]
        inv = [inv[i] + mm(inv[i], pw[i]) for i in range(n)]
        s *= 2
    for m in masks[1:]:
        t = [mm(jnp.where(m, a_list[i], 0.0), inv[i]) for i in range(n)]
        inv = [inv[i] - mm(inv[i], t[i]) for i in range(n)]
    return inv


def _dn_kernel(qf_ref, kf_ref, vf_ref, qb_ref, kb_ref, vb_ref, abf_ref, abb_ref, abtf_ref, abtb_ref,
               prow_ref, pcol_ref, s0_ref, of_ref, ob_ref, sfin_ref, s_ref):
    c = pl.program_id(1)

    @pl.when(c == 0)
    def _():
        s_ref[...] = s0_ref[...].reshape(s_ref.shape)

    cs = DN_CHUNK
    ii = lax.broadcasted_iota(jnp.int32, (cs, cs), 0)
    jj = lax.broadcasted_iota(jnp.int32, (cs, cs), 1)
    eye = (ii == jj).astype(F32)
    blk_masks = _block_masks(cs)
    scale = HEAD_DIM ** -0.5
    hi = lax.Precision.HIGHEST

    nb = qf_ref.shape[0]
    nst = 2 * N_HEADS
    chains = [(bb, d, h) for bb in range(nb) for d in range(2) for h in range(N_HEADS)]
    nch = len(chains)
    incl = [(jj <= ii), (jj >= ii)]
    strict = [(jj < ii), (jj > ii)]
    gc_col, gc_row, beta_all = {}, {}, {}
    for bb in range(nb):
        for d in range(2):
            tri = incl[d].astype(F32)
            ab = (abf_ref if d == 0 else abb_ref)[bb]
            abt = (abtf_ref if d == 0 else abtb_ref)[bb, 0]
            g_col = -jnp.exp(prow_ref[0:1, :]) * _softplus(ab + prow_ref[1:2, :])
            g_row = -jnp.exp(pcol_ref[:, 0:1]) * _softplus(abt + pcol_ref[:, 1:2])
            gc_col[bb, d] = jnp.dot(tri, g_col, preferred_element_type=F32, precision=hi)
            gc_row[bb, d] = lax.dot_general(g_row, tri, (((1,), (1,)), ((), ())),
                                            preferred_element_type=F32, precision=hi)
            beta_all[bb, d] = jax.nn.sigmoid(ab)
    qkv_refs = ((qf_ref, kf_ref, vf_ref), (qb_ref, kb_ref, vb_ref))
    sl = [slice(h * HEAD_DIM, (h + 1) * HEAD_DIM) for h in range(N_HEADS)]
    col = [d * N_HEADS + h for bb, d, h in chains]
    q = [qkv_refs[d][0][bb, :, sl[h]] * scale for bb, d, h in chains]
    k = [qkv_refs[d][1][bb, :, sl[h]] for bb, d, h in chains]
    v = [qkv_refs[d][2][bb, :, sl[h]] for bb, d, h in chains]
    s_old = [s_ref[bb * nst + col[j]] for j, (bb, d, h) in enumerate(chains)]
    gcc = [gc_col[bb, d][:, col[j]:col[j] + 1] for j, (bb, d, h) in enumerate(chains)]
    gcr = [gc_row[bb, d][col[j]:col[j] + 1, :] for j, (bb, d, h) in enumerate(chains)]
    beta = [beta_all[bb, d][:, nst + col[j]:nst + col[j] + 1] for j, (bb, d, h) in enumerate(chains)]
    decay = [jnp.exp(jnp.where(incl[d], gcc[j] - gcr[j], -jnp.inf)) for j, (bb, d, h) in enumerate(chains)]
    eg = [jnp.exp(x) for x in gcc]
    g_last = [gcc[j][cs - 1:cs, :] if d == 0 else gcc[j][0:1, :] for j, (bb, d, h) in enumerate(chains)]
    kbf = [x.astype(BF16) for x in k]
    kbeta = [k[j] * beta[j] for j in range(nch)]
    kq = [_dot_nt(jnp.concatenate([kbeta[j], q[j]], axis=0).astype(BF16), kbf[j]) for j in range(nch)]
    a_mat = [jnp.where(strict[d], kq[j][:cs] * decay[j], 0.0) for j, (bb, d, h) in enumerate(chains)]
    qk = [jnp.where(incl[d], kq[j][cs:] * decay[j], 0.0) for j, (bb, d, h) in enumerate(chains)]
    inv = _unit_triangular_inverses(a_mat, eye, blk_masks)
    rhs = [jnp.concatenate([v[j] * beta[j], kbeta[j] * eg[j]], axis=1).astype(BF16) for j in range(nch)]
    uw = [_dot(inv[j].astype(BF16), rhs[j]) for j in range(nch)]
    ws_qs = [_dot(jnp.concatenate([uw[j][:, HEAD_DIM:], q[j] * eg[j]], axis=0).astype(BF16), s_old[j].astype(BF16))
             for j in range(nch)]
    v_new = [uw[j][:, :HEAD_DIM] - ws_qs[j][:cs] for j in range(nch)]
    vnb = [x.astype(BF16) for x in v_new]
    o = [ws_qs[j][cs:] + _dot(qk[j].astype(BF16), vnb[j]) for j in range(nch)]
    kd = [(k[j] * jnp.exp(g_last[j] - gcc[j])).astype(BF16) for j in range(nch)]
    s_new = [s_old[j] * jnp.exp(g_last[j]) + _dot_tn(kd[j], vnb[j]) for j in range(nch)]
    for j, (bb, d, h) in enumerate(chains):
        s_ref[bb * nst + col[j]] = s_new[j]
        (of_ref if d == 0 else ob_ref)[bb, :, sl[h]] = o[j]

    @pl.when(c == pl.num_programs(1) - 1)
    def _():
        sfin_ref[...] = s_ref[...].reshape(sfin_ref.shape)


def _deltanet(act, pab, a_log, dt_bias, s0):
    b, t, _ = act.shape
    n = t // DN_CHUNK
    assert n * DN_CHUNK == t
    abt = jnp.swapaxes(pab[..., :N_AB].reshape(b, n, DN_CHUNK, N_AB), 2, 3)
    al, dt = a_log.reshape(-1).astype(F32), dt_bias.reshape(-1).astype(F32)
    prow = jnp.zeros((2, LANES), F32).at[0, :al.size].set(al).at[1, :dt.size].set(dt)
    pcol = jnp.zeros((N_AB, 2), F32).at[:al.size, 0].set(al).at[:dt.size, 1].set(dt)
    nb = DN_BATCH_PER_STEP if b % DN_BATCH_PER_STEP == 0 else 1
    fwd = lambda part: (lambda bi, c: (bi, c, part))
    bwd = lambda part: (lambda bi, c: (bi, n - 1 - c, part))
    qkv_spec = lambda imap: pl.BlockSpec((nb, DN_CHUNK, W_BR), imap)
    n_state = 2 * N_HEADS
    state_spec = pl.BlockSpec((nb, n_state, HEAD_DIM, HEAD_DIM), lambda bi, c: (bi, 0, 0, 0))
    return pl.pallas_call(
        _dn_kernel,
        grid=(b // nb, n),
        in_specs=[
            qkv_spec(fwd(0)), qkv_spec(fwd(1)), qkv_spec(fwd(2)),
            qkv_spec(bwd(0)), qkv_spec(bwd(1)), qkv_spec(bwd(2)),
            pl.BlockSpec((nb, DN_CHUNK, LANES), fwd(0)),
            pl.BlockSpec((nb, DN_CHUNK, LANES), bwd(0)),
            pl.BlockSpec((nb, 1, N_AB, DN_CHUNK), lambda bi, c: (bi, c, 0, 0)),
            pl.BlockSpec((nb, 1, N_AB, DN_CHUNK), lambda bi, c: (bi, n - 1 - c, 0, 0)),
            pl.BlockSpec((2, LANES), lambda bi, c: (0, 0)),
            pl.BlockSpec((N_AB, 2), lambda bi, c: (0, 0)),
            state_spec,
        ],
        out_specs=[
            pl.BlockSpec((nb, DN_CHUNK, W_BR), fwd(0)),
            pl.BlockSpec((nb, DN_CHUNK, W_BR), bwd(0)),
            state_spec,
        ],
        out_shape=[SDS((b, t, W_BR), F32), SDS((b, t, W_BR), F32),
                   SDS((b, n_state, HEAD_DIM, HEAD_DIM), F32)],
        scratch_shapes=[pltpu.VMEM((nb * n_state, HEAD_DIM, HEAD_DIM), F32)],
        compiler_params=_cp("parallel", "arbitrary"),
        name="deltanet",
    )(act, act, act, act, act, act, pab, pab, abt, abt, prow, pcol, s0)


def _dn_out_kernel(of_ref, ob_ref, z_ref, w_ref, y_ref):
    o = of_ref[0] + ob_ref[0]
    z = z_ref[0].astype(F32)
    for h in range(N_HEADS):
        sl = slice(h * HEAD_DIM, (h + 1) * HEAD_DIM)
        y_ref[0, :, sl] = (_rms(o[:, sl], w_ref[...]) * _silu(z[:, sl])).astype(y_ref.dtype)


def _dn_out(o_f, o_b, pm, onorm):
    b, t, _ = o_f.shape
    tm = _tile(t, 1024)
    return pl.pallas_call(
        _dn_out_kernel,
        grid=(b, t // tm),
        in_specs=[
            pl.BlockSpec((1, tm, W_BR), lambda bi, i: (bi, i, 0)),
            pl.BlockSpec((1, tm, W_BR), lambda bi, i: (bi, i, 0)),
            pl.BlockSpec((1, tm, W_BR), lambda bi, i: (bi, i, C_DN_Z // W_BR)),
            pl.BlockSpec((1, HEAD_DIM), lambda bi, i: (0, 0)),
        ],
        out_specs=pl.BlockSpec((1, tm, W_BR), lambda bi, i: (bi, i, 0)),
        out_shape=SDS((b, t, W_BR), BF16),
        compiler_params=_cp("parallel", "parallel"),
        name="dn_out",
    )(o_f, o_b, pm, onorm)


def _fourier_kernel(x_ref, m1_ref, m2_ref, m3_ref, twc_ref, tws_ref, o_ref, xf_ref, bre_ref, bim_ref, zo_ref,
                    *, n1, n2, scale):
    m1, m2, m3 = m1_ref[...].astype(BF16), m2_ref[...].astype(BF16), m3_ref[...].astype(BF16)
    nu = FT_UNROLL
    p2, p1 = n2 + FT_PAD, n1 + FT_PAD
    for t1 in range(n1):
        xf_ref[t1 * p2:t1 * p2 + n2, :] = x_ref[0, t1 * n2:(t1 + 1) * n2, :].astype(F32)

    def stage1(i, carry):
        t2 = [i * nu + u for u in range(nu)]
        xs = [xf_ref[pl.ds(t, n1, stride=p2), :].astype(BF16) for t in t2]
        a = [_dot(m1, x) for x in xs]
        cw = [twc_ref[t] for t in t2]
        sw = [tws_ref[t] for t in t2]
        bre = [a[u][:n1] * cw[u] - a[u][n1:] * sw[u] for u in range(nu)]
        bim = [-(a[u][n1:] * cw[u] + a[u][:n1] * sw[u]) for u in range(nu)]
        for u in range(nu):
            bre_ref[pl.ds(t2[u], n1, stride=p2), :] = bre[u]
            bim_ref[pl.ds(t2[u], n1, stride=p2), :] = bim[u]
        return carry

    lax.fori_loop(0, n2 // nu, stage1, 0)

    def stage2(i, carry):
        f1 = [i * nu + u for u in range(nu)]
        r0 = [pl.multiple_of(f * p2, FT_PAD) for f in f1]
        b = [jnp.concatenate([bre_ref[pl.ds(r, n2), :], bim_ref[pl.ds(r, n2), :]], axis=0).astype(BF16) for r in r0]
        y = [_dot(m2, x) for x in b]
        z = [_dot(jnp.concatenate([yy[:n2], yy[n2:]], axis=1).astype(BF16), m3) for yy in y]
        for u in range(nu):
            zo_ref[pl.ds(f1[u], n2, stride=p1), :] = z[u] * scale
        return carry

    lax.fori_loop(0, n1 // nu, stage2, 0)
    for f2 in range(n2):
        o_ref[0, f2 * n1:(f2 + 1) * n1, :] = zo_ref[f2 * p1:f2 * p1 + n1, :]


def _dft_mats(n):
    ang = 2.0 * np.pi * ((np.arange(n)[:, None] * np.arange(n)[None, :]) % n) / n
    return np.cos(ang), np.sin(ang)


def _fourier(pm):
    b, t, _ = pm.shape
    n1 = 1 << ((t.bit_length() - 1 + 1) // 2)
    n2 = t // n1
    assert n1 * n2 == t and n1 % FT_UNROLL == 0 and n2 % FT_UNROLL == 0
    c1, s1 = _dft_mats(n1)
    c2, s2 = _dft_mats(n2)
    cc, sc = _dft_mats(HEAD_DIM)
    m1 = jnp.asarray(np.concatenate([c1, s1], axis=0), F32)
    m2 = jnp.asarray(np.block([[c2, s2], [-s2, c2]]), F32)
    m3 = jnp.asarray(np.concatenate([cc, sc], axis=0), F32)
    ang = 2.0 * np.pi * ((np.arange(n2)[:, None] * np.arange(n1)[None, :]) % t) / t
    twc = jnp.broadcast_to(jnp.asarray(np.cos(ang), F32)[:, :, None], (n2, n1, HEAD_DIM))
    tws = jnp.broadcast_to(jnp.asarray(np.sin(ang), F32)[:, :, None], (n2, n1, HEAD_DIM))
    const = lambda shape: pl.BlockSpec(shape, lambda bi, g: (0,) * len(shape))
    return pl.pallas_call(
        functools.partial(_fourier_kernel, n1=n1, n2=n2, scale=float((t * HEAD_DIM) ** -0.5)),
        grid=(b, N_HEADS),
        in_specs=[
            pl.BlockSpec((1, t, HEAD_DIM), lambda bi, g: (bi, 0, C_FT // HEAD_DIM + g)),
            const(m1.shape), const(m2.shape), const(m3.shape),
            const((n2, n1, HEAD_DIM)), const((n2, n1, HEAD_DIM)),
        ],
        out_specs=pl.BlockSpec((1, t, HEAD_DIM), lambda bi, g: (bi, 0, g)),
        out_shape=SDS((b, t, W_BR), F32),
        scratch_shapes=[pltpu.VMEM((n1 * (n2 + FT_PAD), HEAD_DIM), F32)] * 3
        + [pltpu.VMEM((n2 * (n1 + FT_PAD), HEAD_DIM), F32)],
        compiler_params=_cp("parallel", "parallel"),
        name="fourier",
    )(pm, m1, m2, m3, twc, tws)


def _sgu_kernel(u_ref, v_ref, vn_ref, ws_ref, bs_ref, o_ref):
    u = jax.nn.gelu(u_ref[0].astype(F32))
    v = jax.nn.gelu(v_ref[0].astype(F32))
    mu = jnp.mean(v, axis=-1, keepdims=True)
    var = jnp.mean(jnp.square(v - mu), axis=-1, keepdims=True)
    vb = ((v - mu) * lax.rsqrt(var + EPS) * vn_ref[...]).astype(BF16)
    out = []
    for ch in range(u.shape[0] // SG_CHUNK):
        rs = slice(ch * SG_CHUNK, (ch + 1) * SG_CHUNK)
        mix = [_dot(ws_ref[g], vb[rs, g * HEAD_DIM:(g + 1) * HEAD_DIM]) + bs_ref[:, g:g + 1] for g in range(N_HEADS)]
        out.append((u[rs, :] * jnp.concatenate(mix, axis=1)).astype(o_ref.dtype))
    for ch, y in enumerate(out):
        o_ref[0, ch * SG_CHUNK:(ch + 1) * SG_CHUNK, :] = y


def _spatial_gating(pm, ws, bs_t, vnorm):
    b, t, _ = pm.shape
    tm = _tile(t, 4 * SG_CHUNK, SG_CHUNK)
    assert tm % SG_CHUNK == 0
    return pl.pallas_call(
        _sgu_kernel,
        grid=(b, t // tm),
        in_specs=[
            pl.BlockSpec((1, tm, W_BR), lambda bi, i: (bi, i, C_SG_U // W_BR)),
            pl.BlockSpec((1, tm, W_BR), lambda bi, i: (bi, i, C_SG_V // W_BR)),
            pl.BlockSpec((1, W_BR), lambda bi, i: (0, 0)),
            pl.BlockSpec((N_HEADS, SG_CHUNK, SG_CHUNK), lambda bi, i: (0, 0, 0)),
            pl.BlockSpec((SG_CHUNK, N_HEADS), lambda bi, i: (0, 0)),
        ],
        out_specs=pl.BlockSpec((1, tm, W_BR), lambda bi, i: (bi, i, 0)),
        out_shape=SDS((b, t, W_BR), BF16),
        compiler_params=_cp("parallel", "parallel"),
        name="sgu",
    )(pm, pm, vnorm, ws, bs_t)


def _merge_kernel(h_ref, y0_ref, y1_ref, y2_ref, y3_ref, wg_ref, bg_ref, wb_ref, o_ref):
    h = h_ref[0]
    acc = None
    for i, y_ref in enumerate((y0_ref, y1_ref, y2_ref, y3_ref)):
        gate = jax.nn.sigmoid(_dot(h, wg_ref[0, i]) + bg_ref[0, i])
        term = gate * _dot(y_ref[0].astype(BF16), wb_ref[0, i])
        acc = term if acc is None else acc + term
    o_ref[0] = acc.astype(o_ref.dtype)


def _merge(h, ys, wg, bg, wb, l):
    bx, t, d = h.shape
    tm, tn = _tile(t, 1024), _tile(d, 512, LANES)
    y_spec = pl.BlockSpec((1, tm, W_BR), lambda b, i, j: (b, i, 0))
    return pl.pallas_call(
        _merge_kernel,
        grid=(bx, t // tm, d // tn),
        in_specs=[
            pl.BlockSpec((1, tm, d), lambda b, i, j: (b, i, 0)),
            y_spec, y_spec, y_spec, y_spec,
            pl.BlockSpec((1, N_BRANCH, d, tn), lambda b, i, j: (l, 0, 0, j)),
            pl.BlockSpec((1, N_BRANCH, 1, tn), lambda b, i, j: (l, 0, 0, j)),
            pl.BlockSpec((1, N_BRANCH, W_BR, tn), lambda b, i, j: (l, 0, 0, j)),
        ],
        out_specs=pl.BlockSpec((1, tm, tn), lambda b, i, j: (b, i, j)),
        out_shape=SDS((bx, t, d), BF16),
        compiler_params=_cp("parallel", "parallel", "arbitrary"),
        name="merge",
    )(h, *ys, wg, bg, wb)


def _resid_kernel(a_ref, w_ref, x_ref, g_ref, o_ref, *, row):
    o_ref[0] = x_ref[0] + _mod_row(g_ref, row) * _dot(a_ref[0], w_ref[0])


def _resid_proj(a, w, l, x, mod, gate_blk, row):
    bx, t, k = a.shape
    d = w.shape[2]
    if k * d * 2 <= RESIDENT_WEIGHT_BYTES:
        tm, tn = _tile(t, 512), d
    else:
        tm, tn = _tile(t, 1024), _tile(d, 512, LANES)
    nj = d // tn
    return pl.pallas_call(
        functools.partial(_resid_kernel, row=row),
        grid=(bx, t // tm, nj),
        in_specs=[
            pl.BlockSpec((1, tm, k), lambda b, i, j: (b, i, 0)),
            pl.BlockSpec((1, k, tn), lambda b, i, j: (l, 0, j)),
            pl.BlockSpec((1, tm, tn), lambda b, i, j: (b, i, j)),
            pl.BlockSpec((MOD_ROWS, tn), lambda b, i, j: (0, gate_blk * nj + j)),
        ],
        out_specs=pl.BlockSpec((1, tm, tn), lambda b, i, j: (b, i, j)),
        out_shape=SDS((bx, t, d), F32),
        compiler_params=_cp("parallel", "parallel", "arbitrary"),
        name="resid_proj",
    )(a, w, x, mod)


def _ffn_up_kernel(x_ref, nw_ref, sh_ref, sc_ref, w1_ref, w3_ref, o_ref, hs_ref, *, row):
    @pl.when(pl.program_id(2) == 0)
    def _():
        hs_ref[...] = _norm_mod(x_ref[0], nw_ref, sh_ref, sc_ref, row)

    h = hs_ref[...]
    o_ref[0] = (_silu(_dot(h, w1_ref[0])) * _dot(h, w3_ref[0])).astype(o_ref.dtype)


def _ffn_up(x, nw, mod, w1, w3, l, row):
    bx, t, d = x.shape
    f = w1.shape[2]
    tm, tn = _tile(t, 1024), _tile(f, 512, LANES)
    return pl.pallas_call(
        functools.partial(_ffn_up_kernel, row=row),
        grid=(bx, t // tm, f // tn),
        in_specs=[
            pl.BlockSpec((1, tm, d), lambda b, i, j: (b, i, 0)),
            pl.BlockSpec((1, d), lambda b, i, j: (0, 0)),
            pl.BlockSpec((MOD_ROWS, d), lambda b, i, j: (0, 3)),
            pl.BlockSpec((MOD_ROWS, d), lambda b, i, j: (0, 4)),
            pl.BlockSpec((1, d, tn), lambda b, i, j: (l, 0, j)),
            pl.BlockSpec((1, d, tn), lambda b, i, j: (l, 0, j)),
        ],
        out_specs=pl.BlockSpec((1, tm, tn), lambda b, i, j: (b, i, j)),
        out_shape=SDS((bx, t, f), BF16),
        scratch_shapes=[pltpu.VMEM((tm, d), BF16)],
        compiler_params=_cp("parallel", "parallel", "arbitrary"),
        name="ffn_up",
    )(x, nw, mod, mod, w1, w3)


def _mixers(pm, pab, pm_c, pab_c, p, need_ctx):
    b = pm.shape[0]
    y_na = _neighbourhood_attention(pm, pm_c, p["qn"], p["kn"], p["rpb"])
    zero = jnp.zeros((b, 2 * N_HEADS, HEAD_DIM, HEAD_DIM), F32)
    of_c, ob_c, s_ctx = _deltanet(_dn_conv(pm_c, p["conv"]), pab_c, p["a_log"], p["dt_bias"], zero)
    of, ob, _ = _deltanet(_dn_conv(pm, p["conv"]), pab, p["a_log"], p["dt_bias"], s_ctx)
    ys = (y_na, _dn_out(of, ob, pm, p["onorm"]), _fourier(pm), _spatial_gating(pm, p["sg_w"], p["sg_bt"], p["sg_vn"]))
    if not need_ctx:
        return ys, None
    ys_c = (_dense_attention(pm_c, p["qn"], p["kn"]), _dn_out(of_c, ob_c, pm_c, p["onorm"]), _fourier(pm_c),
            _spatial_gating(pm_c, p["sg_w"], p["sg_bt"], p["sg_vn"]))
    return ys, ys_c


def kernel(x, c, ctx, c_ctx, w_ada, b_ada, norm1_w, norm2_w, w_in, na_qnorm, na_knorm, na_rpb, dn_conv, dn_a_log, dn_dt_bias, dn_onorm, sg_w, sg_b, sg_vnorm, w_gate, b_gate, w_branch, w_out, ffn_w1, ffn_w3, ffn_w2):
    bsz, seq, d = x.shape
    ctx_len = ctx.shape[1]
    depth = w_ada.shape[0]
    assert bsz + 1 <= MOD_ROWS and d % LANES == 0
    ctx_row = bsz

    cc = jnp.zeros((MOD_ROWS, d), F32).at[:bsz].set(c).at[ctx_row].set(c_ctx)
    mod_all = _ada_mod(cc, w_ada, b_ada)
    xc = ctx.reshape(1, bsz * ctx_len, d)

    w_in_b = w_in.astype(BF16)
    w_main = jnp.concatenate([w_in_b[..., :C_AB_SRC], w_in_b[..., C_AB_SRC + N_AB:]], axis=-1)
    w_ab = jnp.pad(w_in_b[..., C_AB_SRC:C_AB_SRC + N_AB], ((0, 0), (0, 0), (0, LANES - N_AB)))
    wg, wb, wo = w_gate.astype(BF16), w_branch.astype(BF16), w_out.astype(BF16)
    w1, w3, w2 = ffn_w1.astype(BF16), ffn_w3.astype(BF16), ffn_w2.astype(BF16)
    bg = b_gate.reshape(depth, N_BRANCH, 1, d)

    for l in range(depth):
        need_ctx = l < depth - 1
        mod = mod_all[l]
        p = dict(qn=na_qnorm[l][None], kn=na_knorm[l][None], rpb=na_rpb[l], conv=dn_conv[l], a_log=dn_a_log[l],
                 dt_bias=dn_dt_bias[l], onorm=dn_onorm[l][None], sg_w=sg_w[l].astype(BF16), sg_bt=sg_b[l].T,
                 sg_vn=sg_vnorm[l][None])
        n1w, n2w = norm1_w[l][None], norm2_w[l][None]

        pm, pab, h = _in_proj(x, n1w, mod, w_main, w_ab, l, None)
        pm_c, pab_c, h_c = _in_proj(xc, n1w, mod, w_main, w_ab, l, ctx_row)
        pm_c = pm_c.reshape(bsz, ctx_len, N_MAIN)
        pab_c = pab_c.reshape(bsz, ctx_len, LANES)
        ys, ys_c = _mixers(pm, pab, pm_c, pab_c, p, need_ctx)

        merged = _merge(h, ys, wg, bg, wb, l)
        x = _resid_proj(merged, wo, l, x, mod, 2, None)
        x = _resid_proj(_ffn_up(x, n2w, mod, w1, w3, l, None), w2, l, x, mod, 5, None)
        if need_ctx:
            ys_c = tuple(y.reshape(1, bsz * ctx_len, W_BR) for y in ys_c)
            merged_c = _merge(h_c, ys_c, wg, bg, wb, l)
            xc = _resid_proj(merged_c, wo, l, xc, mod, 2, ctx_row)
            xc = _resid_proj(_ffn_up(xc, n2w, mod, w1, w3, l, ctx_row), w2, l, xc, mod, 5, ctx_row)
    return x
```

```python
import functools

import numpy as np
import jax
import jax.numpy as jnp
from jax import lax
from jax.experimental import pallas as pl
from jax.experimental.pallas import tpu as pltpu

F32 = jnp.float32
BF16 = jnp.bfloat16
SDS = jax.ShapeDtypeStruct

LANES = 128
HEAD_DIM = 128
N_HEADS = 4
W_BR = N_HEADS * HEAD_DIM
N_BRANCH = 4
GRID_W = 64
NA_WIN_R = 8
NA_WIN_C = 16
NA_ROWS_PER_STEP = 16
DN_CHUNK = 64
DN_BATCH_PER_STEP = 4
DN_INV_BASE = 8
DN_CONV = 5
DN_CONV_PAD = 8
DN_CONV_WIDE_TILE_ROWS = 1024
SG_CHUNK = 128
FT_UNROLL = 8
FT_PAD = 4
N_MOD = 6
EPS = 1e-6
NEG_INF = -1e30
MOD_ROWS = 8
N_AB = 4 * N_HEADS
VMEM_LIMIT_BYTES = 56 * 1024 * 1024
RESIDENT_WEIGHT_BYTES = 8 * 1024 * 1024
FFN_UP_ROWS = 2048

C_NA_Q, C_NA_K, C_NA_V = 0, W_BR, 2 * W_BR
C_DN_QKV = 3 * W_BR
C_DN_Z = 6 * W_BR
C_FT = 7 * W_BR
C_SG_U = 8 * W_BR
C_SG_V = 9 * W_BR
N_MAIN = 10 * W_BR
C_AB_SRC = 7 * W_BR


def _cp(*sem):
    return pltpu.CompilerParams(dimension_semantics=sem, vmem_limit_bytes=VMEM_LIMIT_BYTES)


def _tile(n, pref, mult=8):
    if n <= pref:
        return n
    for t in range(pref - pref % mult, 0, -mult):
        if n % t == 0:
            return t
    return n


def _dot(a, b):
    return jnp.dot(a, b, preferred_element_type=F32)


def _dot_nt(a, b):
    return lax.dot_general(a, b, (((1,), (1,)), ((), ())), preferred_element_type=F32)


def _dot_tn(a, b):
    return lax.dot_general(a, b, (((0,), (0,)), ((), ())), preferred_element_type=F32)


def _silu(x):
    return x * jax.nn.sigmoid(x)


def _rms(x, w):
    return x * lax.rsqrt(jnp.mean(x * x, axis=-1, keepdims=True) + EPS) * w


def _mod_row(ref, row):
    b = pl.program_id(0) if row is None else row
    return ref[pl.ds(b, 1), :]


def _norm_mod(x, nw_ref, sh_ref, sc_ref, row):
    return (_rms(x, nw_ref[...]) * (1.0 + _mod_row(sc_ref, row)) + _mod_row(sh_ref, row)).astype(BF16)


def _ada_kernel(c_ref, w_ref, b_ref, o_ref):
    s = _silu(c_ref[...]).astype(BF16)
    o_ref[0] = _dot(s, w_ref[0].astype(BF16)) + b_ref[0]


def _ada_mod(cc, w_ada, b_ada):
    depth, d, n = w_ada.shape
    tn = _tile(n, 1024, LANES)
    return pl.pallas_call(
        _ada_kernel,
        grid=(depth, n // tn),
        in_specs=[
            pl.BlockSpec((MOD_ROWS, d), lambda l, j: (0, 0)),
            pl.BlockSpec((1, d, tn), lambda l, j: (l, 0, j)),
            pl.BlockSpec((1, 1, tn), lambda l, j: (l, 0, j)),
        ],
        out_specs=pl.BlockSpec((1, MOD_ROWS, tn), lambda l, j: (l, 0, j)),
        out_shape=SDS((depth, MOD_ROWS, n), F32),
        compiler_params=_cp("parallel", "parallel"),
        name="ada_mod",
    )(cc, w_ada, b_ada.reshape(depth, 1, n))


def _in_proj_kernel(x_ref, nw_ref, sh_ref, sc_ref, w_ref, wab_ref, o_ref, oab_ref, h_ref, hs_ref, *, row):
    @pl.when(pl.program_id(2) == 0)
    def _():
        h = _norm_mod(x_ref[0], nw_ref, sh_ref, sc_ref, row)
        hs_ref[...] = h
        h_ref[0] = h
        oab_ref[0] = _dot(h, wab_ref[0])

    o_ref[0] = _dot(hs_ref[...], w_ref[0]).astype(o_ref.dtype)


def _in_proj(x, nw, mod, w_main, w_ab, l, row):
    bx, t, d = x.shape
    n = w_main.shape[2]
    tm, tn = _tile(t, 1024), _tile(n, 1024, LANES)
    return pl.pallas_call(
        functools.partial(_in_proj_kernel, row=row),
        grid=(bx, t // tm, n // tn),
        in_specs=[
            pl.BlockSpec((1, tm, d), lambda b, i, j: (b, i, 0)),
            pl.BlockSpec((1, d), lambda b, i, j: (0, 0)),
            pl.BlockSpec((MOD_ROWS, d), lambda b, i, j: (0, 0)),
            pl.BlockSpec((MOD_ROWS, d), lambda b, i, j: (0, 1)),
            pl.BlockSpec((1, d, tn), lambda b, i, j: (l, 0, j)),
            pl.BlockSpec((1, d, LANES), lambda b, i, j: (l, 0, 0)),
        ],
        out_specs=[
            pl.BlockSpec((1, tm, tn), lambda b, i, j: (b, i, j)),
            pl.BlockSpec((1, tm, LANES), lambda b, i, j: (b, i, 0)),
            pl.BlockSpec((1, tm, d), lambda b, i, j: (b, i, 0)),
        ],
        out_shape=[SDS((bx, t, n), BF16), SDS((bx, t, LANES), F32), SDS((bx, t, d), BF16)],
        scratch_shapes=[pltpu.VMEM((tm, d), BF16)],
        compiler_params=_cp("parallel", "parallel", "arbitrary"),
        name="in_proj",
    )(x, nw, mod, mod, w_main, w_ab)


def _na_kernel(q_ref, k_ref, v_ref, kc_ref, vc_ref, qn_ref, kn_ref, bias_ref, o_ref,
               qs, ks, vs, kcs, vcs, *, rows):
    averaging = jnp.full((HEAD_DIM, HEAD_DIM), 1.0 / HEAD_DIM, BF16)

    def rms(x, w):
        return x * lax.rsqrt(_dot((x * x).astype(BF16), averaging) + EPS) * w

    qs[...] = (rms(q_ref[0].astype(F32), qn_ref[...]) * HEAD_DIM ** -0.5).astype(BF16)
    ks[...] = rms(k_ref[0].astype(F32), kn_ref[...]).astype(BF16)
    vs[...] = v_ref[0].astype(BF16)
    kcs[...] = rms(kc_ref[0].astype(F32), kn_ref[...]).astype(BF16)
    vcs[...] = vc_ref[0].astype(BF16)
    band = NA_WIN_R * GRID_W
    nr = NA_ROWS_PER_STEP if rows % NA_ROWS_PER_STEP == 0 else NA_WIN_R

    def body(g, carry):
        rr = [g * nr + i for i in range(nr)]
        start = [jnp.clip(r - NA_WIN_R // 2, 0, rows - NA_WIN_R) for r in rr]
        r0 = [pl.multiple_of(r * GRID_W, GRID_W) for r in rr]
        k0 = [pl.multiple_of(s * GRID_W, GRID_W) for s in start]
        q = [qs[pl.ds(x, GRID_W), :] for x in r0]
        kb = [ks[pl.ds(x, band), :] for x in k0]
        vb = [vs[pl.ds(x, band), :] for x in k0]
        bias = [bias_ref[0, r - s] for r, s in zip(rr, start)]
        kc, vc = kcs[...], vcs[...]
        s_loc = [_dot_nt(q[i], kb[i]) + bias[i] for i in range(nr)]
        s_ctx = [_dot_nt(q[i], kc) for i in range(nr)]
        m = [jnp.maximum(jnp.max(s_loc[i], axis=-1, keepdims=True), jnp.max(s_ctx[i], axis=-1, keepdims=True))
             for i in range(nr)]
        p = [jnp.exp(s_loc[i] - m[i]) for i in range(nr)]
        pc = [jnp.exp(s_ctx[i] - m[i]) for i in range(nr)]
        denom = [jnp.sum(p[i], axis=-1, keepdims=True) + jnp.sum(pc[i], axis=-1, keepdims=True) for i in range(nr)]
        o = [_dot(p[i].astype(BF16), vb[i]) + _dot(pc[i].astype(BF16), vc) for i in range(nr)]
        for i in range(nr):
            o_ref[0, pl.ds(r0[i], GRID_W), :] = (o[i] / denom[i]).astype(o_ref.dtype)
        return carry

    lax.fori_loop(0, rows // nr, body, 0)


def _na_bias_table(rpb):
    col = np.arange(GRID_W)
    dc = np.clip(col[None, :] - col[:, None], 1 - NA_WIN_C, NA_WIN_C - 1) + NA_WIN_C - 1
    col_start = np.clip(col - NA_WIN_C // 2, 0, GRID_W - NA_WIN_C)
    in_win = (col[None, :] >= col_start[:, None]) & (col[None, :] < col_start[:, None] + NA_WIN_C)
    rows = jnp.stack([rpb[:, NA_WIN_R - 1 - typ:2 * NA_WIN_R - 1 - typ, :] for typ in range(NA_WIN_R)], axis=1)
    onehot = (dc[None] == np.arange(2 * NA_WIN_C - 1)[:, None, None]).astype(np.float32)
    b = jnp.einsum('htjc,cqk->htqjk', rows.astype(F32), onehot, precision=lax.Precision.HIGHEST)
    b = jnp.where(in_win[None, None, :, None, :], b, NEG_INF)
    return b.reshape(rpb.shape[0], NA_WIN_R, GRID_W, NA_WIN_R * GRID_W)


def _neighbourhood_attention(pm, pm_c, qn, kn, rpb):
    b, t, _ = pm.shape
    ctx_len = pm_c.shape[1]
    rows = t // GRID_W
    assert rows >= NA_WIN_R and rows * GRID_W == t and rows % NA_WIN_R == 0
    band = NA_WIN_R * GRID_W
    hb = lambda off: (lambda bi, h: (bi, 0, off // HEAD_DIM + h))
    return pl.pallas_call(
        functools.partial(_na_kernel, rows=rows),
        grid=(b, N_HEADS),
        in_specs=[
            pl.BlockSpec((1, t, HEAD_DIM), hb(C_NA_Q)),
            pl.BlockSpec((1, t, HEAD_DIM), hb(C_NA_K)),
            pl.BlockSpec((1, t, HEAD_DIM), hb(C_NA_V)),
            pl.BlockSpec((1, ctx_len, HEAD_DIM), hb(C_NA_K)),
            pl.BlockSpec((1, ctx_len, HEAD_DIM), hb(C_NA_V)),
            pl.BlockSpec((1, HEAD_DIM), lambda bi, h: (0, 0)),
            pl.BlockSpec((1, HEAD_DIM), lambda bi, h: (0, 0)),
            pl.BlockSpec((1, NA_WIN_R, GRID_W, band), lambda bi, h: (h, 0, 0, 0)),
        ],
        out_specs=pl.BlockSpec((1, t, HEAD_DIM), lambda bi, h: (bi, 0, h)),
        out_shape=SDS((b, t, W_BR), BF16),
        scratch_shapes=[pltpu.VMEM((t, HEAD_DIM), BF16)] * 3 + [pltpu.VMEM((ctx_len, HEAD_DIM), BF16)] * 2,
        compiler_params=_cp("parallel", "parallel"),
        name="natten",
    )(pm, pm, pm, pm_c, pm_c, qn, kn, _na_bias_table(rpb))


def _ctx_attn_kernel(q_ref, k_ref, v_ref, qn_ref, kn_ref, o_ref):
    q = _rms(q_ref[0].astype(F32), qn_ref[...]).astype(BF16)
    k = _rms(k_ref[0].astype(F32), kn_ref[...]).astype(BF16)
    s = _dot_nt(q, k) * HEAD_DIM ** -0.5
    p = jnp.exp(s - jnp.max(s, axis=-1, keepdims=True))
    o = _dot(p.astype(BF16), v_ref[0].astype(BF16)) / jnp.sum(p, axis=-1, keepdims=True)
    o_ref[0] = o.astype(o_ref.dtype)


def _dense_attention(pm_c, qn, kn):
    b, ctx_len, _ = pm_c.shape
    hb = lambda off: (lambda bi, h: (bi, 0, off // HEAD_DIM + h))
    return pl.pallas_call(
        _ctx_attn_kernel,
        grid=(b, N_HEADS),
        in_specs=[
            pl.BlockSpec((1, ctx_len, HEAD_DIM), hb(C_NA_Q)),
            pl.BlockSpec((1, ctx_len, HEAD_DIM), hb(C_NA_K)),
            pl.BlockSpec((1, ctx_len, HEAD_DIM), hb(C_NA_V)),
            pl.BlockSpec((1, HEAD_DIM), lambda bi, h: (0, 0)),
            pl.BlockSpec((1, HEAD_DIM), lambda bi, h: (0, 0)),
        ],
        out_specs=pl.BlockSpec((1, ctx_len, HEAD_DIM), lambda bi, h: (bi, 0, h)),
        out_shape=SDS((b, ctx_len, W_BR), BF16),
        compiler_params=_cp("parallel", "parallel"),
        name="ctx_attn",
    )(pm_c, pm_c, pm_c, qn, kn)


def _dn_conv_kernel(x_ref, w_ref, o_ref, xp_ref, *, n_norm_tiles):
    x = x_ref[0].astype(F32)
    t = x.shape[0]
    border = jnp.zeros((DN_CONV_PAD, x.shape[1]), F32)
    xp_ref[0:DN_CONV_PAD, :] = border
    xp_ref[DN_CONV_PAD + t:, :] = border
    xp_ref[DN_CONV_PAD:DN_CONV_PAD + t, :] = x
    half = DN_CONV // 2
    acc = x * w_ref[half:half + 1, :]
    for kk in range(DN_CONV):
        d = kk - half
        if d != 0:
            acc = acc + xp_ref[DN_CONV_PAD + d:DN_CONV_PAD + d + t, :] * w_ref[kk:kk + 1, :]
    y = _silu(acc)
    normed = pl.program_id(1) < n_norm_tiles
    for h in range(y.shape[1] // HEAD_DIM):
        sl = slice(h * HEAD_DIM, (h + 1) * HEAD_DIM)
        yh = y[:, sl]
        unit = yh * lax.rsqrt(jnp.sum(yh * yh, axis=-1, keepdims=True) + EPS)
        o_ref[0, :, sl] = jnp.where(normed, unit, yh)


def _dn_conv(pm, conv_w):
    b, t, _ = pm.shape
    tw = W_BR if t <= DN_CONV_WIDE_TILE_ROWS else HEAD_DIM
    return pl.pallas_call(
        functools.partial(_dn_conv_kernel, n_norm_tiles=2 * W_BR // tw),
        grid=(b, 3 * W_BR // tw),
        in_specs=[
            pl.BlockSpec((1, t, tw), lambda bi, j: (bi, 0, C_DN_QKV // tw + j)),
            pl.BlockSpec((DN_CONV, tw), lambda bi, j: (0, j)),
        ],
        out_specs=pl.BlockSpec((1, t, tw), lambda bi, j: (bi, 0, j)),
        out_shape=SDS((b, t, 3 * W_BR), F32),
        scratch_shapes=[pltpu.VMEM((t + 2 * DN_CONV_PAD, tw), F32)],
        compiler_params=_cp("parallel", "parallel"),
        name="dn_conv",
    )(pm, conv_w)


def _softplus(x):
    return jnp.maximum(x, 0.0) + jnp.log1p(jnp.exp(-jnp.abs(x)))


def _block_masks(n):
    ii = lax.broadcasted_iota(jnp.int32, (n, n), 0)
    jj = lax.broadcasted_iota(jnp.int32, (n, n), 1)
    same = lambda s: (ii // s) == (jj // s)
    masks, s = [same(DN_INV_BASE)], DN_INV_BASE
    while s < n:
        masks.append(same(2 * s) & jnp.logical_not(same(s)))
        s *= 2
    return masks


def _unit_triangular_inverses(a_list, eye, masks):
    mm = lambda x, y: _dot(x.astype(BF16), y.astype(BF16))
    n = len(a_list)
    pw = [jnp.where(masks[0], a, 0.0) for a in a_list]
    inv = [eye - p for p in pw]
    s = 2
    while s < DN_INV_BASE:
        pw = [mm(p, p) for p in pw]
        inv = [inv[i] + mm(inv[i], pw[i]) for i in range(n)]
        s *= 2
    for m in masks[1:]:
        t = [mm(jnp.where(m, a_list[i], 0.0), inv[i]) for i in range(n)]
        inv = [inv[i] - mm(inv[i], t[i]) for i in range(n)]
    return inv


def _dn_kernel(qf_ref, kf_ref, vf_ref, qb_ref, kb_ref, vb_ref, abf_ref, abb_ref, abtf_ref, abtb_ref,
               prow_ref, pcol_ref, s0_ref, of_ref, ob_ref, sfin_ref, s_ref):
    c = pl.program_id(1)

    @pl.when(c == 0)
    def _():
        s_ref[...] = s0_ref[...].reshape(s_ref.shape)

    cs = DN_CHUNK
    ii = lax.broadcasted_iota(jnp.int32, (cs, cs), 0)
    jj = lax.broadcasted_iota(jnp.int32, (cs, cs), 1)
    eye = (ii == jj).astype(F32)
    blk_masks = _block_masks(cs)
    scale = HEAD_DIM ** -0.5
    hi = lax.Precision.HIGHEST

    nb = qf_ref.shape[0]
    nst = 2 * N_HEADS
    chains = [(bb, d, h) for bb in range(nb) for d in range(2) for h in range(N_HEADS)]
    nch = len(chains)
    incl = [(jj <= ii), (jj >= ii)]
    strict = [(jj < ii), (jj > ii)]
    gc_col, gc_row, beta_all = {}, {}, {}
    for bb in range(nb):
        for d in range(2):
            tri = incl[d].astype(F32)
            ab = (abf_ref if d == 0 else abb_ref)[bb]
            abt = (abtf_ref if d == 0 else abtb_ref)[bb, 0]
            g_col = -jnp.exp(prow_ref[0:1, :]) * _softplus(ab + prow_ref[1:2, :])
            g_row = -jnp.exp(pcol_ref[:, 0:1]) * _softplus(abt + pcol_ref[:, 1:2])
            gc_col[bb, d] = jnp.dot(tri, g_col, preferred_element_type=F32, precision=hi)
            gc_row[bb, d] = lax.dot_general(g_row, tri, (((1,), (1,)), ((), ())),
                                            preferred_element_type=F32, precision=hi)
            beta_all[bb, d] = jax.nn.sigmoid(ab)
    qkv_refs = ((qf_ref, kf_ref, vf_ref), (qb_ref, kb_ref, vb_ref))
    sl = [slice(h * HEAD_DIM, (h + 1) * HEAD_DIM) for h in range(N_HEADS)]
    col = [d * N_HEADS + h for bb, d, h in chains]
    q = [qkv_refs[d][0][bb, :, sl[h]] * scale for bb, d, h in chains]
    k = [qkv_refs[d][1][bb, :, sl[h]] for bb, d, h in chains]
    v = [qkv_refs[d][2][bb, :, sl[h]] for bb, d, h in chains]
    s_old = [s_ref[bb * nst + col[j]] for j, (bb, d, h) in enumerate(chains)]
    gcc = [gc_col[bb, d][:, col[j]:col[j] + 1] for j, (bb, d, h) in enumerate(chains)]
    gcr = [gc_row[bb, d][col[j]:col[j] + 1, :] for j, (bb, d, h) in enumerate(chains)]
    beta = [beta_all[bb, d][:, nst + col[j]:nst + col[j] + 1] for j, (bb, d, h) in enumerate(chains)]
    decay = [jnp.exp(jnp.where(incl[d], gcc[j] - gcr[j], -jnp.inf)) for j, (bb, d, h) in enumerate(chains)]
    eg = [jnp.exp(x) for x in gcc]
    g_last = [gcc[j][cs - 1:cs, :] if d == 0 else gcc[j][0:1, :] for j, (bb, d, h) in enumerate(chains)]
    kbf = [x.astype(BF16) for x in k]
    kbeta = [k[j] * beta[j] for j in range(nch)]
    kq = [_dot_nt(jnp.concatenate([kbeta[j], q[j]], axis=0).astype(BF16), kbf[j]) for j in range(nch)]
    a_mat = [jnp.where(strict[d], kq[j][:cs] * decay[j], 0.0) for j, (bb, d, h) in enumerate(chains)]
    qk = [jnp.where(incl[d], kq[j][cs:] * decay[j], 0.0) for j, (bb, d, h) in enumerate(chains)]
    inv = _unit_triangular_inverses(a_mat, eye, blk_masks)
    rhs = [jnp.concatenate([v[j] * beta[j], kbeta[j] * eg[j]], axis=1).astype(BF16) for j in range(nch)]
    uw = [_dot(inv[j].astype(BF16), rhs[j]) for j in range(nch)]
    ws_qs = [_dot(jnp.concatenate([uw[j][:, HEAD_DIM:], q[j] * eg[j]], axis=0).astype(BF16), s_old[j].astype(BF16))
             for j in range(nch)]
    v_new = [uw[j][:, :HEAD_DIM] - ws_qs[j][:cs] for j in range(nch)]
    vnb = [x.astype(BF16) for x in v_new]
    o = [ws_qs[j][cs:] + _dot(qk[j].astype(BF16), vnb[j]) for j in range(nch)]
    kd = [(k[j] * jnp.exp(g_last[j] - gcc[j])).astype(BF16) for j in range(nch)]
    s_new = [s_old[j] * jnp.exp(g_last[j]) + _dot_tn(kd[j], vnb[j]) for j in range(nch)]
    for j, (bb, d, h) in enumerate(chains):
        s_ref[bb * nst + col[j]] = s_new[j]
        (of_ref if d == 0 else ob_ref)[bb, :, sl[h]] = o[j]

    @pl.when(c == pl.num_programs(1) - 1)
    def _():
        sfin_ref[...] = s_ref[...].reshape(sfin_ref.shape)


def _deltanet(act, pab, a_log, dt_bias, s0):
    b, t, _ = act.shape
    n = t // DN_CHUNK
    assert n * DN_CHUNK == t
    abt = jnp.swapaxes(pab[..., :N_AB].reshape(b, n, DN_CHUNK, N_AB), 2, 3)
    al, dt = a_log.reshape(-1).astype(F32), dt_bias.reshape(-1).astype(F32)
    prow = jnp.zeros((2, LANES), F32).at[0, :al.size].set(al).at[1, :dt.size].set(dt)
    pcol = jnp.zeros((N_AB, 2), F32).at[:al.size, 0].set(al).at[:dt.size, 1].set(dt)
    nb = DN_BATCH_PER_STEP if b % DN_BATCH_PER_STEP == 0 else 1
    fwd = lambda part: (lambda bi, c: (bi, c, part))
    bwd = lambda part: (lambda bi, c: (bi, n - 1 - c, part))
    qkv_spec = lambda imap: pl.BlockSpec((nb, DN_CHUNK, W_BR), imap)
    n_state = 2 * N_HEADS
    state_spec = pl.BlockSpec((nb, n_state, HEAD_DIM, HEAD_DIM), lambda bi, c: (bi, 0, 0, 0))
    return pl.pallas_call(
        _dn_kernel,
        grid=(b // nb, n),
        in_specs=[
            qkv_spec(fwd(0)), qkv_spec(fwd(1)), qkv_spec(fwd(2)),
            qkv_spec(bwd(0)), qkv_spec(bwd(1)), qkv_spec(bwd(2)),
            pl.BlockSpec((nb, DN_CHUNK, LANES), fwd(0)),
            pl.BlockSpec((nb, DN_CHUNK, LANES), bwd(0)),
            pl.BlockSpec((nb, 1, N_AB, DN_CHUNK), lambda bi, c: (bi, c, 0, 0)),
            pl.BlockSpec((nb, 1, N_AB, DN_CHUNK), lambda bi, c: (bi, n - 1 - c, 0, 0)),
            pl.BlockSpec((2, LANES), lambda bi, c: (0, 0)),
            pl.BlockSpec((N_AB, 2), lambda bi, c: (0, 0)),
            state_spec,
        ],
        out_specs=[
            pl.BlockSpec((nb, DN_CHUNK, W_BR), fwd(0)),
            pl.BlockSpec((nb, DN_CHUNK, W_BR), bwd(0)),
            state_spec,
        ],
        out_shape=[SDS((b, t, W_BR), F32), SDS((b, t, W_BR), F32),
                   SDS((b, n_state, HEAD_DIM, HEAD_DIM), F32)],
        scratch_shapes=[pltpu.VMEM((nb * n_state, HEAD_DIM, HEAD_DIM), F32)],
        compiler_params=_cp("parallel", "arbitrary"),
        name="deltanet",
    )(act, act, act, act, act, act, pab, pab, abt, abt, prow, pcol, s0)


def _dn_out_kernel(of_ref, ob_ref, z_ref, w_ref, y_ref):
    o = of_ref[0] + ob_ref[0]
    z = z_ref[0].astype(F32)
    for h in range(N_HEADS):
        sl = slice(h * HEAD_DIM, (h + 1) * HEAD_DIM)
        y_ref[0, :, sl] = (_rms(o[:, sl], w_ref[...]) * _silu(z[:, sl])).astype(y_ref.dtype)


def _dn_out(o_f, o_b, pm, onorm):
    b, t, _ = o_f.shape
    tm = _tile(t, 1024)
    return pl.pallas_call(
        _dn_out_kernel,
        grid=(b, t // tm),
        in_specs=[
            pl.BlockSpec((1, tm, W_BR), lambda bi, i: (bi, i, 0)),
            pl.BlockSpec((1, tm, W_BR), lambda bi, i: (bi, i, 0)),
            pl.BlockSpec((1, tm, W_BR), lambda bi, i: (bi, i, C_DN_Z // W_BR)),
            pl.BlockSpec((1, HEAD_DIM), lambda bi, i: (0, 0)),
        ],
        out_specs=pl.BlockSpec((1, tm, W_BR), lambda bi, i: (bi, i, 0)),
        out_shape=SDS((b, t, W_BR), BF16),
        compiler_params=_cp("parallel", "parallel"),
        name="dn_out",
    )(o_f, o_b, pm, onorm)


def _fourier_kernel(x_ref, m1_ref, m2_ref, m3_ref, twc_ref, tws_ref, o_ref, xf_ref, bre_ref, bim_ref, zo_ref,
                    *, n1, n2, scale):
    m1, m2, m3 = m1_ref[...].astype(BF16), m2_ref[...].astype(BF16), m3_ref[...].astype(BF16)
    nu = FT_UNROLL
    p2, p1 = n2 + FT_PAD, n1 + FT_PAD
    for t1 in range(n1):
        xf_ref[t1 * p2:t1 * p2 + n2, :] = x_ref[0, t1 * n2:(t1 + 1) * n2, :].astype(F32)

    def stage1(i, carry):
        t2 = [i * nu + u for u in range(nu)]
        xs = [xf_ref[pl.ds(t, n1, stride=p2), :].astype(BF16) for t in t2]
        a = [_dot(m1, x) for x in xs]
        cw = [twc_ref[t] for t in t2]
        sw = [tws_ref[t] for t in t2]
        bre = [a[u][:n1] * cw[u] - a[u][n1:] * sw[u] for u in range(nu)]
        bim = [-(a[u][n1:] * cw[u] + a[u][:n1] * sw[u]) for u in range(nu)]
        for u in range(nu):
            bre_ref[pl.ds(t2[u], n1, stride=p2), :] = bre[u]
            bim_ref[pl.ds(t2[u], n1, stride=p2), :] = bim[u]
        return carry

    lax.fori_loop(0, n2 // nu, stage1, 0)

    def stage2(i, carry):
        f1 = [i * nu + u for u in range(nu)]
        r0 = [pl.multiple_of(f * p2, FT_PAD) for f in f1]
        b = [jnp.concatenate([bre_ref[pl.ds(r, n2), :], bim_ref[pl.ds(r, n2), :]], axis=0).astype(BF16) for r in r0]
        y = [_dot(m2, x) for x in b]
        z = [_dot(jnp.concatenate([yy[:n2], yy[n2:]], axis=1).astype(BF16), m3) for yy in y]
        for u in range(nu):
            zo_ref[pl.ds(f1[u], n2, stride=p1), :] = z[u] * scale
        return carry

    lax.fori_loop(0, n1 // nu, stage2, 0)
    for f2 in range(n2):
        o_ref[0, f2 * n1:(f2 + 1) * n1, :] = zo_ref[f2 * p1:f2 * p1 + n1, :]


def _dft_mats(n):
    ang = 2.0 * np.pi * ((np.arange(n)[:, None] * np.arange(n)[None, :]) % n) / n
    return np.cos(ang), np.sin(ang)


def _fourier(pm):
    b, t, _ = pm.shape
    n1 = 1 << ((t.bit_length() - 1 + 1) // 2)
    n2 = t // n1
    assert n1 * n2 == t and n1 % FT_UNROLL == 0 and n2 % FT_UNROLL == 0
    c1, s1 = _dft_mats(n1)
    c2, s2 = _dft_mats(n2)
    cc, sc = _dft_mats(HEAD_DIM)
    m1 = jnp.asarray(np.concatenate([c1, s1], axis=0), F32)
    m2 = jnp.asarray(np.block([[c2, s2], [-s2, c2]]), F32)
    m3 = jnp.asarray(np.concatenate([cc, sc], axis=0), F32)
    ang = 2.0 * np.pi * ((np.arange(n2)[:, None] * np.arange(n1)[None, :]) % t) / t
    twc = jnp.broadcast_to(jnp.asarray(np.cos(ang), F32)[:, :, None], (n2, n1, HEAD_DIM))
    tws = jnp.broadcast_to(jnp.asarray(np.sin(ang), F32)[:, :, None], (n2, n1, HEAD_DIM))
    const = lambda shape: pl.BlockSpec(shape, lambda bi, g: (0,) * len(shape))
    return pl.pallas_call(
        functools.partial(_fourier_kernel, n1=n1, n2=n2, scale=float((t * HEAD_DIM) ** -0.5)),
        grid=(b, N_HEADS),
        in_specs=[
            pl.BlockSpec((1, t, HEAD_DIM), lambda bi, g: (bi, 0, C_FT // HEAD_DIM + g)),
            const(m1.shape), const(m2.shape), const(m3.shape),
            const((n2, n1, HEAD_DIM)), const((n2, n1, HEAD_DIM)),
        ],
        out_specs=pl.BlockSpec((1, t, HEAD_DIM), lambda bi, g: (bi, 0, g)),
        out_shape=SDS((b, t, W_BR), F32),
        scratch_shapes=[pltpu.VMEM((n1 * (n2 + FT_PAD), HEAD_DIM), F32)] * 3
        + [pltpu.VMEM((n2 * (n1 + FT_PAD), HEAD_DIM), F32)],
        compiler_params=_cp("parallel", "parallel"),
        name="fourier",
    )(pm, m1, m2, m3, twc, tws)


def _sgu_kernel(u_ref, v_ref, vn_ref, ws_ref, bs_ref, o_ref):
    u = jax.nn.gelu(u_ref[0].astype(F32))
    v = jax.nn.gelu(v_ref[0].astype(F32))
    mu = jnp.mean(v, axis=-1, keepdims=True)
    var = jnp.mean(jnp.square(v - mu), axis=-1, keepdims=True)
    vb = ((v - mu) * lax.rsqrt(var + EPS) * vn_ref[...]).astype(BF16)
    out = []
    for ch in range(u.shape[0] // SG_CHUNK):
        rs = slice(ch * SG_CHUNK, (ch + 1) * SG_CHUNK)
        mix = [_dot(ws_ref[g], vb[rs, g * HEAD_DIM:(g + 1) * HEAD_DIM]) + bs_ref[:, g:g + 1] for g in range(N_HEADS)]
        out.append((u[rs, :] * jnp.concatenate(mix, axis=1)).astype(o_ref.dtype))
    for ch, y in enumerate(out):
        o_ref[0, ch * SG_CHUNK:(ch + 1) * SG_CHUNK, :] = y


def _spatial_gating(pm, ws, bs_t, vnorm):
    b, t, _ = pm.shape
    tm = _tile(t, 4 * SG_CHUNK, SG_CHUNK)
    assert tm % SG_CHUNK == 0
    return pl.pallas_call(
        _sgu_kernel,
        grid=(b, t // tm),
        in_specs=[
            pl.BlockSpec((1, tm, W_BR), lambda bi, i: (bi, i, C_SG_U // W_BR)),
            pl.BlockSpec((1, tm, W_BR), lambda bi, i: (bi, i, C_SG_V // W_BR)),
            pl.BlockSpec((1, W_BR), lambda bi, i: (0, 0)),
            pl.BlockSpec((N_HEADS, SG_CHUNK, SG_CHUNK), lambda bi, i: (0, 0, 0)),
            pl.BlockSpec((SG_CHUNK, N_HEADS), lambda bi, i: (0, 0)),
        ],
        out_specs=pl.BlockSpec((1, tm, W_BR), lambda bi, i: (bi, i, 0)),
        out_shape=SDS((b, t, W_BR), BF16),
        compiler_params=_cp("parallel", "parallel"),
        name="sgu",
    )(pm, pm, vnorm, ws, bs_t)


def _merge_kernel(h_ref, y0_ref, y1_ref, y2_ref, y3_ref, wg_ref, bg_ref, wb_ref, o_ref):
    h = h_ref[0]
    acc = None
    for i, y_ref in enumerate((y0_ref, y1_ref, y2_ref, y3_ref)):
        gate = jax.nn.sigmoid(_dot(h, wg_ref[0, i]) + bg_ref[0, i])
        term = gate * _dot(y_ref[0].astype(BF16), wb_ref[0, i])
        acc = term if acc is None else acc + term
    o_ref[0] = acc.astype(o_ref.dtype)


def _merge(h, ys, wg, bg, wb, l):
    bx, t, d = h.shape
    tm, tn = _tile(t, 1024), _tile(d, 512, LANES)
    y_spec = pl.BlockSpec((1, tm, W_BR), lambda b, i, j: (b, i, 0))
    return pl.pallas_call(
        _merge_kernel,
        grid=(bx, t // tm, d // tn),
        in_specs=[
            pl.BlockSpec((1, tm, d), lambda b, i, j: (b, i, 0)),
            y_spec, y_spec, y_spec, y_spec,
            pl.BlockSpec((1, N_BRANCH, d, tn), lambda b, i, j: (l, 0, 0, j)),
            pl.BlockSpec((1, N_BRANCH, 1, tn), lambda b, i, j: (l, 0, 0, j)),
            pl.BlockSpec((1, N_BRANCH, W_BR, tn), lambda b, i, j: (l, 0, 0, j)),
        ],
        out_specs=pl.BlockSpec((1, tm, tn), lambda b, i, j: (b, i, j)),
        out_shape=SDS((bx, t, d), BF16),
        compiler_params=_cp("parallel", "parallel", "arbitrary"),
        name="merge",
    )(h, *ys, wg, bg, wb)


def _resid_kernel(a_ref, w_ref, x_ref, g_ref, o_ref, *, row):
    o_ref[0] = x_ref[0] + _mod_row(g_ref, row) * _dot(a_ref[0], w_ref[0])


def _resid_norm_kernel(a_ref, w_ref, x_ref, g_ref, nw_ref, sh_ref, sc_ref, o_ref, h_ref, *, row):
    x_new = x_ref[0] + _mod_row(g_ref, row) * _dot(a_ref[0], w_ref[0])
    o_ref[0] = x_new
    h_ref[0] = _norm_mod(x_new, nw_ref, sh_ref, sc_ref, row)


def _resid_proj_norm(a, w, l, x, mod, gate_blk, nw, norm_blk, row):
    bx, t, k = a.shape
    d = w.shape[2]
    assert k * d * 2 <= RESIDENT_WEIGHT_BYTES
    tm = _tile(t, 512)
    row_spec = lambda dt: pl.BlockSpec((1, tm, d), lambda b, i: (b, i, 0))
    return pl.pallas_call(
        functools.partial(_resid_norm_kernel, row=row),
        grid=(bx, t // tm),
        in_specs=[
            pl.BlockSpec((1, tm, k), lambda b, i: (b, i, 0)),
            pl.BlockSpec((1, k, d), lambda b, i: (l, 0, 0)),
            row_spec(F32),
            pl.BlockSpec((MOD_ROWS, d), lambda b, i: (0, gate_blk)),
            pl.BlockSpec((1, d), lambda b, i: (0, 0)),
            pl.BlockSpec((MOD_ROWS, d), lambda b, i: (0, norm_blk)),
            pl.BlockSpec((MOD_ROWS, d), lambda b, i: (0, norm_blk + 1)),
        ],
        out_specs=[row_spec(F32), row_spec(BF16)],
        out_shape=[SDS((bx, t, d), F32), SDS((bx, t, d), BF16)],
        compiler_params=_cp("parallel", "parallel"),
        name="resid_proj_norm",
    )(a, w, x, mod, nw, mod, mod)


def _resid_proj(a, w, l, x, mod, gate_blk, row):
    bx, t, k = a.shape
    d = w.shape[2]
    tm, tn = _tile(t, 1024), _tile(d, 512, LANES)
    nj = d // tn
    return pl.pallas_call(
        functools.partial(_resid_kernel, row=row),
        grid=(bx, t // tm, nj),
        in_specs=[
            pl.BlockSpec((1, tm, k), lambda b, i, j: (b, i, 0)),
            pl.BlockSpec((1, k, tn), lambda b, i, j: (l, 0, j)),
            pl.BlockSpec((1, tm, tn), lambda b, i, j: (b, i, j)),
            pl.BlockSpec((MOD_ROWS, tn), lambda b, i, j: (0, gate_blk * nj + j)),
        ],
        out_specs=pl.BlockSpec((1, tm, tn), lambda b, i, j: (b, i, j)),
        out_shape=SDS((bx, t, d), F32),
        compiler_params=_cp("parallel", "parallel", "arbitrary"),
        name="resid_proj",
    )(a, w, x, mod)


def _ffn_up_kernel(h_ref, w1_ref, w3_ref, o_ref):
    h = h_ref[0]
    o_ref[0] = (_silu(_dot(h, w1_ref[0])) * _dot(h, w3_ref[0])).astype(o_ref.dtype)


def _ffn_up(h, w1, w3, l):
    bx, t, d = h.shape
    f = w1.shape[2]
    tm, tn = _tile(t, FFN_UP_ROWS), _tile(f, 512, LANES)
    return pl.pallas_call(
        _ffn_up_kernel,
        grid=(bx, t // tm, f // tn),
        in_specs=[
            pl.BlockSpec((1, tm, d), lambda b, i, j: (b, i, 0)),
            pl.BlockSpec((1, d, tn), lambda b, i, j: (l, 0, j)),
            pl.BlockSpec((1, d, tn), lambda b, i, j: (l, 0, j)),
        ],
        out_specs=pl.BlockSpec((1, tm, tn), lambda b, i, j: (b, i, j)),
        out_shape=SDS((bx, t, f), BF16),
        compiler_params=_cp("parallel", "parallel", "arbitrary"),
        name="ffn_up",
    )(h, w1, w3)


def _mixers(pm, pab, pm_c, pab_c, p, need_ctx):
    b = pm.shape[0]
    y_na = _neighbourhood_attention(pm, pm_c, p["qn"], p["kn"], p["rpb"])
    zero = jnp.zeros((b, 2 * N_HEADS, HEAD_DIM, HEAD_DIM), F32)
    of_c, ob_c, s_ctx = _deltanet(_dn_conv(pm_c, p["conv"]), pab_c, p["a_log"], p["dt_bias"], zero)
    of, ob, _ = _deltanet(_dn_conv(pm, p["conv"]), pab, p["a_log"], p["dt_bias"], s_ctx)
    ys = (y_na, _dn_out(of, ob, pm, p["onorm"]), _fourier(pm), _spatial_gating(pm, p["sg_w"], p["sg_bt"], p["sg_vn"]))
    if not need_ctx:
        return ys, None
    ys_c = (_dense_attention(pm_c, p["qn"], p["kn"]), _dn_out(of_c, ob_c, pm_c, p["onorm"]), _fourier(pm_c),
            _spatial_gating(pm_c, p["sg_w"], p["sg_bt"], p["sg_vn"]))
    return ys, ys_c


def kernel(x, c, ctx, c_ctx, w_ada, b_ada, norm1_w, norm2_w, w_in, na_qnorm, na_knorm, na_rpb, dn_conv, dn_a_log, dn_dt_bias, dn_onorm, sg_w, sg_b, sg_vnorm, w_gate, b_gate, w_branch, w_out, ffn_w1, ffn_w3, ffn_w2):
    bsz, seq, d = x.shape
    ctx_len = ctx.shape[1]
    depth = w_ada.shape[0]
    assert bsz + 1 <= MOD_ROWS and d % LANES == 0
    ctx_row = bsz

    cc = jnp.zeros((MOD_ROWS, d), F32).at[:bsz].set(c).at[ctx_row].set(c_ctx)
    mod_all = _ada_mod(cc, w_ada, b_ada)
    xc = ctx.reshape(1, bsz * ctx_len, d)

    w_in_b = w_in.astype(BF16)
    w_main = jnp.concatenate([w_in_b[..., :C_AB_SRC], w_in_b[..., C_AB_SRC + N_AB:]], axis=-1)
    w_ab = jnp.pad(w_in_b[..., C_AB_SRC:C_AB_SRC + N_AB], ((0, 0), (0, 0), (0, LANES - N_AB)))
    wg, wb, wo = w_gate.astype(BF16), w_branch.astype(BF16), w_out.astype(BF16)
    w1, w3, w2 = ffn_w1.astype(BF16), ffn_w3.astype(BF16), ffn_w2.astype(BF16)
    bg = b_gate.reshape(depth, N_BRANCH, 1, d)

    for l in range(depth):
        need_ctx = l < depth - 1
        mod = mod_all[l]
        p = dict(qn=na_qnorm[l][None], kn=na_knorm[l][None], rpb=na_rpb[l], conv=dn_conv[l], a_log=dn_a_log[l],
                 dt_bias=dn_dt_bias[l], onorm=dn_onorm[l][None], sg_w=sg_w[l].astype(BF16), sg_bt=sg_b[l].T,
                 sg_vn=sg_vnorm[l][None])
        n1w, n2w = norm1_w[l][None], norm2_w[l][None]

        pm, pab, h = _in_proj(x, n1w, mod, w_main, w_ab, l, None)
        pm_c, pab_c, h_c = _in_proj(xc, n1w, mod, w_main, w_ab, l, ctx_row)
        pm_c = pm_c.reshape(bsz, ctx_len, N_MAIN)
        pab_c = pab_c.reshape(bsz, ctx_len, LANES)
        ys, ys_c = _mixers(pm, pab, pm_c, pab_c, p, need_ctx)

        merged = _merge(h, ys, wg, bg, wb, l)
        x, h2 = _resid_proj_norm(merged, wo, l, x, mod, 2, n2w, 3, None)
        x = _resid_proj(_ffn_up(h2, w1, w3, l), w2, l, x, mod, 5, None)
        if need_ctx:
            ys_c = tuple(y.reshape(1, bsz * ctx_len, W_BR) for y in ys_c)
            merged_c = _merge(h_c, ys_c, wg, bg, wb, l)
            xc, h2_c = _resid_proj_norm(merged_c, wo, l, xc, mod, 2, n2w, 3, ctx_row)
            xc = _resid_proj(_ffn_up(h2_c, w1, w3, l), w2, l, xc, mod, 5, ctx_row)
    return x
```

```python
import functools

import numpy as np
import jax
import jax.numpy as jnp
from jax import lax
from jax.experimental import pallas as pl
from jax.experimental.pallas import tpu as pltpu

F32 = jnp.float32
BF16 = jnp.bfloat16
SDS = jax.ShapeDtypeStruct

LANES = 128
HEAD_DIM = 128
N_HEADS = 4
W_BR = N_HEADS * HEAD_DIM
N_BRANCH = 4
GRID_W = 64
NA_WIN_R = 8
NA_WIN_C = 16
NA_ROWS_PER_STEP = 16
DN_CHUNK = 64
DN_BATCH_PER_STEP = 4
DN_INV_BASE = 8
DN_CONV = 5
DN_CONV_PAD = 8
DN_CONV_WIDE_TILE_ROWS = 1024
SG_CHUNK = 128
FT_UNROLL = 8
FT_PAD = 4
N_MOD = 6
EPS = 1e-6
NEG_INF = -1e30
MOD_ROWS = 8
N_AB = 4 * N_HEADS
VMEM_LIMIT_BYTES = 56 * 1024 * 1024
RESIDENT_WEIGHT_BYTES = 8 * 1024 * 1024
FFN_UP_ROWS = 2048

C_NA_Q, C_NA_K, C_NA_V = 0, W_BR, 2 * W_BR
C_DN_QKV = 3 * W_BR
C_DN_Z = 6 * W_BR
C_FT = 7 * W_BR
C_SG_U = 8 * W_BR
C_SG_V = 9 * W_BR
N_MAIN = 10 * W_BR
C_AB_SRC = 7 * W_BR


def _cp(*sem):
    return pltpu.CompilerParams(dimension_semantics=sem, vmem_limit_bytes=VMEM_LIMIT_BYTES)


def _tile(n, pref, mult=8):
    if n <= pref:
        return n
    for t in range(pref - pref % mult, 0, -mult):
        if n % t == 0:
            return t
    return n


def _dot(a, b):
    return jnp.dot(a, b, preferred_element_type=F32)


def _dot_nt(a, b):
    return lax.dot_general(a, b, (((1,), (1,)), ((), ())), preferred_element_type=F32)


def _dot_tn(a, b):
    return lax.dot_general(a, b, (((0,), (0,)), ((), ())), preferred_element_type=F32)


def _silu(x):
    return x * jax.nn.sigmoid(x)


def _rms(x, w):
    return x * lax.rsqrt(jnp.mean(x * x, axis=-1, keepdims=True) + EPS) * w


def _mod_row(ref, row):
    b = pl.program_id(0) if row is None else row
    return ref[pl.ds(b, 1), :]


def _norm_mod(x, nw_ref, sh_ref, sc_ref, row):
    return (_rms(x, nw_ref[...]) * (1.0 + _mod_row(sc_ref, row)) + _mod_row(sh_ref, row)).astype(BF16)


def _ada_kernel(c_ref, w_ref, b_ref, o_ref):
    s = _silu(c_ref[...]).astype(BF16)
    o_ref[0] = _dot(s, w_ref[0].astype(BF16)) + b_ref[0]


def _ada_mod(cc, w_ada, b_ada):
    depth, d, n = w_ada.shape
    tn = _tile(n, 1024, LANES)
    return pl.pallas_call(
        _ada_kernel,
        grid=(depth, n // tn),
        in_specs=[
            pl.BlockSpec((MOD_ROWS, d), lambda l, j: (0, 0)),
            pl.BlockSpec((1, d, tn), lambda l, j: (l, 0, j)),
            pl.BlockSpec((1, 1, tn), lambda l, j: (l, 0, j)),
        ],
        out_specs=pl.BlockSpec((1, MOD_ROWS, tn), lambda l, j: (l, 0, j)),
        out_shape=SDS((depth, MOD_ROWS, n), F32),
        compiler_params=_cp("parallel", "parallel"),
        name="ada_mod",
    )(cc, w_ada, b_ada.reshape(depth, 1, n))


def _in_proj_kernel(x_ref, nw_ref, sh_ref, sc_ref, w_ref, wab_ref, o_ref, oab_ref, h_ref, hs_ref, *, row):
    @pl.when(pl.program_id(2) == 0)
    def _():
        h = _norm_mod(x_ref[0], nw_ref, sh_ref, sc_ref, row)
        hs_ref[...] = h
        h_ref[0] = h
        oab_ref[0] = _dot(h, wab_ref[0])

    o_ref[0] = _dot(hs_ref[...], w_ref[0]).astype(o_ref.dtype)


def _in_proj(x, nw, mod, w_main, w_ab, l, row):
    bx, t, d = x.shape
    n = w_main.shape[2]
    tm, tn = _tile(t, 1024), _tile(n, 1280, LANES)
    return pl.pallas_call(
        functools.partial(_in_proj_kernel, row=row),
        grid=(bx, t // tm, n // tn),
        in_specs=[
            pl.BlockSpec((1, tm, d), lambda b, i, j: (b, i, 0)),
            pl.BlockSpec((1, d), lambda b, i, j: (0, 0)),
            pl.BlockSpec((MOD_ROWS, d), lambda b, i, j: (0, 0)),
            pl.BlockSpec((MOD_ROWS, d), lambda b, i, j: (0, 1)),
            pl.BlockSpec((1, d, tn), lambda b, i, j: (l, 0, j)),
            pl.BlockSpec((1, d, LANES), lambda b, i, j: (l, 0, 0)),
        ],
        out_specs=[
            pl.BlockSpec((1, tm, tn), lambda b, i, j: (b, i, j)),
            pl.BlockSpec((1, tm, LANES), lambda b, i, j: (b, i, 0)),
            pl.BlockSpec((1, tm, d), lambda b, i, j: (b, i, 0)),
        ],
        out_shape=[SDS((bx, t, n), BF16), SDS((bx, t, LANES), F32), SDS((bx, t, d), BF16)],
        scratch_shapes=[pltpu.VMEM((tm, d), BF16)],
        compiler_params=_cp("parallel", "parallel", "arbitrary"),
        name="in_proj",
    )(x, nw, mod, mod, w_main, w_ab)


def _na_kernel(q_ref, k_ref, v_ref, kc_ref, vc_ref, qn_ref, kn_ref, bias_ref, o_ref,
               qs, ks, vs, kcs, vcs, *, rows):
    averaging = jnp.full((HEAD_DIM, HEAD_DIM), 1.0 / HEAD_DIM, BF16)

    def rms(x, w):
        return x * lax.rsqrt(_dot((x * x).astype(BF16), averaging) + EPS) * w

    qs[...] = (rms(q_ref[0].astype(F32), qn_ref[...]) * HEAD_DIM ** -0.5).astype(BF16)
    ks[...] = rms(k_ref[0].astype(F32), kn_ref[...]).astype(BF16)
    vs[...] = v_ref[0].astype(BF16)
    kcs[...] = rms(kc_ref[0].astype(F32), kn_ref[...]).astype(BF16)
    vcs[...] = vc_ref[0].astype(BF16)
    band = NA_WIN_R * GRID_W
    nr = NA_ROWS_PER_STEP if rows % NA_ROWS_PER_STEP == 0 else NA_WIN_R

    def body(g, carry):
        rr = [g * nr + i for i in range(nr)]
        start = [jnp.clip(r - NA_WIN_R // 2, 0, rows - NA_WIN_R) for r in rr]
        r0 = [pl.multiple_of(r * GRID_W, GRID_W) for r in rr]
        k0 = [pl.multiple_of(s * GRID_W, GRID_W) for s in start]
        q = [qs[pl.ds(x, GRID_W), :] for x in r0]
        kb = [ks[pl.ds(x, band), :] for x in k0]
        vb = [vs[pl.ds(x, band), :] for x in k0]
        bias = [bias_ref[0, r - s] for r, s in zip(rr, start)]
        kc, vc = kcs[...], vcs[...]
        s_loc = [_dot_nt(q[i], kb[i]) + bias[i] for i in range(nr)]
        s_ctx = [_dot_nt(q[i], kc) for i in range(nr)]
        m = [jnp.maximum(jnp.max(s_loc[i], axis=-1, keepdims=True), jnp.max(s_ctx[i], axis=-1, keepdims=True))
             for i in range(nr)]
        p = [jnp.exp(s_loc[i] - m[i]) for i in range(nr)]
        pc = [jnp.exp(s_ctx[i] - m[i]) for i in range(nr)]
        denom = [jnp.sum(p[i], axis=-1, keepdims=True) + jnp.sum(pc[i], axis=-1, keepdims=True) for i in range(nr)]
        o = [_dot(p[i].astype(BF16), vb[i]) + _dot(pc[i].astype(BF16), vc) for i in range(nr)]
        for i in range(nr):
            o_ref[0, pl.ds(r0[i], GRID_W), :] = (o[i] / denom[i]).astype(o_ref.dtype)
        return carry

    lax.fori_loop(0, rows // nr, body, 0)


def _na_bias_table(rpb):
    col = np.arange(GRID_W)
    dc = np.clip(col[None, :] - col[:, None], 1 - NA_WIN_C, NA_WIN_C - 1) + NA_WIN_C - 1
    col_start = np.clip(col - NA_WIN_C // 2, 0, GRID_W - NA_WIN_C)
    in_win = (col[None, :] >= col_start[:, None]) & (col[None, :] < col_start[:, None] + NA_WIN_C)
    rows = jnp.stack([rpb[:, NA_WIN_R - 1 - typ:2 * NA_WIN_R - 1 - typ, :] for typ in range(NA_WIN_R)], axis=1)
    onehot = (dc[None] == np.arange(2 * NA_WIN_C - 1)[:, None, None]).astype(np.float32)
    b = jnp.einsum('htjc,cqk->htqjk', rows.astype(F32), onehot, precision=lax.Precision.HIGHEST)
    b = jnp.where(in_win[None, None, :, None, :], b, NEG_INF)
    return b.reshape(rpb.shape[0], NA_WIN_R, GRID_W, NA_WIN_R * GRID_W)


def _neighbourhood_attention(pm, pm_c, qn, kn, rpb):
    b, t, _ = pm.shape
    ctx_len = pm_c.shape[1]
    rows = t // GRID_W
    assert rows >= NA_WIN_R and rows * GRID_W == t and rows % NA_WIN_R == 0
    band = NA_WIN_R * GRID_W
    hb = lambda off: (lambda bi, h: (bi, 0, off // HEAD_DIM + h))
    return pl.pallas_call(
        functools.partial(_na_kernel, rows=rows),
        grid=(b, N_HEADS),
        in_specs=[
            pl.BlockSpec((1, t, HEAD_DIM), hb(C_NA_Q)),
            pl.BlockSpec((1, t, HEAD_DIM), hb(C_NA_K)),
            pl.BlockSpec((1, t, HEAD_DIM), hb(C_NA_V)),
            pl.BlockSpec((1, ctx_len, HEAD_DIM), hb(C_NA_K)),
            pl.BlockSpec((1, ctx_len, HEAD_DIM), hb(C_NA_V)),
            pl.BlockSpec((1, HEAD_DIM), lambda bi, h: (0, 0)),
            pl.BlockSpec((1, HEAD_DIM), lambda bi, h: (0, 0)),
            pl.BlockSpec((1, NA_WIN_R, GRID_W, band), lambda bi, h: (h, 0, 0, 0)),
        ],
        out_specs=pl.BlockSpec((1, t, HEAD_DIM), lambda bi, h: (bi, 0, h)),
        out_shape=SDS((b, t, W_BR), BF16),
        scratch_shapes=[pltpu.VMEM((t, HEAD_DIM), BF16)] * 3 + [pltpu.VMEM((ctx_len, HEAD_DIM), BF16)] * 2,
        compiler_params=_cp("parallel", "parallel"),
        name="natten",
    )(pm, pm, pm, pm_c, pm_c, qn, kn, _na_bias_table(rpb))


def _ctx_attn_kernel(q_ref, k_ref, v_ref, qn_ref, kn_ref, o_ref):
    q = _rms(q_ref[0].astype(F32), qn_ref[...]).astype(BF16)
    k = _rms(k_ref[0].astype(F32), kn_ref[...]).astype(BF16)
    s = _dot_nt(q, k) * HEAD_DIM ** -0.5
    p = jnp.exp(s - jnp.max(s, axis=-1, keepdims=True))
    o = _dot(p.astype(BF16), v_ref[0].astype(BF16)) / jnp.sum(p, axis=-1, keepdims=True)
    o_ref[0] = o.astype(o_ref.dtype)


def _dense_attention(pm_c, qn, kn):
    b, ctx_len, _ = pm_c.shape
    hb = lambda off: (lambda bi, h: (bi, 0, off // HEAD_DIM + h))
    return pl.pallas_call(
        _ctx_attn_kernel,
        grid=(b, N_HEADS),
        in_specs=[
            pl.BlockSpec((1, ctx_len, HEAD_DIM), hb(C_NA_Q)),
            pl.BlockSpec((1, ctx_len, HEAD_DIM), hb(C_NA_K)),
            pl.BlockSpec((1, ctx_len, HEAD_DIM), hb(C_NA_V)),
            pl.BlockSpec((1, HEAD_DIM), lambda bi, h: (0, 0)),
            pl.BlockSpec((1, HEAD_DIM), lambda bi, h: (0, 0)),
        ],
        out_specs=pl.BlockSpec((1, ctx_len, HEAD_DIM), lambda bi, h: (bi, 0, h)),
        out_shape=SDS((b, ctx_len, W_BR), BF16),
        compiler_params=_cp("parallel", "parallel"),
        name="ctx_attn",
    )(pm_c, pm_c, pm_c, qn, kn)


def _dn_conv_kernel(x_ref, w_ref, o_ref, xp_ref, *, n_norm_tiles):
    x = x_ref[0].astype(F32)
    t = x.shape[0]
    border = jnp.zeros((DN_CONV_PAD, x.shape[1]), F32)
    xp_ref[0:DN_CONV_PAD, :] = border
    xp_ref[DN_CONV_PAD + t:, :] = border
    xp_ref[DN_CONV_PAD:DN_CONV_PAD + t, :] = x
    half = DN_CONV // 2
    acc = x * w_ref[half:half + 1, :]
    for kk in range(DN_CONV):
        d = kk - half
        if d != 0:
            acc = acc + xp_ref[DN_CONV_PAD + d:DN_CONV_PAD + d + t, :] * w_ref[kk:kk + 1, :]
    y = _silu(acc)
    normed = pl.program_id(1) < n_norm_tiles
    for h in range(y.shape[1] // HEAD_DIM):
        sl = slice(h * HEAD_DIM, (h + 1) * HEAD_DIM)
        yh = y[:, sl]
        unit = yh * lax.rsqrt(jnp.sum(yh * yh, axis=-1, keepdims=True) + EPS)
        o_ref[0, :, sl] = jnp.where(normed, unit, yh)


def _dn_conv(pm, conv_w):
    b, t, _ = pm.shape
    tw = W_BR if t <= DN_CONV_WIDE_TILE_ROWS else HEAD_DIM
    return pl.pallas_call(
        functools.partial(_dn_conv_kernel, n_norm_tiles=2 * W_BR // tw),
        grid=(b, 3 * W_BR // tw),
        in_specs=[
            pl.BlockSpec((1, t, tw), lambda bi, j: (bi, 0, C_DN_QKV // tw + j)),
            pl.BlockSpec((DN_CONV, tw), lambda bi, j: (0, j)),
        ],
        out_specs=pl.BlockSpec((1, t, tw), lambda bi, j: (bi, 0, j)),
        out_shape=SDS((b, t, 3 * W_BR), F32),
        scratch_shapes=[pltpu.VMEM((t + 2 * DN_CONV_PAD, tw), F32)],
        compiler_params=_cp("parallel", "parallel"),
        name="dn_conv",
    )(pm, conv_w)


def _softplus(x):
    return jnp.maximum(x, 0.0) + jnp.log1p(jnp.exp(-jnp.abs(x)))


def _block_masks(n):
    ii = lax.broadcasted_iota(jnp.int32, (n, n), 0)
    jj = lax.broadcasted_iota(jnp.int32, (n, n), 1)
    same = lambda s: (ii // s) == (jj // s)
    masks, s = [same(DN_INV_BASE)], DN_INV_BASE
    while s < n:
        masks.append(same(2 * s) & jnp.logical_not(same(s)))
        s *= 2
    return masks


def _unit_triangular_inverses(a_list, eye, masks):
    mm = lambda x, y: _dot(x.astype(BF16), y.astype(BF16))
    n = len(a_list)
    pw = [jnp.where(masks[0], a, 0.0) for a in a_list]
    inv = [eye - p for p in pw]
    s = 2
    while s < DN_INV_BASE:
        pw = [mm(p, p) for p in pw]
        inv = [inv[i] + mm(inv[i], pw[i]) for i in range(n)]
        s *= 2
    for m in masks[1:]:
        t = [mm(jnp.where(m, a_list[i], 0.0), inv[i]) for i in range(n)]
        inv = [inv[i] - mm(inv[i], t[i]) for i in range(n)]
    return inv


def _dn_kernel(qf_ref, kf_ref, vf_ref, qb_ref, kb_ref, vb_ref, abf_ref, abb_ref, abtf_ref, abtb_ref,
               prow_ref, pcol_ref, s0_ref, of_ref, ob_ref, sfin_ref, s_ref):
    c = pl.program_id(1)

    @pl.when(c == 0)
    def _():
        s_ref[...] = s0_ref[...].reshape(s_ref.shape)

    cs = DN_CHUNK
    ii = lax.broadcasted_iota(jnp.int32, (cs, cs), 0)
    jj = lax.broadcasted_iota(jnp.int32, (cs, cs), 1)
    eye = (ii == jj).astype(F32)
    blk_masks = _block_masks(cs)
    scale = HEAD_DIM ** -0.5
    hi = lax.Precision.HIGHEST

    nb = qf_ref.shape[0]
    nst = 2 * N_HEADS
    chains = [(bb, d, h) for bb in range(nb) for d in range(2) for h in range(N_HEADS)]
    nch = len(chains)
    incl = [(jj <= ii), (jj >= ii)]
    strict = [(jj < ii), (jj > ii)]
    gc_col, gc_row, beta_all = {}, {}, {}
    for bb in range(nb):
        for d in range(2):
            tri = incl[d].astype(F32)
            ab = (abf_ref if d == 0 else abb_ref)[bb]
            abt = (abtf_ref if d == 0 else abtb_ref)[bb, 0]
            g_col = -jnp.exp(prow_ref[0:1, :]) * _softplus(ab + prow_ref[1:2, :])
            g_row = -jnp.exp(pcol_ref[:, 0:1]) * _softplus(abt + pcol_ref[:, 1:2])
            gc_col[bb, d] = jnp.dot(tri, g_col, preferred_element_type=F32, precision=hi)
            gc_row[bb, d] = lax.dot_general(g_row, tri, (((1,), (1,)), ((), ())),
                                            preferred_element_type=F32, precision=hi)
            beta_all[bb, d] = jax.nn.sigmoid(ab)
    qkv_refs = ((qf_ref, kf_ref, vf_ref), (qb_ref, kb_ref, vb_ref))
    sl = [slice(h * HEAD_DIM, (h + 1) * HEAD_DIM) for h in range(N_HEADS)]
    col = [d * N_HEADS + h for bb, d, h in chains]
    q = [qkv_refs[d][0][bb, :, sl[h]] * scale for bb, d, h in chains]
    k = [qkv_refs[d][1][bb, :, sl[h]] for bb, d, h in chains]
    v = [qkv_refs[d][2][bb, :, sl[h]] for bb, d, h in chains]
    s_old = [s_ref[bb * nst + col[j]] for j, (bb, d, h) in enumerate(chains)]
    gcc = [gc_col[bb, d][:, col[j]:col[j] + 1] for j, (bb, d, h) in enumerate(chains)]
    gcr = [gc_row[bb, d][col[j]:col[j] + 1, :] for j, (bb, d, h) in enumerate(chains)]
    beta = [beta_all[bb, d][:, nst + col[j]:nst + col[j] + 1] for j, (bb, d, h) in enumerate(chains)]
    decay = [jnp.exp(jnp.where(incl[d], gcc[j] - gcr[j], -jnp.inf)) for j, (bb, d, h) in enumerate(chains)]
    eg = [jnp.exp(x) for x in gcc]
    g_last = [gcc[j][cs - 1:cs, :] if d == 0 else gcc[j][0:1, :] for j, (bb, d, h) in enumerate(chains)]
    kbf = [x.astype(BF16) for x in k]
    kbeta = [k[j] * beta[j] for j in range(nch)]
    kq = [_dot_nt(jnp.concatenate([kbeta[j], q[j]], axis=0).astype(BF16), kbf[j]) for j in range(nch)]
    a_mat = [jnp.where(strict[d], kq[j][:cs] * decay[j], 0.0) for j, (bb, d, h) in enumerate(chains)]
    qk = [jnp.where(incl[d], kq[j][cs:] * decay[j], 0.0) for j, (bb, d, h) in enumerate(chains)]
    inv = _unit_triangular_inverses(a_mat, eye, blk_masks)
    rhs = [jnp.concatenate([v[j] * beta[j], kbeta[j] * eg[j]], axis=1).astype(BF16) for j in range(nch)]
    uw = [_dot(inv[j].astype(BF16), rhs[j]) for j in range(nch)]
    ws_qs = [_dot(jnp.concatenate([uw[j][:, HEAD_DIM:], q[j] * eg[j]], axis=0).astype(BF16), s_old[j].astype(BF16))
             for j in range(nch)]
    v_new = [uw[j][:, :HEAD_DIM] - ws_qs[j][:cs] for j in range(nch)]
    vnb = [x.astype(BF16) for x in v_new]
    o = [ws_qs[j][cs:] + _dot(qk[j].astype(BF16), vnb[j]) for j in range(nch)]
    kd = [(k[j] * jnp.exp(g_last[j] - gcc[j])).astype(BF16) for j in range(nch)]
    s_new = [s_old[j] * jnp.exp(g_last[j]) + _dot_tn(kd[j], vnb[j]) for j in range(nch)]
    for j, (bb, d, h) in enumerate(chains):
        s_ref[bb * nst + col[j]] = s_new[j]
        (of_ref if d == 0 else ob_ref)[bb, :, sl[h]] = o[j]

    @pl.when(c == pl.num_programs(1) - 1)
    def _():
        sfin_ref[...] = s_ref[...].reshape(sfin_ref.shape)


def _deltanet(act, pab, a_log, dt_bias, s0):
    b, t, _ = act.shape
    n = t // DN_CHUNK
    assert n * DN_CHUNK == t
    abt = jnp.swapaxes(pab[..., :N_AB].reshape(b, n, DN_CHUNK, N_AB), 2, 3)
    al, dt = a_log.reshape(-1).astype(F32), dt_bias.reshape(-1).astype(F32)
    prow = jnp.zeros((2, LANES), F32).at[0, :al.size].set(al).at[1, :dt.size].set(dt)
    pcol = jnp.zeros((N_AB, 2), F32).at[:al.size, 0].set(al).at[:dt.size, 1].set(dt)
    nb = DN_BATCH_PER_STEP if b % DN_BATCH_PER_STEP == 0 else 1
    fwd = lambda part: (lambda bi, c: (bi, c, part))
    bwd = lambda part: (lambda bi, c: (bi, n - 1 - c, part))
    qkv_spec = lambda imap: pl.BlockSpec((nb, DN_CHUNK, W_BR), imap)
    n_state = 2 * N_HEADS
    state_spec = pl.BlockSpec((nb, n_state, HEAD_DIM, HEAD_DIM), lambda bi, c: (bi, 0, 0, 0))
    return pl.pallas_call(
        _dn_kernel,
        grid=(b // nb, n),
        in_specs=[
            qkv_spec(fwd(0)), qkv_spec(fwd(1)), qkv_spec(fwd(2)),
            qkv_spec(bwd(0)), qkv_spec(bwd(1)), qkv_spec(bwd(2)),
            pl.BlockSpec((nb, DN_CHUNK, LANES), fwd(0)),
            pl.BlockSpec((nb, DN_CHUNK, LANES), bwd(0)),
            pl.BlockSpec((nb, 1, N_AB, DN_CHUNK), lambda bi, c: (bi, c, 0, 0)),
            pl.BlockSpec((nb, 1, N_AB, DN_CHUNK), lambda bi, c: (bi, n - 1 - c, 0, 0)),
            pl.BlockSpec((2, LANES), lambda bi, c: (0, 0)),
            pl.BlockSpec((N_AB, 2), lambda bi, c: (0, 0)),
            state_spec,
        ],
        out_specs=[
            pl.BlockSpec((nb, DN_CHUNK, W_BR), fwd(0)),
            pl.BlockSpec((nb, DN_CHUNK, W_BR), bwd(0)),
            state_spec,
        ],
        out_shape=[SDS((b, t, W_BR), F32), SDS((b, t, W_BR), F32),
                   SDS((b, n_state, HEAD_DIM, HEAD_DIM), F32)],
        scratch_shapes=[pltpu.VMEM((nb * n_state, HEAD_DIM, HEAD_DIM), F32)],
        compiler_params=_cp("parallel", "arbitrary"),
        name="deltanet",
    )(act, act, act, act, act, act, pab, pab, abt, abt, prow, pcol, s0)


def _dn_out_kernel(of_ref, ob_ref, z_ref, w_ref, y_ref):
    o = of_ref[0] + ob_ref[0]
    z = z_ref[0].astype(F32)
    for h in range(N_HEADS):
        sl = slice(h * HEAD_DIM, (h + 1) * HEAD_DIM)
        y_ref[0, :, sl] = (_rms(o[:, sl], w_ref[...]) * _silu(z[:, sl])).astype(y_ref.dtype)


def _dn_out(o_f, o_b, pm, onorm):
    b, t, _ = o_f.shape
    tm = _tile(t, 1024)
    return pl.pallas_call(
        _dn_out_kernel,
        grid=(b, t // tm),
        in_specs=[
            pl.BlockSpec((1, tm, W_BR), lambda bi, i: (bi, i, 0)),
            pl.BlockSpec((1, tm, W_BR), lambda bi, i: (bi, i, 0)),
            pl.BlockSpec((1, tm, W_BR), lambda bi, i: (bi, i, C_DN_Z // W_BR)),
            pl.BlockSpec((1, HEAD_DIM), lambda bi, i: (0, 0)),
        ],
        out_specs=pl.BlockSpec((1, tm, W_BR), lambda bi, i: (bi, i, 0)),
        out_shape=SDS((b, t, W_BR), BF16),
        compiler_params=_cp("parallel", "parallel"),
        name="dn_out",
    )(o_f, o_b, pm, onorm)


def _fourier_kernel(x_ref, m1_ref, m2_ref, m3_ref, twc_ref, tws_ref, o_ref, xf_ref, bre_ref, bim_ref, zo_ref,
                    *, n1, n2, scale):
    m1, m2, m3 = m1_ref[...].astype(BF16), m2_ref[...].astype(BF16), m3_ref[...].astype(BF16)
    nu = FT_UNROLL
    p2, p1 = n2 + FT_PAD, n1 + FT_PAD
    for t1 in range(n1):
        xf_ref[t1 * p2:t1 * p2 + n2, :] = x_ref[0, t1 * n2:(t1 + 1) * n2, :].astype(F32)

    def stage1(i, carry):
        t2 = [i * nu + u for u in range(nu)]
        xs = [xf_ref[pl.ds(t, n1, stride=p2), :].astype(BF16) for t in t2]
        a = [_dot(m1, x) for x in xs]
        cw = [twc_ref[t] for t in t2]
        sw = [tws_ref[t] for t in t2]
        bre = [a[u][:n1] * cw[u] - a[u][n1:] * sw[u] for u in range(nu)]
        bim = [-(a[u][n1:] * cw[u] + a[u][:n1] * sw[u]) for u in range(nu)]
        for u in range(nu):
            bre_ref[pl.ds(t2[u], n1, stride=p2), :] = bre[u]
            bim_ref[pl.ds(t2[u], n1, stride=p2), :] = bim[u]
        return carry

    lax.fori_loop(0, n2 // nu, stage1, 0)

    def stage2(i, carry):
        f1 = [i * nu + u for u in range(nu)]
        r0 = [pl.multiple_of(f * p2, FT_PAD) for f in f1]
        b = [jnp.concatenate([bre_ref[pl.ds(r, n2), :], bim_ref[pl.ds(r, n2), :]], axis=0).astype(BF16) for r in r0]
        y = [_dot(m2, x) for x in b]
        z = [_dot(jnp.concatenate([yy[:n2], yy[n2:]], axis=1).astype(BF16), m3) for yy in y]
        for u in range(nu):
            zo_ref[pl.ds(f1[u], n2, stride=p1), :] = z[u] * scale
        return carry

    lax.fori_loop(0, n1 // nu, stage2, 0)
    for f2 in range(n2):
        o_ref[0, f2 * n1:(f2 + 1) * n1, :] = zo_ref[f2 * p1:f2 * p1 + n1, :]


def _dft_mats(n):
    ang = 2.0 * np.pi * ((np.arange(n)[:, None] * np.arange(n)[None, :]) % n) / n
    return np.cos(ang), np.sin(ang)


def _fourier(pm):
    b, t, _ = pm.shape
    n1 = 1 << ((t.bit_length() - 1 + 1) // 2)
    n2 = t // n1
    assert n1 * n2 == t and n1 % FT_UNROLL == 0 and n2 % FT_UNROLL == 0
    c1, s1 = _dft_mats(n1)
    c2, s2 = _dft_mats(n2)
    cc, sc = _dft_mats(HEAD_DIM)
    m1 = jnp.asarray(np.concatenate([c1, s1], axis=0), F32)
    m2 = jnp.asarray(np.block([[c2, s2], [-s2, c2]]), F32)
    m3 = jnp.asarray(np.concatenate([cc, sc], axis=0), F32)
    ang = 2.0 * np.pi * ((np.arange(n2)[:, None] * np.arange(n1)[None, :]) % t) / t
    twc = jnp.broadcast_to(jnp.asarray(np.cos(ang), F32)[:, :, None], (n2, n1, HEAD_DIM))
    tws = jnp.broadcast_to(jnp.asarray(np.sin(ang), F32)[:, :, None], (n2, n1, HEAD_DIM))
    const = lambda shape: pl.BlockSpec(shape, lambda bi, g: (0,) * len(shape))
    return pl.pallas_call(
        functools.partial(_fourier_kernel, n1=n1, n2=n2, scale=float((t * HEAD_DIM) ** -0.5)),
        grid=(b, N_HEADS),
        in_specs=[
            pl.BlockSpec((1, t, HEAD_DIM), lambda bi, g: (bi, 0, C_FT // HEAD_DIM + g)),
            const(m1.shape), const(m2.shape), const(m3.shape),
            const((n2, n1, HEAD_DIM)), const((n2, n1, HEAD_DIM)),
        ],
        out_specs=pl.BlockSpec((1, t, HEAD_DIM), lambda bi, g: (bi, 0, g)),
        out_shape=SDS((b, t, W_BR), F32),
        scratch_shapes=[pltpu.VMEM((n1 * (n2 + FT_PAD), HEAD_DIM), F32)] * 3
        + [pltpu.VMEM((n2 * (n1 + FT_PAD), HEAD_DIM), F32)],
        compiler_params=_cp("parallel", "parallel"),
        name="fourier",
    )(pm, m1, m2, m3, twc, tws)


def _sgu_kernel(u_ref, v_ref, vn_ref, ws_ref, bs_ref, o_ref):
    u = jax.nn.gelu(u_ref[0].astype(F32))
    v = jax.nn.gelu(v_ref[0].astype(F32))
    mu = jnp.mean(v, axis=-1, keepdims=True)
    var = jnp.mean(jnp.square(v - mu), axis=-1, keepdims=True)
    vb = ((v - mu) * lax.rsqrt(var + EPS) * vn_ref[...]).astype(BF16)
    out = []
    for ch in range(u.shape[0] // SG_CHUNK):
        rs = slice(ch * SG_CHUNK, (ch + 1) * SG_CHUNK)
        mix = [_dot(ws_ref[g], vb[rs, g * HEAD_DIM:(g + 1) * HEAD_DIM]) + bs_ref[:, g:g + 1] for g in range(N_HEADS)]
        out.append((u[rs, :] * jnp.concatenate(mix, axis=1)).astype(o_ref.dtype))
    for ch, y in enumerate(out):
        o_ref[0, ch * SG_CHUNK:(ch + 1) * SG_CHUNK, :] = y


def _spatial_gating(pm, ws, bs_t, vnorm):
    b, t, _ = pm.shape
    tm = _tile(t, 4 * SG_CHUNK, SG_CHUNK)
    assert tm % SG_CHUNK == 0
    return pl.pallas_call(
        _sgu_kernel,
        grid=(b, t // tm),
        in_specs=[
            pl.BlockSpec((1, tm, W_BR), lambda bi, i: (bi, i, C_SG_U // W_BR)),
            pl.BlockSpec((1, tm, W_BR), lambda bi, i: (bi, i, C_SG_V // W_BR)),
            pl.BlockSpec((1, W_BR), lambda bi, i: (0, 0)),
            pl.BlockSpec((N_HEADS, SG_CHUNK, SG_CHUNK), lambda bi, i: (0, 0, 0)),
            pl.BlockSpec((SG_CHUNK, N_HEADS), lambda bi, i: (0, 0)),
        ],
        out_specs=pl.BlockSpec((1, tm, W_BR), lambda bi, i: (bi, i, 0)),
        out_shape=SDS((b, t, W_BR), BF16),
        compiler_params=_cp("parallel", "parallel"),
        name="sgu",
    )(pm, pm, vnorm, ws, bs_t)


def _merge_kernel(h_ref, y0_ref, y1_ref, y2_ref, y3_ref, wg_ref, bg_ref, wb_ref, o_ref):
    h = h_ref[0]
    acc = None
    for i, y_ref in enumerate((y0_ref, y1_ref, y2_ref, y3_ref)):
        gate = jax.nn.sigmoid(_dot(h, wg_ref[0, i]) + bg_ref[0, i])
        term = gate * _dot(y_ref[0].astype(BF16), wb_ref[0, i])
        acc = term if acc is None else acc + term
    o_ref[0] = acc.astype(o_ref.dtype)


def _merge(h, ys, wg, bg, wb, l):
    bx, t, d = h.shape
    tm, tn = _tile(t, 1024), _tile(d, 512, LANES)
    y_spec = pl.BlockSpec((1, tm, W_BR), lambda b, i, j: (b, i, 0))
    return pl.pallas_call(
        _merge_kernel,
        grid=(bx, t // tm, d // tn),
        in_specs=[
            pl.BlockSpec((1, tm, d), lambda b, i, j: (b, i, 0)),
            y_spec, y_spec, y_spec, y_spec,
            pl.BlockSpec((1, N_BRANCH, d, tn), lambda b, i, j: (l, 0, 0, j)),
            pl.BlockSpec((1, N_BRANCH, 1, tn), lambda b, i, j: (l, 0, 0, j)),
            pl.BlockSpec((1, N_BRANCH, W_BR, tn), lambda b, i, j: (l, 0, 0, j)),
        ],
        out_specs=pl.BlockSpec((1, tm, tn), lambda b, i, j: (b, i, j)),
        out_shape=SDS((bx, t, d), BF16),
        compiler_params=_cp("parallel", "parallel", "arbitrary"),
        name="merge",
    )(h, *ys, wg, bg, wb)


def _resid_kernel(a_ref, w_ref, x_ref, g_ref, o_ref, *, row):
    o_ref[0] = x_ref[0] + _mod_row(g_ref, row) * _dot(a_ref[0], w_ref[0])


def _resid_norm_kernel(a_ref, w_ref, x_ref, g_ref, nw_ref, sh_ref, sc_ref, o_ref, h_ref, *, row):
    x_new = x_ref[0] + _mod_row(g_ref, row) * _dot(a_ref[0], w_ref[0])
    o_ref[0] = x_new
    h_ref[0] = _norm_mod(x_new, nw_ref, sh_ref, sc_ref, row)


def _resid_proj_norm(a, w, l, x, mod, gate_blk, nw, norm_blk, row):
    bx, t, k = a.shape
    d = w.shape[2]
    assert k * d * 2 <= RESIDENT_WEIGHT_BYTES
    tm = _tile(t, 512)
    row_spec = lambda dt: pl.BlockSpec((1, tm, d), lambda b, i: (b, i, 0))
    return pl.pallas_call(
        functools.partial(_resid_norm_kernel, row=row),
        grid=(bx, t // tm),
        in_specs=[
            pl.BlockSpec((1, tm, k), lambda b, i: (b, i, 0)),
            pl.BlockSpec((1, k, d), lambda b, i: (l, 0, 0)),
            row_spec(F32),
            pl.BlockSpec((MOD_ROWS, d), lambda b, i: (0, gate_blk)),
            pl.BlockSpec((1, d), lambda b, i: (0, 0)),
            pl.BlockSpec((MOD_ROWS, d), lambda b, i: (0, norm_blk)),
            pl.BlockSpec((MOD_ROWS, d), lambda b, i: (0, norm_blk + 1)),
        ],
        out_specs=[row_spec(F32), row_spec(BF16)],
        out_shape=[SDS((bx, t, d), F32), SDS((bx, t, d), BF16)],
        compiler_params=_cp("parallel", "parallel"),
        name="resid_proj_norm",
    )(a, w, x, mod, nw, mod, mod)


def _resid_proj(a, w, l, x, mod, gate_blk, row):
    bx, t, k = a.shape
    d = w.shape[2]
    tm, tn = _tile(t, 256), d
    nj = d // tn
    return pl.pallas_call(
        functools.partial(_resid_kernel, row=row),
        grid=(bx, t // tm, nj),
        in_specs=[
            pl.BlockSpec((1, tm, k), lambda b, i, j: (b, i, 0)),
            pl.BlockSpec((1, k, tn), lambda b, i, j: (l, 0, j), pipeline_mode=pl.Buffered(1)),
            pl.BlockSpec((1, tm, tn), lambda b, i, j: (b, i, j)),
            pl.BlockSpec((MOD_ROWS, tn), lambda b, i, j: (0, gate_blk * nj + j)),
        ],
        out_specs=pl.BlockSpec((1, tm, tn), lambda b, i, j: (b, i, j)),
        out_shape=SDS((bx, t, d), F32),
        compiler_params=_cp("parallel", "parallel", "arbitrary"),
        name="resid_proj",
    )(a, w, x, mod)


def _ffn_up_kernel(h_ref, w1_ref, w3_ref, o_ref):
    h = h_ref[0]
    w1, w3 = w1_ref[0].astype(BF16), w3_ref[0].astype(BF16)
    o_ref[0] = (_silu(_dot(h, w1)) * _dot(h, w3)).astype(o_ref.dtype)


def _ffn_up(h, w1, w3, l):
    bx, t, d = h.shape
    f = w1.shape[2]
    tm, tn = _tile(t, FFN_UP_ROWS), _tile(f, 512, LANES)
    return pl.pallas_call(
        _ffn_up_kernel,
        grid=(bx, t // tm, f // tn),
        in_specs=[
            pl.BlockSpec((1, tm, d), lambda b, i, j: (b, i, 0)),
            pl.BlockSpec((1, d, tn), lambda b, i, j: (l, 0, j)),
            pl.BlockSpec((1, d, tn), lambda b, i, j: (l, 0, j)),
        ],
        out_specs=pl.BlockSpec((1, tm, tn), lambda b, i, j: (b, i, j)),
        out_shape=SDS((bx, t, f), BF16),
        compiler_params=_cp("parallel", "parallel", "arbitrary"),
        name="ffn_up",
    )(h, w1, w3)


def _mixers(pm, pab, pm_c, pab_c, p, need_ctx):
    b = pm.shape[0]
    y_na = _neighbourhood_attention(pm, pm_c, p["qn"], p["kn"], p["rpb"])
    zero = jnp.zeros((b, 2 * N_HEADS, HEAD_DIM, HEAD_DIM), F32)
    of_c, ob_c, s_ctx = _deltanet(_dn_conv(pm_c, p["conv"]), pab_c, p["a_log"], p["dt_bias"], zero)
    of, ob, _ = _deltanet(_dn_conv(pm, p["conv"]), pab, p["a_log"], p["dt_bias"], s_ctx)
    ys = (y_na, _dn_out(of, ob, pm, p["onorm"]), _fourier(pm), _spatial_gating(pm, p["sg_w"], p["sg_bt"], p["sg_vn"]))
    if not need_ctx:
        return ys, None
    ys_c = (_dense_attention(pm_c, p["qn"], p["kn"]), _dn_out(of_c, ob_c, pm_c, p["onorm"]), _fourier(pm_c),
            _spatial_gating(pm_c, p["sg_w"], p["sg_bt"], p["sg_vn"]))
    return ys, ys_c


def kernel(x, c, ctx, c_ctx, w_ada, b_ada, norm1_w, norm2_w, w_in, na_qnorm, na_knorm, na_rpb, dn_conv, dn_a_log, dn_dt_bias, dn_onorm, sg_w, sg_b, sg_vnorm, w_gate, b_gate, w_branch, w_out, ffn_w1, ffn_w3, ffn_w2):
    bsz, seq, d = x.shape
    ctx_len = ctx.shape[1]
    depth = w_ada.shape[0]
    assert bsz + 1 <= MOD_ROWS and d % LANES == 0
    ctx_row = bsz

    cc = jnp.zeros((MOD_ROWS, d), F32).at[:bsz].set(c).at[ctx_row].set(c_ctx)
    mod_all = _ada_mod(cc, w_ada, b_ada)
    xc = ctx.reshape(1, bsz * ctx_len, d)

    w_in_b = w_in.astype(BF16)
    w_main = jnp.concatenate([w_in_b[..., :C_AB_SRC], w_in_b[..., C_AB_SRC + N_AB:]], axis=-1)
    w_ab = jnp.pad(w_in_b[..., C_AB_SRC:C_AB_SRC + N_AB], ((0, 0), (0, 0), (0, LANES - N_AB)))
    wg, wb, wo = w_gate.astype(BF16), w_branch.astype(BF16), w_out.astype(BF16)
    w1, w3, w2 = ffn_w1, ffn_w3, ffn_w2.astype(BF16)
    bg = b_gate.reshape(depth, N_BRANCH, 1, d)

    for l in range(depth):
        need_ctx = l < depth - 1
        mod = mod_all[l]
        p = dict(qn=na_qnorm[l][None], kn=na_knorm[l][None], rpb=na_rpb[l], conv=dn_conv[l], a_log=dn_a_log[l],
                 dt_bias=dn_dt_bias[l], onorm=dn_onorm[l][None], sg_w=sg_w[l].astype(BF16), sg_bt=sg_b[l].T,
                 sg_vn=sg_vnorm[l][None])
        n1w, n2w = norm1_w[l][None], norm2_w[l][None]

        pm, pab, h = _in_proj(x, n1w, mod, w_main, w_ab, l, None)
        pm_c, pab_c, h_c = _in_proj(xc, n1w, mod, w_main, w_ab, l, ctx_row)
        pm_c = pm_c.reshape(bsz, ctx_len, N_MAIN)
        pab_c = pab_c.reshape(bsz, ctx_len, LANES)
        ys, ys_c = _mixers(pm, pab, pm_c, pab_c, p, need_ctx)

        merged = _merge(h, ys, wg, bg, wb, l)
        x, h2 = _resid_proj_norm(merged, wo, l, x, mod, 2, n2w, 3, None)
        x = _resid_proj(_ffn_up(h2, w1, w3, l), w2, l, x, mod, 5, None)
        if need_ctx:
            ys_c = tuple(y.reshape(1, bsz * ctx_len, W_BR) for y in ys_c)
            merged_c = _merge(h_c, ys_c, wg, bg, wb, l)
            xc, h2_c = _resid_proj_norm(merged_c, wo, l, xc, mod, 2, n2w, 3, ctx_row)
            xc = _resid_proj(_ffn_up(h2_c, w1, w3, l), w2, l, xc, mod, 5, ctx_row)
    return x
```

```python
import functools

import numpy as np
import jax
import jax.numpy as jnp
from jax import lax
from jax.experimental import pallas as pl
from jax.experimental.pallas import tpu as pltpu

F32 = jnp.float32
BF16 = jnp.bfloat16
SDS = jax.ShapeDtypeStruct

LANES = 128
HEAD_DIM = 128
N_HEADS = 4
W_BR = N_HEADS * HEAD_DIM
N_BRANCH = 4
GRID_W = 64
NA_WIN_R = 8
NA_WIN_C = 16
NA_ROWS_PER_STEP = 16
DN_CHUNK = 64
DN_BATCH_PER_STEP = 4
DN_INV_BASE = 8
DN_CONV = 5
DN_CONV_PAD = 8
DN_CONV_WIDE_TILE_ROWS = 1024
SG_CHUNK = 128
FT_UNROLL = 16
FT_PAD = 4
N_MOD = 6
EPS = 1e-6
NEG_INF = -1e30
MOD_ROWS = 8
N_AB = 4 * N_HEADS
VMEM_LIMIT_BYTES = 56 * 1024 * 1024
RESIDENT_WEIGHT_BYTES = 8 * 1024 * 1024
FFN_UP_ROWS = 2048

C_NA_Q, C_NA_K, C_NA_V = 0, W_BR, 2 * W_BR
C_DN_QKV = 3 * W_BR
C_DN_Z = 6 * W_BR
C_FT = 7 * W_BR
C_SG_U = 8 * W_BR
C_SG_V = 9 * W_BR
N_MAIN = 10 * W_BR
C_AB_SRC = 7 * W_BR


def _cp(*sem):
    return pltpu.CompilerParams(dimension_semantics=sem, vmem_limit_bytes=VMEM_LIMIT_BYTES)


def _tile(n, pref, mult=8):
    if n <= pref:
        return n
    for t in range(pref - pref % mult, 0, -mult):
        if n % t == 0:
            return t
    return n


def _dot(a, b):
    return jnp.dot(a, b, preferred_element_type=F32)


def _dot_nt(a, b):
    return lax.dot_general(a, b, (((1,), (1,)), ((), ())), preferred_element_type=F32)


def _dot_tn(a, b):
    return lax.dot_general(a, b, (((0,), (0,)), ((), ())), preferred_element_type=F32)


def _silu(x):
    return x * jax.nn.sigmoid(x)


def _rms(x, w):
    return x * lax.rsqrt(jnp.mean(x * x, axis=-1, keepdims=True) + EPS) * w


def _mod_row(ref, row):
    b = pl.program_id(0) if row is None else row
    return ref[pl.ds(b, 1), :]


def _norm_mod(x, nw_ref, sh_ref, sc_ref, row):
    return (_rms(x, nw_ref[...]) * (1.0 + _mod_row(sc_ref, row)) + _mod_row(sh_ref, row)).astype(BF16)


def _ada_kernel(c_ref, w_ref, b_ref, o_ref):
    s = _silu(c_ref[...]).astype(BF16)
    o_ref[0] = _dot(s, w_ref[0].astype(BF16)) + b_ref[0]


def _ada_mod(cc, w_ada, b_ada):
    depth, d, n = w_ada.shape
    tn = _tile(n, 1024, LANES)
    return pl.pallas_call(
        _ada_kernel,
        grid=(depth, n // tn),
        in_specs=[
            pl.BlockSpec((MOD_ROWS, d), lambda l, j: (0, 0)),
            pl.BlockSpec((1, d, tn), lambda l, j: (l, 0, j)),
            pl.BlockSpec((1, 1, tn), lambda l, j: (l, 0, j)),
        ],
        out_specs=pl.BlockSpec((1, MOD_ROWS, tn), lambda l, j: (l, 0, j)),
        out_shape=SDS((depth, MOD_ROWS, n), F32),
        compiler_params=_cp("parallel", "parallel"),
        name="ada_mod",
    )(cc, w_ada, b_ada.reshape(depth, 1, n))


def _in_proj_kernel(x_ref, nw_ref, sh_ref, sc_ref, w_ref, wab_ref, o_ref, oab_ref, h_ref, hs_ref, *, row):
    @pl.when(pl.program_id(2) == 0)
    def _():
        h = _norm_mod(x_ref[0], nw_ref, sh_ref, sc_ref, row)
        hs_ref[...] = h
        h_ref[0] = h
        oab_ref[0] = _dot(h, wab_ref[0])

    o_ref[0] = _dot(hs_ref[...], w_ref[0]).astype(o_ref.dtype)


def _in_proj(x, nw, mod, w_main, w_ab, l, row):
    bx, t, d = x.shape
    n = w_main.shape[2]
    tm, tn = _tile(t, 1024), _tile(n, 1280, LANES)
    return pl.pallas_call(
        functools.partial(_in_proj_kernel, row=row),
        grid=(bx, t // tm, n // tn),
        in_specs=[
            pl.BlockSpec((1, tm, d), lambda b, i, j: (b, i, 0)),
            pl.BlockSpec((1, d), lambda b, i, j: (0, 0)),
            pl.BlockSpec((MOD_ROWS, d), lambda b, i, j: (0, 0)),
            pl.BlockSpec((MOD_ROWS, d), lambda b, i, j: (0, 1)),
            pl.BlockSpec((1, d, tn), lambda b, i, j: (l, 0, j)),
            pl.BlockSpec((1, d, LANES), lambda b, i, j: (l, 0, 0)),
        ],
        out_specs=[
            pl.BlockSpec((1, tm, tn), lambda b, i, j: (b, i, j)),
            pl.BlockSpec((1, tm, LANES), lambda b, i, j: (b, i, 0)),
            pl.BlockSpec((1, tm, d), lambda b, i, j: (b, i, 0)),
        ],
        out_shape=[SDS((bx, t, n), BF16), SDS((bx, t, LANES), F32), SDS((bx, t, d), BF16)],
        scratch_shapes=[pltpu.VMEM((tm, d), BF16)],
        compiler_params=_cp("parallel", "parallel", "arbitrary"),
        name="in_proj",
    )(x, nw, mod, mod, w_main, w_ab)


def _na_kernel(q_ref, k_ref, v_ref, kc_ref, vc_ref, qn_ref, kn_ref, bias_ref, o_ref,
               qs, ks, vs, kcs, vcs, *, rows):
    averaging = jnp.full((HEAD_DIM, HEAD_DIM), 1.0 / HEAD_DIM, BF16)

    def rms(x, w):
        return x * lax.rsqrt(_dot((x * x).astype(BF16), averaging) + EPS) * w

    qs[...] = (rms(q_ref[0].astype(F32), qn_ref[...]) * HEAD_DIM ** -0.5).astype(BF16)
    ks[...] = rms(k_ref[0].astype(F32), kn_ref[...]).astype(BF16)
    vs[...] = v_ref[0].astype(BF16)
    kcs[...] = rms(kc_ref[0].astype(F32), kn_ref[...]).astype(BF16)
    vcs[...] = vc_ref[0].astype(BF16)
    band = NA_WIN_R * GRID_W
    nr = NA_ROWS_PER_STEP if rows % NA_ROWS_PER_STEP == 0 else NA_WIN_R

    def body(g, carry):
        rr = [g * nr + i for i in range(nr)]
        start = [jnp.clip(r - NA_WIN_R // 2, 0, rows - NA_WIN_R) for r in rr]
        r0 = [pl.multiple_of(r * GRID_W, GRID_W) for r in rr]
        k0 = [pl.multiple_of(s * GRID_W, GRID_W) for s in start]
        q = [qs[pl.ds(x, GRID_W), :] for x in r0]
        kb = [ks[pl.ds(x, band), :] for x in k0]
        vb = [vs[pl.ds(x, band), :] for x in k0]
        bias = [bias_ref[0, r - s] for r, s in zip(rr, start)]
        kc, vc = kcs[...], vcs[...]
        s_loc = [_dot_nt(q[i], kb[i]) + bias[i] for i in range(nr)]
        s_ctx = [_dot_nt(q[i], kc) for i in range(nr)]
        m = [jnp.maximum(jnp.max(s_loc[i], axis=-1, keepdims=True), jnp.max(s_ctx[i], axis=-1, keepdims=True))
             for i in range(nr)]
        p = [jnp.exp(s_loc[i] - m[i]) for i in range(nr)]
        pc = [jnp.exp(s_ctx[i] - m[i]) for i in range(nr)]
        denom = [jnp.sum(p[i], axis=-1, keepdims=True) + jnp.sum(pc[i], axis=-1, keepdims=True) for i in range(nr)]
        o = [_dot(p[i].astype(BF16), vb[i]) + _dot(pc[i].astype(BF16), vc) for i in range(nr)]
        for i in range(nr):
            o_ref[0, pl.ds(r0[i], GRID_W), :] = (o[i] / denom[i]).astype(o_ref.dtype)
        return carry

    lax.fori_loop(0, rows // nr, body, 0)


def _na_bias_table(rpb):
    col = np.arange(GRID_W)
    dc = np.clip(col[None, :] - col[:, None], 1 - NA_WIN_C, NA_WIN_C - 1) + NA_WIN_C - 1
    col_start = np.clip(col - NA_WIN_C // 2, 0, GRID_W - NA_WIN_C)
    in_win = (col[None, :] >= col_start[:, None]) & (col[None, :] < col_start[:, None] + NA_WIN_C)
    rows = jnp.stack([rpb[:, NA_WIN_R - 1 - typ:2 * NA_WIN_R - 1 - typ, :] for typ in range(NA_WIN_R)], axis=1)
    onehot = (dc[None] == np.arange(2 * NA_WIN_C - 1)[:, None, None]).astype(np.float32)
    b = jnp.einsum('htjc,cqk->htqjk', rows.astype(F32), onehot, precision=lax.Precision.HIGHEST)
    b = jnp.where(in_win[None, None, :, None, :], b, NEG_INF)
    return b.reshape(rpb.shape[0], NA_WIN_R, GRID_W, NA_WIN_R * GRID_W)


def _neighbourhood_attention(pm, pm_c, qn, kn, rpb):
    b, t, _ = pm.shape
    ctx_len = pm_c.shape[1]
    rows = t // GRID_W
    assert rows >= NA_WIN_R and rows * GRID_W == t and rows % NA_WIN_R == 0
    band = NA_WIN_R * GRID_W
    hb = lambda off: (lambda bi, h: (bi, 0, off // HEAD_DIM + h))
    return pl.pallas_call(
        functools.partial(_na_kernel, rows=rows),
        grid=(b, N_HEADS),
        in_specs=[
            pl.BlockSpec((1, t, HEAD_DIM), hb(C_NA_Q)),
            pl.BlockSpec((1, t, HEAD_DIM), hb(C_NA_K)),
            pl.BlockSpec((1, t, HEAD_DIM), hb(C_NA_V)),
            pl.BlockSpec((1, ctx_len, HEAD_DIM), hb(C_NA_K)),
            pl.BlockSpec((1, ctx_len, HEAD_DIM), hb(C_NA_V)),
            pl.BlockSpec((1, HEAD_DIM), lambda bi, h: (0, 0)),
            pl.BlockSpec((1, HEAD_DIM), lambda bi, h: (0, 0)),
            pl.BlockSpec((1, NA_WIN_R, GRID_W, band), lambda bi, h: (h, 0, 0, 0)),
        ],
        out_specs=pl.BlockSpec((1, t, HEAD_DIM), lambda bi, h: (bi, 0, h)),
        out_shape=SDS((b, t, W_BR), BF16),
        scratch_shapes=[pltpu.VMEM((t, HEAD_DIM), BF16)] * 3 + [pltpu.VMEM((ctx_len, HEAD_DIM), BF16)] * 2,
        compiler_params=_cp("parallel", "parallel"),
        name="natten",
    )(pm, pm, pm, pm_c, pm_c, qn, kn, _na_bias_table(rpb))


def _ctx_attn_kernel(q_ref, k_ref, v_ref, qn_ref, kn_ref, o_ref):
    q = _rms(q_ref[0].astype(F32), qn_ref[...]).astype(BF16)
    k = _rms(k_ref[0].astype(F32), kn_ref[...]).astype(BF16)
    s = _dot_nt(q, k) * HEAD_DIM ** -0.5
    p = jnp.exp(s - jnp.max(s, axis=-1, keepdims=True))
    o = _dot(p.astype(BF16), v_ref[0].astype(BF16)) / jnp.sum(p, axis=-1, keepdims=True)
    o_ref[0] = o.astype(o_ref.dtype)


def _dense_attention(pm_c, qn, kn):
    b, ctx_len, _ = pm_c.shape
    hb = lambda off: (lambda bi, h: (bi, 0, off // HEAD_DIM + h))
    return pl.pallas_call(
        _ctx_attn_kernel,
        grid=(b, N_HEADS),
        in_specs=[
            pl.BlockSpec((1, ctx_len, HEAD_DIM), hb(C_NA_Q)),
            pl.BlockSpec((1, ctx_len, HEAD_DIM), hb(C_NA_K)),
            pl.BlockSpec((1, ctx_len, HEAD_DIM), hb(C_NA_V)),
            pl.BlockSpec((1, HEAD_DIM), lambda bi, h: (0, 0)),
            pl.BlockSpec((1, HEAD_DIM), lambda bi, h: (0, 0)),
        ],
        out_specs=pl.BlockSpec((1, ctx_len, HEAD_DIM), lambda bi, h: (bi, 0, h)),
        out_shape=SDS((b, ctx_len, W_BR), BF16),
        compiler_params=_cp("parallel", "parallel"),
        name="ctx_attn",
    )(pm_c, pm_c, pm_c, qn, kn)


def _dn_conv_kernel(x_ref, w_ref, o_ref, xp_ref, *, n_norm_tiles):
    x = x_ref[0].astype(F32)
    t = x.shape[0]
    border = jnp.zeros((DN_CONV_PAD, x.shape[1]), F32)
    xp_ref[0:DN_CONV_PAD, :] = border
    xp_ref[DN_CONV_PAD + t:, :] = border
    xp_ref[DN_CONV_PAD:DN_CONV_PAD + t, :] = x
    half = DN_CONV // 2
    acc = x * w_ref[half:half + 1, :]
    for kk in range(DN_CONV):
        d = kk - half
        if d != 0:
            acc = acc + xp_ref[DN_CONV_PAD + d:DN_CONV_PAD + d + t, :] * w_ref[kk:kk + 1, :]
    y = _silu(acc)
    normed = pl.program_id(1) < n_norm_tiles
    for h in range(y.shape[1] // HEAD_DIM):
        sl = slice(h * HEAD_DIM, (h + 1) * HEAD_DIM)
        yh = y[:, sl]
        inv_norm = lax.rsqrt(jnp.sum(yh * yh, axis=-1, keepdims=True) + EPS)
        o_ref[0, :, sl] = yh * jnp.where(normed, inv_norm, 1.0)


def _dn_conv(pm, conv_w):
    b, t, _ = pm.shape
    tw = W_BR if t <= DN_CONV_WIDE_TILE_ROWS else HEAD_DIM
    return pl.pallas_call(
        functools.partial(_dn_conv_kernel, n_norm_tiles=2 * W_BR // tw),
        grid=(b, 3 * W_BR // tw),
        in_specs=[
            pl.BlockSpec((1, t, tw), lambda bi, j: (bi, 0, C_DN_QKV // tw + j)),
            pl.BlockSpec((DN_CONV, tw), lambda bi, j: (0, j)),
        ],
        out_specs=pl.BlockSpec((1, t, tw), lambda bi, j: (bi, 0, j)),
        out_shape=SDS((b, t, 3 * W_BR), F32),
        scratch_shapes=[pltpu.VMEM((t + 2 * DN_CONV_PAD, tw), F32)],
        compiler_params=_cp("parallel", "parallel"),
        name="dn_conv",
    )(pm, conv_w)


def _softplus(x):
    return jnp.maximum(x, 0.0) + jnp.log1p(jnp.exp(-jnp.abs(x)))


def _block_masks(n):
    ii = lax.broadcasted_iota(jnp.int32, (n, n), 0)
    jj = lax.broadcasted_iota(jnp.int32, (n, n), 1)
    same = lambda s: (ii // s) == (jj // s)
    masks, s = [same(DN_INV_BASE)], DN_INV_BASE
    while s < n:
        masks.append(same(2 * s) & jnp.logical_not(same(s)))
        s *= 2
    return masks


def _unit_triangular_inverses(a_list, eye, masks):
    mm = lambda x, y: _dot(x.astype(BF16), y.astype(BF16))
    n = len(a_list)
    pw = [jnp.where(masks[0], a, 0.0) for a in a_list]
    inv = [eye - p for p in pw]
    s = 2
    while s < DN_INV_BASE:
        pw = [mm(p, p) for p in pw]
        inv = [inv[i] + mm(inv[i], pw[i]) for i in range(n)]
        s *= 2
    for m in masks[1:]:
        t = [mm(jnp.where(m, a_list[i], 0.0), inv[i]) for i in range(n)]
        inv = [inv[i] - mm(inv[i], t[i]) for i in range(n)]
    return inv


def _dn_kernel(qf_ref, kf_ref, vf_ref, qb_ref, kb_ref, vb_ref, abf_ref, abb_ref, abtf_ref, abtb_ref,
               prow_ref, pcol_ref, s0_ref, of_ref, ob_ref, sfin_ref, s_ref):
    c = pl.program_id(1)

    @pl.when(c == 0)
    def _():
        s_ref[...] = s0_ref[...].reshape(s_ref.shape)

    cs = DN_CHUNK
    ii = lax.broadcasted_iota(jnp.int32, (cs, cs), 0)
    jj = lax.broadcasted_iota(jnp.int32, (cs, cs), 1)
    eye = (ii == jj).astype(F32)
    blk_masks = _block_masks(cs)
    scale = HEAD_DIM ** -0.5
    hi = lax.Precision.HIGHEST

    nb = qf_ref.shape[0]
    nst = 2 * N_HEADS
    chains = [(bb, d, h) for bb in range(nb) for d in range(2) for h in range(N_HEADS)]
    nch = len(chains)
    incl = [(jj <= ii), (jj >= ii)]
    strict = [(jj < ii), (jj > ii)]
    gc_col, gc_row, beta_all = {}, {}, {}
    for bb in range(nb):
        for d in range(2):
            tri = incl[d].astype(F32)
            ab = (abf_ref if d == 0 else abb_ref)[bb]
            abt = (abtf_ref if d == 0 else abtb_ref)[bb, 0]
            g_col = -jnp.exp(prow_ref[0:1, :]) * _softplus(ab + prow_ref[1:2, :])
            g_row = -jnp.exp(pcol_ref[:, 0:1]) * _softplus(abt + pcol_ref[:, 1:2])
            gc_col[bb, d] = jnp.dot(tri, g_col, preferred_element_type=F32, precision=hi)
            gc_row[bb, d] = lax.dot_general(g_row, tri, (((1,), (1,)), ((), ())),
                                            preferred_element_type=F32, precision=hi)
            beta_all[bb, d] = jax.nn.sigmoid(ab)
    qkv_refs = ((qf_ref, kf_ref, vf_ref), (qb_ref, kb_ref, vb_ref))
    sl = [slice(h * HEAD_DIM, (h + 1) * HEAD_DIM) for h in range(N_HEADS)]
    col = [d * N_HEADS + h for bb, d, h in chains]
    q = [qkv_refs[d][0][bb, :, sl[h]] * scale for bb, d, h in chains]
    k = [qkv_refs[d][1][bb, :, sl[h]] for bb, d, h in chains]
    v = [qkv_refs[d][2][bb, :, sl[h]] for bb, d, h in chains]
    s_old = [s_ref[bb * nst + col[j]] for j, (bb, d, h) in enumerate(chains)]
    gcc = [gc_col[bb, d][:, col[j]:col[j] + 1] for j, (bb, d, h) in enumerate(chains)]
    gcr = [gc_row[bb, d][col[j]:col[j] + 1, :] for j, (bb, d, h) in enumerate(chains)]
    beta = [beta_all[bb, d][:, nst + col[j]:nst + col[j] + 1] for j, (bb, d, h) in enumerate(chains)]
    decay = [jnp.exp(jnp.where(incl[d], gcc[j] - gcr[j], -jnp.inf)) for j, (bb, d, h) in enumerate(chains)]
    eg = [jnp.exp(x) for x in gcc]
    g_last = [gcc[j][cs - 1:cs, :] if d == 0 else gcc[j][0:1, :] for j, (bb, d, h) in enumerate(chains)]
    kbf = [x.astype(BF16) for x in k]
    kbeta = [k[j] * beta[j] for j in range(nch)]
    kq = [_dot_nt(jnp.concatenate([kbeta[j], q[j]], axis=0).astype(BF16), kbf[j]) for j in range(nch)]
    a_mat = [jnp.where(strict[d], kq[j][:cs] * decay[j], 0.0) for j, (bb, d, h) in enumerate(chains)]
    qk = [jnp.where(incl[d], kq[j][cs:] * decay[j], 0.0) for j, (bb, d, h) in enumerate(chains)]
    inv = _unit_triangular_inverses(a_mat, eye, blk_masks)
    rhs = [jnp.concatenate([v[j] * beta[j], kbeta[j] * eg[j]], axis=1).astype(BF16) for j in range(nch)]
    uw = [_dot(inv[j].astype(BF16), rhs[j]) for j in range(nch)]
    ws_qs = [_dot(jnp.concatenate([uw[j][:, HEAD_DIM:], q[j] * eg[j]], axis=0).astype(BF16), s_old[j].astype(BF16))
             for j in range(nch)]
    v_new = [uw[j][:, :HEAD_DIM] - ws_qs[j][:cs] for j in range(nch)]
    vnb = [x.astype(BF16) for x in v_new]
    o = [ws_qs[j][cs:] + _dot(qk[j].astype(BF16), vnb[j]) for j in range(nch)]
    kd = [(k[j] * jnp.exp(g_last[j] - gcc[j])).astype(BF16) for j in range(nch)]
    s_new = [s_old[j] * jnp.exp(g_last[j]) + _dot_tn(kd[j], vnb[j]) for j in range(nch)]
    for j, (bb, d, h) in enumerate(chains):
        s_ref[bb * nst + col[j]] = s_new[j]
        (of_ref if d == 0 else ob_ref)[bb, :, sl[h]] = o[j]

    @pl.when(c == pl.num_programs(1) - 1)
    def _():
        sfin_ref[...] = s_ref[...].reshape(sfin_ref.shape)


def _deltanet(act, pab, a_log, dt_bias, s0):
    b, t, _ = act.shape
    n = t // DN_CHUNK
    assert n * DN_CHUNK == t
    abt = jnp.swapaxes(pab[..., :N_AB].reshape(b, n, DN_CHUNK, N_AB), 2, 3)
    al, dt = a_log.reshape(-1).astype(F32), dt_bias.reshape(-1).astype(F32)
    prow = jnp.zeros((2, LANES), F32).at[0, :al.size].set(al).at[1, :dt.size].set(dt)
    pcol = jnp.zeros((N_AB, 2), F32).at[:al.size, 0].set(al).at[:dt.size, 1].set(dt)
    nb = DN_BATCH_PER_STEP if b % DN_BATCH_PER_STEP == 0 else 1
    fwd = lambda part: (lambda bi, c: (bi, c, part))
    bwd = lambda part: (lambda bi, c: (bi, n - 1 - c, part))
    qkv_spec = lambda imap: pl.BlockSpec((nb, DN_CHUNK, W_BR), imap)
    n_state = 2 * N_HEADS
    state_spec = pl.BlockSpec((nb, n_state, HEAD_DIM, HEAD_DIM), lambda bi, c: (bi, 0, 0, 0))
    return pl.pallas_call(
        _dn_kernel,
        grid=(b // nb, n),
        in_specs=[
            qkv_spec(fwd(0)), qkv_spec(fwd(1)), qkv_spec(fwd(2)),
            qkv_spec(bwd(0)), qkv_spec(bwd(1)), qkv_spec(bwd(2)),
            pl.BlockSpec((nb, DN_CHUNK, LANES), fwd(0)),
            pl.BlockSpec((nb, DN_CHUNK, LANES), bwd(0)),
            pl.BlockSpec((nb, 1, N_AB, DN_CHUNK), lambda bi, c: (bi, c, 0, 0)),
            pl.BlockSpec((nb, 1, N_AB, DN_CHUNK), lambda bi, c: (bi, n - 1 - c, 0, 0)),
            pl.BlockSpec((2, LANES), lambda bi, c: (0, 0)),
            pl.BlockSpec((N_AB, 2), lambda bi, c: (0, 0)),
            state_spec,
        ],
        out_specs=[
            pl.BlockSpec((nb, DN_CHUNK, W_BR), fwd(0)),
            pl.BlockSpec((nb, DN_CHUNK, W_BR), bwd(0)),
            state_spec,
        ],
        out_shape=[SDS((b, t, W_BR), F32), SDS((b, t, W_BR), F32),
                   SDS((b, n_state, HEAD_DIM, HEAD_DIM), F32)],
        scratch_shapes=[pltpu.VMEM((nb * n_state, HEAD_DIM, HEAD_DIM), F32)],
        compiler_params=_cp("parallel", "arbitrary"),
        name="deltanet",
    )(act, act, act, act, act, act, pab, pab, abt, abt, prow, pcol, s0)


def _dn_out_kernel(of_ref, ob_ref, z_ref, w_ref, y_ref):
    o = of_ref[0] + ob_ref[0]
    z = z_ref[0].astype(F32)
    for h in range(N_HEADS):
        sl = slice(h * HEAD_DIM, (h + 1) * HEAD_DIM)
        y_ref[0, :, sl] = (_rms(o[:, sl], w_ref[...]) * _silu(z[:, sl])).astype(y_ref.dtype)


def _dn_out(o_f, o_b, pm, onorm):
    b, t, _ = o_f.shape
    tm = _tile(t, 1024)
    return pl.pallas_call(
        _dn_out_kernel,
        grid=(b, t // tm),
        in_specs=[
            pl.BlockSpec((1, tm, W_BR), lambda bi, i: (bi, i, 0)),
            pl.BlockSpec((1, tm, W_BR), lambda bi, i: (bi, i, 0)),
            pl.BlockSpec((1, tm, W_BR), lambda bi, i: (bi, i, C_DN_Z // W_BR)),
            pl.BlockSpec((1, HEAD_DIM), lambda bi, i: (0, 0)),
        ],
        out_specs=pl.BlockSpec((1, tm, W_BR), lambda bi, i: (bi, i, 0)),
        out_shape=SDS((b, t, W_BR), BF16),
        compiler_params=_cp("parallel", "parallel"),
        name="dn_out",
    )(o_f, o_b, pm, onorm)


def _fourier_kernel(x_ref, m1_ref, m2_ref, m3_ref, twc_ref, tws_ref, o_ref, xf_ref, bre_ref, bim_ref, zo_ref,
                    *, n1, n2, scale):
    m1, m2, m3 = m1_ref[...].astype(BF16), m2_ref[...].astype(BF16), m3_ref[...].astype(BF16)
    nu = FT_UNROLL
    p2, p1 = n2 + FT_PAD, n1 + FT_PAD
    for t1 in range(n1):
        xf_ref[t1 * p2:t1 * p2 + n2, :] = x_ref[0, t1 * n2:(t1 + 1) * n2, :].astype(F32)

    def stage1(i, carry):
        t2 = [i * nu + u for u in range(nu)]
        xs = [xf_ref[pl.ds(t, n1, stride=p2), :].astype(BF16) for t in t2]
        a = [_dot(m1, x) for x in xs]
        cw = [twc_ref[t] for t in t2]
        sw = [tws_ref[t] for t in t2]
        bre = [a[u][:n1] * cw[u] - a[u][n1:] * sw[u] for u in range(nu)]
        bim = [-(a[u][n1:] * cw[u] + a[u][:n1] * sw[u]) for u in range(nu)]
        for u in range(nu):
            bre_ref[pl.ds(t2[u], n1, stride=p2), :] = bre[u]
            bim_ref[pl.ds(t2[u], n1, stride=p2), :] = bim[u]
        return carry

    lax.fori_loop(0, n2 // nu, stage1, 0)

    def stage2(i, carry):
        f1 = [i * nu + u for u in range(nu)]
        r0 = [pl.multiple_of(f * p2, FT_PAD) for f in f1]
        b = [jnp.concatenate([bre_ref[pl.ds(r, n2), :], bim_ref[pl.ds(r, n2), :]], axis=0).astype(BF16) for r in r0]
        y = [_dot(m2, x) for x in b]
        z = [_dot(jnp.concatenate([yy[:n2], yy[n2:]], axis=1).astype(BF16), m3) for yy in y]
        for u in range(nu):
            zo_ref[pl.ds(f1[u], n2, stride=p1), :] = z[u] * scale
        return carry

    lax.fori_loop(0, n1 // nu, stage2, 0)
    for f2 in range(n2):
        o_ref[0, f2 * n1:(f2 + 1) * n1, :] = zo_ref[f2 * p1:f2 * p1 + n1, :].astype(o_ref.dtype)


def _dft_mats(n):
    ang = 2.0 * np.pi * ((np.arange(n)[:, None] * np.arange(n)[None, :]) % n) / n
    return np.cos(ang), np.sin(ang)


def _fourier(pm):
    b, t, _ = pm.shape
    n1 = 1 << ((t.bit_length() - 1 + 1) // 2)
    n2 = t // n1
    assert n1 * n2 == t and n1 % FT_UNROLL == 0 and n2 % FT_UNROLL == 0
    c1, s1 = _dft_mats(n1)
    c2, s2 = _dft_mats(n2)
    cc, sc = _dft_mats(HEAD_DIM)
    m1 = jnp.asarray(np.concatenate([c1, s1], axis=0), F32)
    m2 = jnp.asarray(np.block([[c2, s2], [-s2, c2]]), F32)
    m3 = jnp.asarray(np.concatenate([cc, sc], axis=0), F32)
    ang = 2.0 * np.pi * ((np.arange(n2)[:, None] * np.arange(n1)[None, :]) % t) / t
    twc = jnp.broadcast_to(jnp.asarray(np.cos(ang), F32)[:, :, None], (n2, n1, HEAD_DIM))
    tws = jnp.broadcast_to(jnp.asarray(np.sin(ang), F32)[:, :, None], (n2, n1, HEAD_DIM))
    const = lambda shape: pl.BlockSpec(shape, lambda bi, g: (0,) * len(shape))
    return pl.pallas_call(
        functools.partial(_fourier_kernel, n1=n1, n2=n2, scale=float((t * HEAD_DIM) ** -0.5)),
        grid=(b, N_HEADS),
        in_specs=[
            pl.BlockSpec((1, t, HEAD_DIM), lambda bi, g: (bi, 0, C_FT // HEAD_DIM + g)),
            const(m1.shape), const(m2.shape), const(m3.shape),
            const((n2, n1, HEAD_DIM)), const((n2, n1, HEAD_DIM)),
        ],
        out_specs=pl.BlockSpec((1, t, HEAD_DIM), lambda bi, g: (bi, 0, g)),
        out_shape=SDS((b, t, W_BR), BF16),
        scratch_shapes=[pltpu.VMEM((n1 * (n2 + FT_PAD), HEAD_DIM), F32)] * 3
        + [pltpu.VMEM((n2 * (n1 + FT_PAD), HEAD_DIM), F32)],
        compiler_params=_cp("parallel", "parallel"),
        name="fourier",
    )(pm, m1, m2, m3, twc, tws)


def _sgu_kernel(u_ref, v_ref, vn_ref, ws_ref, bs_ref, o_ref):
    u = jax.nn.gelu(u_ref[0].astype(F32))
    v = jax.nn.gelu(v_ref[0].astype(F32))
    mu = jnp.mean(v, axis=-1, keepdims=True)
    var = jnp.mean(jnp.square(v - mu), axis=-1, keepdims=True)
    vb = ((v - mu) * lax.rsqrt(var + EPS) * vn_ref[...]).astype(BF16)
    out = []
    for ch in range(u.shape[0] // SG_CHUNK):
        rs = slice(ch * SG_CHUNK, (ch + 1) * SG_CHUNK)
        mix = [_dot(ws_ref[g], vb[rs, g * HEAD_DIM:(g + 1) * HEAD_DIM]) + bs_ref[:, g:g + 1] for g in range(N_HEADS)]
        out.append((u[rs, :] * jnp.concatenate(mix, axis=1)).astype(o_ref.dtype))
    for ch, y in enumerate(out):
        o_ref[0, ch * SG_CHUNK:(ch + 1) * SG_CHUNK, :] = y


def _spatial_gating(pm, ws, bs_t, vnorm):
    b, t, _ = pm.shape
    tm = _tile(t, 4 * SG_CHUNK, SG_CHUNK)
    assert tm % SG_CHUNK == 0
    return pl.pallas_call(
        _sgu_kernel,
        grid=(b, t // tm),
        in_specs=[
            pl.BlockSpec((1, tm, W_BR), lambda bi, i: (bi, i, C_SG_U // W_BR)),
            pl.BlockSpec((1, tm, W_BR), lambda bi, i: (bi, i, C_SG_V // W_BR)),
            pl.BlockSpec((1, W_BR), lambda bi, i: (0, 0)),
            pl.BlockSpec((N_HEADS, SG_CHUNK, SG_CHUNK), lambda bi, i: (0, 0, 0)),
            pl.BlockSpec((SG_CHUNK, N_HEADS), lambda bi, i: (0, 0)),
        ],
        out_specs=pl.BlockSpec((1, tm, W_BR), lambda bi, i: (bi, i, 0)),
        out_shape=SDS((b, t, W_BR), BF16),
        compiler_params=_cp("parallel", "parallel"),
        name="sgu",
    )(pm, pm, vnorm, ws, bs_t)


def _merge_kernel(h_ref, y0_ref, y1_ref, y2_ref, y3_ref, wg_ref, bg_ref, wb_ref, o_ref):
    h = h_ref[0]
    acc = None
    for i, y_ref in enumerate((y0_ref, y1_ref, y2_ref, y3_ref)):
        gate = jax.nn.sigmoid(_dot(h, wg_ref[0, i]) + bg_ref[0, i])
        term = gate * _dot(y_ref[0].astype(BF16), wb_ref[0, i])
        acc = term if acc is None else acc + term
    o_ref[0] = acc.astype(o_ref.dtype)


def _merge(h, ys, wg, bg, wb, l):
    bx, t, d = h.shape
    tm, tn = _tile(t, 1024), _tile(d, 512, LANES)
    y_spec = pl.BlockSpec((1, tm, W_BR), lambda b, i, j: (b, i, 0))
    return pl.pallas_call(
        _merge_kernel,
        grid=(bx, t // tm, d // tn),
        in_specs=[
            pl.BlockSpec((1, tm, d), lambda b, i, j: (b, i, 0)),
            y_spec, y_spec, y_spec, y_spec,
            pl.BlockSpec((1, N_BRANCH, d, tn), lambda b, i, j: (l, 0, 0, j)),
            pl.BlockSpec((1, N_BRANCH, 1, tn), lambda b, i, j: (l, 0, 0, j)),
            pl.BlockSpec((1, N_BRANCH, W_BR, tn), lambda b, i, j: (l, 0, 0, j)),
        ],
        out_specs=pl.BlockSpec((1, tm, tn), lambda b, i, j: (b, i, j)),
        out_shape=SDS((bx, t, d), BF16),
        compiler_params=_cp("parallel", "parallel", "arbitrary"),
        name="merge",
    )(h, *ys, wg, bg, wb)


def _resid_kernel(a_ref, w_ref, x_ref, g_ref, o_ref, *, row):
    o_ref[0] = x_ref[0] + _mod_row(g_ref, row) * _dot(a_ref[0], w_ref[0])


def _resid_norm_kernel(a_ref, w_ref, x_ref, g_ref, nw_ref, sh_ref, sc_ref, o_ref, h_ref, *, row):
    x_new = x_ref[0] + _mod_row(g_ref, row) * _dot(a_ref[0], w_ref[0])
    o_ref[0] = x_new
    h_ref[0] = _norm_mod(x_new, nw_ref, sh_ref, sc_ref, row)


def _resid_proj_norm(a, w, l, x, mod, gate_blk, nw, norm_blk, row):
    bx, t, k = a.shape
    d = w.shape[2]
    assert k * d * 2 <= RESIDENT_WEIGHT_BYTES
    tm = _tile(t, 512)
    row_spec = lambda dt: pl.BlockSpec((1, tm, d), lambda b, i: (b, i, 0))
    return pl.pallas_call(
        functools.partial(_resid_norm_kernel, row=row),
        grid=(bx, t // tm),
        in_specs=[
            pl.BlockSpec((1, tm, k), lambda b, i: (b, i, 0)),
            pl.BlockSpec((1, k, d), lambda b, i: (l, 0, 0)),
            row_spec(F32),
            pl.BlockSpec((MOD_ROWS, d), lambda b, i: (0, gate_blk)),
            pl.BlockSpec((1, d), lambda b, i: (0, 0)),
            pl.BlockSpec((MOD_ROWS, d), lambda b, i: (0, norm_blk)),
            pl.BlockSpec((MOD_ROWS, d), lambda b, i: (0, norm_blk + 1)),
        ],
        out_specs=[row_spec(F32), row_spec(BF16)],
        out_shape=[SDS((bx, t, d), F32), SDS((bx, t, d), BF16)],
        compiler_params=_cp("parallel", "parallel"),
        name="resid_proj_norm",
    )(a, w, x, mod, nw, mod, mod)


def _resid_proj(a, w, l, x, mod, gate_blk, row):
    bx, t, k = a.shape
    d = w.shape[2]
    tm, tn = _tile(t, 256), d
    nj = d // tn
    return pl.pallas_call(
        functools.partial(_resid_kernel, row=row),
        grid=(bx, t // tm, nj),
        in_specs=[
            pl.BlockSpec((1, tm, k), lambda b, i, j: (b, i, 0)),
            pl.BlockSpec((1, k, tn), lambda b, i, j: (l, 0, j), pipeline_mode=pl.Buffered(1)),
            pl.BlockSpec((1, tm, tn), lambda b, i, j: (b, i, j)),
            pl.BlockSpec((MOD_ROWS, tn), lambda b, i, j: (0, gate_blk * nj + j)),
        ],
        out_specs=pl.BlockSpec((1, tm, tn), lambda b, i, j: (b, i, j)),
        out_shape=SDS((bx, t, d), F32),
        compiler_params=_cp("parallel", "parallel", "arbitrary"),
        name="resid_proj",
    )(a, w, x, mod)


def _ffn_up_kernel(h_ref, w1_ref, w3_ref, o_ref):
    h = h_ref[0]
    w1, w3 = w1_ref[0].astype(BF16), w3_ref[0].astype(BF16)
    o_ref[0] = (_silu(_dot(h, w1)) * _dot(h, w3)).astype(o_ref.dtype)


def _ffn_up(h, w1, w3, l):
    bx, t, d = h.shape
    f = w1.shape[2]
    tm, tn = _tile(t, FFN_UP_ROWS), _tile(f, 512, LANES)
    return pl.pallas_call(
        _ffn_up_kernel,
        grid=(bx, t // tm, f // tn),
        in_specs=[
            pl.BlockSpec((1, tm, d), lambda b, i, j: (b, i, 0)),
            pl.BlockSpec((1, d, tn), lambda b, i, j: (l, 0, j)),
            pl.BlockSpec((1, d, tn), lambda b, i, j: (l, 0, j)),
        ],
        out_specs=pl.BlockSpec((1, tm, tn), lambda b, i, j: (b, i, j)),
        out_shape=SDS((bx, t, f), BF16),
        compiler_params=_cp("parallel", "parallel", "arbitrary"),
        name="ffn_up",
    )(h, w1, w3)


def _mixers(pm, pab, pm_c, pab_c, p, need_ctx):
    b = pm.shape[0]
    y_na = _neighbourhood_attention(pm, pm_c, p["qn"], p["kn"], p["rpb"])
    zero = jnp.zeros((b, 2 * N_HEADS, HEAD_DIM, HEAD_DIM), F32)
    of_c, ob_c, s_ctx = _deltanet(_dn_conv(pm_c, p["conv"]), pab_c, p["a_log"], p["dt_bias"], zero)
    of, ob, _ = _deltanet(_dn_conv(pm, p["conv"]), pab, p["a_log"], p["dt_bias"], s_ctx)
    ys = (y_na, _dn_out(of, ob, pm, p["onorm"]), _fourier(pm), _spatial_gating(pm, p["sg_w"], p["sg_bt"], p["sg_vn"]))
    if not need_ctx:
        return ys, None
    ys_c = (_dense_attention(pm_c, p["qn"], p["kn"]), _dn_out(of_c, ob_c, pm_c, p["onorm"]), _fourier(pm_c),
            _spatial_gating(pm_c, p["sg_w"], p["sg_bt"], p["sg_vn"]))
    return ys, ys_c


def kernel(x, c, ctx, c_ctx, w_ada, b_ada, norm1_w, norm2_w, w_in, na_qnorm, na_knorm, na_rpb, dn_conv, dn_a_log, dn_dt_bias, dn_onorm, sg_w, sg_b, sg_vnorm, w_gate, b_gate, w_branch, w_out, ffn_w1, ffn_w3, ffn_w2):
    bsz, seq, d = x.shape
    ctx_len = ctx.shape[1]
    depth = w_ada.shape[0]
    assert bsz + 1 <= MOD_ROWS and d % LANES == 0
    ctx_row = bsz

    cc = jnp.zeros((MOD_ROWS, d), F32).at[:bsz].set(c).at[ctx_row].set(c_ctx)
    mod_all = _ada_mod(cc, w_ada, b_ada)
    xc = ctx.reshape(1, bsz * ctx_len, d)

    w_in_b = w_in.astype(BF16)
    w_main = jnp.concatenate([w_in_b[..., :C_AB_SRC], w_in_b[..., C_AB_SRC + N_AB:]], axis=-1)
    w_ab = jnp.pad(w_in_b[..., C_AB_SRC:C_AB_SRC + N_AB], ((0, 0), (0, 0), (0, LANES - N_AB)))
    wg, wb, wo = w_gate.astype(BF16), w_branch.astype(BF16), w_out.astype(BF16)
    w1, w3, w2 = ffn_w1, ffn_w3, ffn_w2.astype(BF16)
    bg = b_gate.reshape(depth, N_BRANCH, 1, d)

    for l in range(depth):
        need_ctx = l < depth - 1
        mod = mod_all[l]
        p = dict(qn=na_qnorm[l][None], kn=na_knorm[l][None], rpb=na_rpb[l], conv=dn_conv[l], a_log=dn_a_log[l],
                 dt_bias=dn_dt_bias[l], onorm=dn_onorm[l][None], sg_w=sg_w[l].astype(BF16), sg_bt=sg_b[l].T,
                 sg_vn=sg_vnorm[l][None])
        n1w, n2w = norm1_w[l][None], norm2_w[l][None]

        pm, pab, h = _in_proj(x, n1w, mod, w_main, w_ab, l, None)
        pm_c, pab_c, h_c = _in_proj(xc, n1w, mod, w_main, w_ab, l, ctx_row)
        pm_c = pm_c.reshape(bsz, ctx_len, N_MAIN)
        pab_c = pab_c.reshape(bsz, ctx_len, LANES)
        ys, ys_c = _mixers(pm, pab, pm_c, pab_c, p, need_ctx)

        merged = _merge(h, ys, wg, bg, wb, l)
        x, h2 = _resid_proj_norm(merged, wo, l, x, mod, 2, n2w, 3, None)
        x = _resid_proj(_ffn_up(h2, w1, w3, l), w2, l, x, mod, 5, None)
        if need_ctx:
            ys_c = tuple(y.reshape(1, bsz * ctx_len, W_BR) for y in ys_c)
            merged_c = _merge(h_c, ys_c, wg, bg, wb, l)
            xc, h2_c = _resid_proj_norm(merged_c, wo, l, xc, mod, 2, n2w, 3, ctx_row)
            xc = _resid_proj(_ffn_up(h2_c, w1, w3, l), w2, l, xc, mod, 5, ctx_row)
    return x
```

```python
import functools

import numpy as np
import jax
import jax.numpy as jnp
from jax import lax
from jax.experimental import pallas as pl
from jax.experimental.pallas import tpu as pltpu

F32 = jnp.float32
BF16 = jnp.bfloat16
SDS = jax.ShapeDtypeStruct

LANES = 128
HEAD_DIM = 128
N_HEADS = 4
W_BR = N_HEADS * HEAD_DIM
N_BRANCH = 4
GRID_W = 64
NA_WIN_R = 8
NA_WIN_C = 16
NA_ROWS_PER_STEP = 16
DN_CHUNK = 64
DN_BATCH_PER_STEP = 4
DN_INV_BASE = 8
DN_CONV = 5
DN_CONV_PAD = 8
DN_CONV_WIDE_TILE_ROWS = 1024
SG_CHUNK = 128
FT_UNROLL = 16
FT_PAD = 4
N_MOD = 6
EPS = 1e-6
NEG_INF = -1e30
MOD_ROWS = 8
N_AB = 4 * N_HEADS
VMEM_LIMIT_BYTES = 56 * 1024 * 1024
RESIDENT_WEIGHT_BYTES = 8 * 1024 * 1024
FFN_UP_ROWS = 2048

C_NA_Q, C_NA_K, C_NA_V = 0, W_BR, 2 * W_BR
C_DN_QKV = 3 * W_BR
C_DN_Z = 6 * W_BR
C_FT = 7 * W_BR
C_SG_U = 8 * W_BR
C_SG_V = 9 * W_BR
N_MAIN = 10 * W_BR
C_AB_SRC = 7 * W_BR


def _cp(*sem):
    return pltpu.CompilerParams(dimension_semantics=sem, vmem_limit_bytes=VMEM_LIMIT_BYTES)


def _tile(n, pref, mult=8):
    if n <= pref:
        return n
    for t in range(pref - pref % mult, 0, -mult):
        if n % t == 0:
            return t
    return n


def _dot(a, b):
    return jnp.dot(a, b, preferred_element_type=F32)


def _dot_nt(a, b):
    return lax.dot_general(a, b, (((1,), (1,)), ((), ())), preferred_element_type=F32)


def _dot_tn(a, b):
    return lax.dot_general(a, b, (((0,), (0,)), ((), ())), preferred_element_type=F32)


def _silu(x):
    return x * jax.nn.sigmoid(x)


def _rms(x, w):
    return x * lax.rsqrt(jnp.mean(x * x, axis=-1, keepdims=True) + EPS) * w


def _mod_row(ref, row):
    b = pl.program_id(0) if row is None else row
    return ref[pl.ds(b, 1), :]


def _norm_mod(x, nw_ref, sh_ref, sc_ref, row):
    return (_rms(x, nw_ref[...]) * (1.0 + _mod_row(sc_ref, row)) + _mod_row(sh_ref, row)).astype(BF16)


def _ada_kernel(c_ref, w_ref, b_ref, o_ref):
    s = _silu(c_ref[...]).astype(BF16)
    o_ref[0] = _dot(s, w_ref[0].astype(BF16)) + b_ref[0]


def _ada_mod(cc, w_ada, b_ada):
    depth, d, n = w_ada.shape
    tn = _tile(n, 1024, LANES)
    return pl.pallas_call(
        _ada_kernel,
        grid=(depth, n // tn),
        in_specs=[
            pl.BlockSpec((MOD_ROWS, d), lambda l, j: (0, 0)),
            pl.BlockSpec((1, d, tn), lambda l, j: (l, 0, j)),
            pl.BlockSpec((1, 1, tn), lambda l, j: (l, 0, j)),
        ],
        out_specs=pl.BlockSpec((1, MOD_ROWS, tn), lambda l, j: (l, 0, j)),
        out_shape=SDS((depth, MOD_ROWS, n), F32),
        compiler_params=_cp("parallel", "parallel"),
        name="ada_mod",
    )(cc, w_ada, b_ada.reshape(depth, 1, n))


def _in_proj_kernel(x_ref, nw_ref, sh_ref, sc_ref, w_ref, wab_ref, o_ref, oab_ref, h_ref, hs_ref, *, row):
    @pl.when(pl.program_id(2) == 0)
    def _():
        h = _norm_mod(x_ref[0], nw_ref, sh_ref, sc_ref, row)
        hs_ref[...] = h
        h_ref[0] = h
        oab_ref[0] = _dot(h, wab_ref[0])

    o_ref[0] = _dot(hs_ref[...], w_ref[0]).astype(o_ref.dtype)


def _in_proj(x, nw, mod, w_main, w_ab, l, row):
    bx, t, d = x.shape
    n = w_main.shape[2]
    tm, tn = _tile(t, 1024), _tile(n, 1280, LANES)
    return pl.pallas_call(
        functools.partial(_in_proj_kernel, row=row),
        grid=(bx, t // tm, n // tn),
        in_specs=[
            pl.BlockSpec((1, tm, d), lambda b, i, j: (b, i, 0)),
            pl.BlockSpec((1, d), lambda b, i, j: (0, 0)),
            pl.BlockSpec((MOD_ROWS, d), lambda b, i, j: (0, 0)),
            pl.BlockSpec((MOD_ROWS, d), lambda b, i, j: (0, 1)),
            pl.BlockSpec((1, d, tn), lambda b, i, j: (l, 0, j)),
            pl.BlockSpec((1, d, LANES), lambda b, i, j: (l, 0, 0)),
        ],
        out_specs=[
            pl.BlockSpec((1, tm, tn), lambda b, i, j: (b, i, j)),
            pl.BlockSpec((1, tm, LANES), lambda b, i, j: (b, i, 0)),
            pl.BlockSpec((1, tm, d), lambda b, i, j: (b, i, 0)),
        ],
        out_shape=[SDS((bx, t, n), BF16), SDS((bx, t, LANES), F32), SDS((bx, t, d), BF16)],
        scratch_shapes=[pltpu.VMEM((tm, d), BF16)],
        compiler_params=_cp("parallel", "parallel", "arbitrary"),
        name="in_proj",
    )(x, nw, mod, mod, w_main, w_ab)


def _na_kernel(q_ref, k_ref, v_ref, kc_ref, vc_ref, qn_ref, kn_ref, bias_ref, o_ref,
               qs, ks, vs, kcs, vcs, *, rows):
    averaging = jnp.full((HEAD_DIM, HEAD_DIM), 1.0 / HEAD_DIM, BF16)

    def rms(x, w):
        return x * lax.rsqrt(_dot((x * x).astype(BF16), averaging) + EPS) * w

    qs[...] = (rms(q_ref[0].astype(F32), qn_ref[...]) * HEAD_DIM ** -0.5).astype(BF16)
    ks[...] = rms(k_ref[0].astype(F32), kn_ref[...]).astype(BF16)
    vs[...] = v_ref[0].astype(BF16)
    kcs[...] = rms(kc_ref[0].astype(F32), kn_ref[...]).astype(BF16)
    vcs[...] = vc_ref[0].astype(BF16)
    band = NA_WIN_R * GRID_W
    nr = NA_ROWS_PER_STEP if rows % NA_ROWS_PER_STEP == 0 else NA_WIN_R

    def body(g, carry):
        rr = [g * nr + i for i in range(nr)]
        start = [jnp.clip(r - NA_WIN_R // 2, 0, rows - NA_WIN_R) for r in rr]
        r0 = [pl.multiple_of(r * GRID_W, GRID_W) for r in rr]
        k0 = [pl.multiple_of(s * GRID_W, GRID_W) for s in start]
        q = [qs[pl.ds(x, GRID_W), :] for x in r0]
        kb = [ks[pl.ds(x, band), :] for x in k0]
        vb = [vs[pl.ds(x, band), :] for x in k0]
        bias = [bias_ref[0, r - s] for r, s in zip(rr, start)]
        kc, vc = kcs[...], vcs[...]
        s_loc = [_dot_nt(q[i], kb[i]) + bias[i] for i in range(nr)]
        s_ctx = [_dot_nt(q[i], kc) for i in range(nr)]
        m = [jnp.maximum(jnp.max(s_loc[i], axis=-1, keepdims=True), jnp.max(s_ctx[i], axis=-1, keepdims=True))
             for i in range(nr)]
        p = [jnp.exp(s_loc[i] - m[i]) for i in range(nr)]
        pc = [jnp.exp(s_ctx[i] - m[i]) for i in range(nr)]
        denom = [jnp.sum(p[i], axis=-1, keepdims=True) + jnp.sum(pc[i], axis=-1, keepdims=True) for i in range(nr)]
        o = [_dot(p[i].astype(BF16), vb[i]) + _dot(pc[i].astype(BF16), vc) for i in range(nr)]
        for i in range(nr):
            o_ref[0, pl.ds(r0[i], GRID_W), :] = (o[i] / denom[i]).astype(o_ref.dtype)
        return carry

    lax.fori_loop(0, rows // nr, body, 0)


def _na_bias_table(rpb):
    col = np.arange(GRID_W)
    dc = np.clip(col[None, :] - col[:, None], 1 - NA_WIN_C, NA_WIN_C - 1) + NA_WIN_C - 1
    col_start = np.clip(col - NA_WIN_C // 2, 0, GRID_W - NA_WIN_C)
    in_win = (col[None, :] >= col_start[:, None]) & (col[None, :] < col_start[:, None] + NA_WIN_C)
    rows = jnp.stack([rpb[:, NA_WIN_R - 1 - typ:2 * NA_WIN_R - 1 - typ, :] for typ in range(NA_WIN_R)], axis=1)
    onehot = (dc[None] == np.arange(2 * NA_WIN_C - 1)[:, None, None]).astype(np.float32)
    b = jnp.einsum('htjc,cqk->htqjk', rows.astype(F32), onehot, precision=lax.Precision.HIGHEST)
    b = jnp.where(in_win[None, None, :, None, :], b, NEG_INF)
    return b.reshape(rpb.shape[0], NA_WIN_R, GRID_W, NA_WIN_R * GRID_W)


def _neighbourhood_attention(pm, pm_c, qn, kn, rpb):
    b, t, _ = pm.shape
    ctx_len = pm_c.shape[1]
    rows = t // GRID_W
    assert rows >= NA_WIN_R and rows * GRID_W == t and rows % NA_WIN_R == 0
    band = NA_WIN_R * GRID_W
    hb = lambda off: (lambda bi, h: (bi, 0, off // HEAD_DIM + h))
    return pl.pallas_call(
        functools.partial(_na_kernel, rows=rows),
        grid=(b, N_HEADS),
        in_specs=[
            pl.BlockSpec((1, t, HEAD_DIM), hb(C_NA_Q)),
            pl.BlockSpec((1, t, HEAD_DIM), hb(C_NA_K)),
            pl.BlockSpec((1, t, HEAD_DIM), hb(C_NA_V)),
            pl.BlockSpec((1, ctx_len, HEAD_DIM), hb(C_NA_K)),
            pl.BlockSpec((1, ctx_len, HEAD_DIM), hb(C_NA_V)),
            pl.BlockSpec((1, HEAD_DIM), lambda bi, h: (0, 0)),
            pl.BlockSpec((1, HEAD_DIM), lambda bi, h: (0, 0)),
            pl.BlockSpec((1, NA_WIN_R, GRID_W, band), lambda bi, h: (h, 0, 0, 0)),
        ],
        out_specs=pl.BlockSpec((1, t, HEAD_DIM), lambda bi, h: (bi, 0, h)),
        out_shape=SDS((b, t, W_BR), BF16),
        scratch_shapes=[pltpu.VMEM((t, HEAD_DIM), BF16)] * 3 + [pltpu.VMEM((ctx_len, HEAD_DIM), BF16)] * 2,
        compiler_params=_cp("parallel", "parallel"),
        name="natten",
    )(pm, pm, pm, pm_c, pm_c, qn, kn, _na_bias_table(rpb))


def _ctx_attn_kernel(q_ref, k_ref, v_ref, qn_ref, kn_ref, o_ref):
    q = _rms(q_ref[0].astype(F32), qn_ref[...]).astype(BF16)
    k = _rms(k_ref[0].astype(F32), kn_ref[...]).astype(BF16)
    s = _dot_nt(q, k) * HEAD_DIM ** -0.5
    p = jnp.exp(s - jnp.max(s, axis=-1, keepdims=True))
    o = _dot(p.astype(BF16), v_ref[0].astype(BF16)) / jnp.sum(p, axis=-1, keepdims=True)
    o_ref[0] = o.astype(o_ref.dtype)


def _dense_attention(pm_c, qn, kn):
    b, ctx_len, _ = pm_c.shape
    hb = lambda off: (lambda bi, h: (bi, 0, off // HEAD_DIM + h))
    return pl.pallas_call(
        _ctx_attn_kernel,
        grid=(b, N_HEADS),
        in_specs=[
            pl.BlockSpec((1, ctx_len, HEAD_DIM), hb(C_NA_Q)),
            pl.BlockSpec((1, ctx_len, HEAD_DIM), hb(C_NA_K)),
            pl.BlockSpec((1, ctx_len, HEAD_DIM), hb(C_NA_V)),
            pl.BlockSpec((1, HEAD_DIM), lambda bi, h: (0, 0)),
            pl.BlockSpec((1, HEAD_DIM), lambda bi, h: (0, 0)),
        ],
        out_specs=pl.BlockSpec((1, ctx_len, HEAD_DIM), lambda bi, h: (bi, 0, h)),
        out_shape=SDS((b, ctx_len, W_BR), BF16),
        compiler_params=_cp("parallel", "parallel"),
        name="ctx_attn",
    )(pm_c, pm_c, pm_c, qn, kn)


def _dn_conv_kernel(x_ref, w_ref, o_ref, xp_ref, *, n_norm_tiles):
    x = x_ref[0].astype(F32)
    t = x.shape[0]
    border = jnp.zeros((DN_CONV_PAD, x.shape[1]), F32)
    xp_ref[0:DN_CONV_PAD, :] = border
    xp_ref[DN_CONV_PAD + t:, :] = border
    xp_ref[DN_CONV_PAD:DN_CONV_PAD + t, :] = x
    half = DN_CONV // 2
    acc = x * w_ref[half:half + 1, :]
    for kk in range(DN_CONV):
        d = kk - half
        if d != 0:
            acc = acc + xp_ref[DN_CONV_PAD + d:DN_CONV_PAD + d + t, :] * w_ref[kk:kk + 1, :]
    y = _silu(acc)
    normed = pl.program_id(1) < n_norm_tiles
    for h in range(y.shape[1] // HEAD_DIM):
        sl = slice(h * HEAD_DIM, (h + 1) * HEAD_DIM)
        yh = y[:, sl]
        inv_norm = lax.rsqrt(jnp.sum(yh * yh, axis=-1, keepdims=True) + EPS)
        o_ref[0, :, sl] = yh * jnp.where(normed, inv_norm, 1.0)


def _dn_conv(pm, conv_w):
    b, t, _ = pm.shape
    tw = W_BR if t <= DN_CONV_WIDE_TILE_ROWS else HEAD_DIM
    return pl.pallas_call(
        functools.partial(_dn_conv_kernel, n_norm_tiles=2 * W_BR // tw),
        grid=(b, 3 * W_BR // tw),
        in_specs=[
            pl.BlockSpec((1, t, tw), lambda bi, j: (bi, 0, C_DN_QKV // tw + j)),
            pl.BlockSpec((DN_CONV, tw), lambda bi, j: (0, j)),
        ],
        out_specs=pl.BlockSpec((1, t, tw), lambda bi, j: (bi, 0, j)),
        out_shape=SDS((b, t, 3 * W_BR), F32),
        scratch_shapes=[pltpu.VMEM((t + 2 * DN_CONV_PAD, tw), F32)],
        compiler_params=_cp("parallel", "parallel"),
        name="dn_conv",
    )(pm, conv_w)


def _softplus(x):
    return jnp.maximum(x, 0.0) + jnp.log1p(jnp.exp(-jnp.abs(x)))


def _block_masks(n):
    ii = lax.broadcasted_iota(jnp.int32, (n, n), 0)
    jj = lax.broadcasted_iota(jnp.int32, (n, n), 1)
    same = lambda s: (ii // s) == (jj // s)
    masks, s = [same(DN_INV_BASE)], DN_INV_BASE
    while s < n:
        masks.append(same(2 * s) & jnp.logical_not(same(s)))
        s *= 2
    return masks


def _unit_triangular_inverses(a_list, eye, masks):
    mm = lambda x, y: _dot(x.astype(BF16), y.astype(BF16))
    n = len(a_list)
    pw = [jnp.where(masks[0], a, 0.0) for a in a_list]
    inv = [eye - p for p in pw]
    s = 2
    while s < DN_INV_BASE:
        pw = [mm(p, p) for p in pw]
        inv = [inv[i] + mm(inv[i], pw[i]) for i in range(n)]
        s *= 2
    for m in masks[1:]:
        t = [mm(jnp.where(m, a_list[i], 0.0), inv[i]) for i in range(n)]
        inv = [inv[i] - mm(inv[i], t[i]) for i in range(n)]
    return inv


def _dn_kernel(qf_ref, kf_ref, vf_ref, qb_ref, kb_ref, vb_ref, abf_ref, abb_ref, abtf_ref, abtb_ref,
               prow_ref, pcol_ref, s0_ref, of_ref, ob_ref, sfin_ref, s_ref):
    c = pl.program_id(1)

    @pl.when(c == 0)
    def _():
        s_ref[...] = s0_ref[...].reshape(s_ref.shape)

    cs = DN_CHUNK
    ii = lax.broadcasted_iota(jnp.int32, (cs, cs), 0)
    jj = lax.broadcasted_iota(jnp.int32, (cs, cs), 1)
    eye = (ii == jj).astype(F32)
    blk_masks = _block_masks(cs)
    scale = HEAD_DIM ** -0.5
    hi = lax.Precision.HIGHEST

    nb = qf_ref.shape[0]
    nst = 2 * N_HEADS
    chains = [(bb, d, h) for bb in range(nb) for d in range(2) for h in range(N_HEADS)]
    nch = len(chains)
    incl = [(jj <= ii), (jj >= ii)]
    strict = [(jj < ii), (jj > ii)]
    gc_col, gc_row, beta_all = {}, {}, {}
    for bb in range(nb):
        for d in range(2):
            tri = incl[d].astype(F32)
            ab = (abf_ref if d == 0 else abb_ref)[bb]
            abt = (abtf_ref if d == 0 else abtb_ref)[bb, 0]
            g_col = -jnp.exp(prow_ref[0:1, :]) * _softplus(ab + prow_ref[1:2, :])
            g_row = -jnp.exp(pcol_ref[:, 0:1]) * _softplus(abt + pcol_ref[:, 1:2])
            gc_col[bb, d] = jnp.dot(tri, g_col, preferred_element_type=F32, precision=hi)
            gc_row[bb, d] = lax.dot_general(g_row, tri, (((1,), (1,)), ((), ())),
                                            preferred_element_type=F32, precision=hi)
            beta_all[bb, d] = jax.nn.sigmoid(ab)
    qkv_refs = ((qf_ref, kf_ref, vf_ref), (qb_ref, kb_ref, vb_ref))
    sl = [slice(h * HEAD_DIM, (h + 1) * HEAD_DIM) for h in range(N_HEADS)]
    col = [d * N_HEADS + h for bb, d, h in chains]
    q = [qkv_refs[d][0][bb, :, sl[h]] * scale for bb, d, h in chains]
    k = [qkv_refs[d][1][bb, :, sl[h]] for bb, d, h in chains]
    v = [qkv_refs[d][2][bb, :, sl[h]] for bb, d, h in chains]
    s_old = [s_ref[bb * nst + col[j]] for j, (bb, d, h) in enumerate(chains)]
    gcc = [gc_col[bb, d][:, col[j]:col[j] + 1] for j, (bb, d, h) in enumerate(chains)]
    gcr = [gc_row[bb, d][col[j]:col[j] + 1, :] for j, (bb, d, h) in enumerate(chains)]
    beta = [beta_all[bb, d][:, nst + col[j]:nst + col[j] + 1] for j, (bb, d, h) in enumerate(chains)]
    decay = [jnp.exp(jnp.where(incl[d], gcc[j] - gcr[j], -jnp.inf)) for j, (bb, d, h) in enumerate(chains)]
    eg = [jnp.exp(x) for x in gcc]
    g_last = [gcc[j][cs - 1:cs, :] if d == 0 else gcc[j][0:1, :] for j, (bb, d, h) in enumerate(chains)]
    kbf = [x.astype(BF16) for x in k]
    kbeta = [k[j] * beta[j] for j in range(nch)]
    kq = [_dot_nt(jnp.concatenate([kbeta[j], q[j]], axis=0).astype(BF16), kbf[j]) for j in range(nch)]
    a_mat = [jnp.where(strict[d], kq[j][:cs] * decay[j], 0.0) for j, (bb, d, h) in enumerate(chains)]
    qk = [jnp.where(incl[d], kq[j][cs:] * decay[j], 0.0) for j, (bb, d, h) in enumerate(chains)]
    inv = _unit_triangular_inverses(a_mat, eye, blk_masks)
    rhs = [jnp.concatenate([v[j] * beta[j], kbeta[j] * eg[j]], axis=1).astype(BF16) for j in range(nch)]
    uw = [_dot(inv[j].astype(BF16), rhs[j]) for j in range(nch)]
    ws_qs = [_dot(jnp.concatenate([uw[j][:, HEAD_DIM:], q[j] * eg[j]], axis=0).astype(BF16), s_old[j].astype(BF16))
             for j in range(nch)]
    v_new = [uw[j][:, :HEAD_DIM] - ws_qs[j][:cs] for j in range(nch)]
    vnb = [x.astype(BF16) for x in v_new]
    o = [ws_qs[j][cs:] + _dot(qk[j].astype(BF16), vnb[j]) for j in range(nch)]
    kd = [(k[j] * jnp.exp(g_last[j] - gcc[j])).astype(BF16) for j in range(nch)]
    s_new = [s_old[j] * jnp.exp(g_last[j]) + _dot_tn(kd[j], vnb[j]) for j in range(nch)]
    for j, (bb, d, h) in enumerate(chains):
        s_ref[bb * nst + col[j]] = s_new[j]
        (of_ref if d == 0 else ob_ref)[bb, :, sl[h]] = o[j]

    @pl.when(c == pl.num_programs(1) - 1)
    def _():
        sfin_ref[...] = s_ref[...].reshape(sfin_ref.shape)


def _deltanet(act, pab, a_log, dt_bias, s0):
    b, t, _ = act.shape
    n = t // DN_CHUNK
    assert n * DN_CHUNK == t
    abt = jnp.swapaxes(pab[..., :N_AB].reshape(b, n, DN_CHUNK, N_AB), 2, 3)
    al, dt = a_log.reshape(-1).astype(F32), dt_bias.reshape(-1).astype(F32)
    prow = jnp.zeros((2, LANES), F32).at[0, :al.size].set(al).at[1, :dt.size].set(dt)
    pcol = jnp.zeros((N_AB, 2), F32).at[:al.size, 0].set(al).at[:dt.size, 1].set(dt)
    nb = DN_BATCH_PER_STEP if b % DN_BATCH_PER_STEP == 0 else 1
    fwd = lambda part: (lambda bi, c: (bi, c, part))
    bwd = lambda part: (lambda bi, c: (bi, n - 1 - c, part))
    qkv_spec = lambda imap: pl.BlockSpec((nb, DN_CHUNK, W_BR), imap)
    n_state = 2 * N_HEADS
    state_spec = pl.BlockSpec((nb, n_state, HEAD_DIM, HEAD_DIM), lambda bi, c: (bi, 0, 0, 0))
    return pl.pallas_call(
        _dn_kernel,
        grid=(b // nb, n),
        in_specs=[
            qkv_spec(fwd(0)), qkv_spec(fwd(1)), qkv_spec(fwd(2)),
            qkv_spec(bwd(0)), qkv_spec(bwd(1)), qkv_spec(bwd(2)),
            pl.BlockSpec((nb, DN_CHUNK, LANES), fwd(0)),
            pl.BlockSpec((nb, DN_CHUNK, LANES), bwd(0)),
            pl.BlockSpec((nb, 1, N_AB, DN_CHUNK), lambda bi, c: (bi, c, 0, 0)),
            pl.BlockSpec((nb, 1, N_AB, DN_CHUNK), lambda bi, c: (bi, n - 1 - c, 0, 0)),
            pl.BlockSpec((2, LANES), lambda bi, c: (0, 0)),
            pl.BlockSpec((N_AB, 2), lambda bi, c: (0, 0)),
            state_spec,
        ],
        out_specs=[
            pl.BlockSpec((nb, DN_CHUNK, W_BR), fwd(0)),
            pl.BlockSpec((nb, DN_CHUNK, W_BR), bwd(0)),
            state_spec,
        ],
        out_shape=[SDS((b, t, W_BR), F32), SDS((b, t, W_BR), F32),
                   SDS((b, n_state, HEAD_DIM, HEAD_DIM), F32)],
        scratch_shapes=[pltpu.VMEM((nb * n_state, HEAD_DIM, HEAD_DIM), F32)],
        compiler_params=_cp("parallel", "arbitrary"),
        name="deltanet",
    )(act, act, act, act, act, act, pab, pab, abt, abt, prow, pcol, s0)


def _dn_out_kernel(of_ref, ob_ref, z_ref, w_ref, y_ref):
    o = of_ref[0] + ob_ref[0]
    z = z_ref[0].astype(F32)
    for h in range(N_HEADS):
        sl = slice(h * HEAD_DIM, (h + 1) * HEAD_DIM)
        y_ref[0, :, sl] = (_rms(o[:, sl], w_ref[...]) * _silu(z[:, sl])).astype(y_ref.dtype)


def _dn_out(o_f, o_b, pm, onorm):
    b, t, _ = o_f.shape
    tm = _tile(t, 1024)
    return pl.pallas_call(
        _dn_out_kernel,
        grid=(b, t // tm),
        in_specs=[
            pl.BlockSpec((1, tm, W_BR), lambda bi, i: (bi, i, 0)),
            pl.BlockSpec((1, tm, W_BR), lambda bi, i: (bi, i, 0)),
            pl.BlockSpec((1, tm, W_BR), lambda bi, i: (bi, i, C_DN_Z // W_BR)),
            pl.BlockSpec((1, HEAD_DIM), lambda bi, i: (0, 0)),
        ],
        out_specs=pl.BlockSpec((1, tm, W_BR), lambda bi, i: (bi, i, 0)),
        out_shape=SDS((b, t, W_BR), BF16),
        compiler_params=_cp("parallel", "parallel"),
        name="dn_out",
    )(o_f, o_b, pm, onorm)


def _fourier_kernel(x_ref, m1_ref, m2_ref, m3_ref, twc_ref, tws_ref, o_ref, xf_ref, bre_ref, bim_ref, zo_ref,
                    *, n1, n2, scale):
    m1, m2, m3 = m1_ref[...].astype(BF16), m2_ref[...].astype(BF16), m3_ref[...].astype(BF16)
    nu = FT_UNROLL
    p2, p1 = n2 + FT_PAD, n1 + FT_PAD
    for t1 in range(n1):
        xf_ref[t1 * p2:t1 * p2 + n2, :] = x_ref[0, t1 * n2:(t1 + 1) * n2, :].astype(F32)

    def stage1(i, carry):
        t2 = [i * nu + u for u in range(nu)]
        xs = [xf_ref[pl.ds(t, n1, stride=p2), :].astype(BF16) for t in t2]
        a = [_dot(m1, x) for x in xs]
        cw = [twc_ref[t] for t in t2]
        sw = [tws_ref[t] for t in t2]
        bre = [a[u][:n1] * cw[u] - a[u][n1:] * sw[u] for u in range(nu)]
        bim = [-(a[u][n1:] * cw[u] + a[u][:n1] * sw[u]) for u in range(nu)]
        for u in range(nu):
            bre_ref[pl.ds(t2[u], n1, stride=p2), :] = bre[u]
            bim_ref[pl.ds(t2[u], n1, stride=p2), :] = bim[u]
        return carry

    lax.fori_loop(0, n2 // nu, stage1, 0)

    def stage2(i, carry):
        f1 = [i * nu + u for u in range(nu)]
        r0 = [pl.multiple_of(f * p2, FT_PAD) for f in f1]
        b = [jnp.concatenate([bre_ref[pl.ds(r, n2), :], bim_ref[pl.ds(r, n2), :]], axis=0).astype(BF16) for r in r0]
        y = [_dot(m2, x) for x in b]
        z = [_dot(jnp.concatenate([yy[:n2], yy[n2:]], axis=1).astype(BF16), m3) for yy in y]
        for u in range(nu):
            zo_ref[pl.ds(f1[u], n2, stride=p1), :] = z[u] * scale
        return carry

    lax.fori_loop(0, n1 // nu, stage2, 0)
    for f2 in range(n2):
        o_ref[0, f2 * n1:(f2 + 1) * n1, :] = zo_ref[f2 * p1:f2 * p1 + n1, :].astype(o_ref.dtype)


def _dft_mats(n):
    ang = 2.0 * np.pi * ((np.arange(n)[:, None] * np.arange(n)[None, :]) % n) / n
    return np.cos(ang), np.sin(ang)


def _fourier(pm):
    b, t, _ = pm.shape
    n1 = 1 << ((t.bit_length() - 1 + 1) // 2)
    n2 = t // n1
    assert n1 * n2 == t and n1 % FT_UNROLL == 0 and n2 % FT_UNROLL == 0
    c1, s1 = _dft_mats(n1)
    c2, s2 = _dft_mats(n2)
    cc, sc = _dft_mats(HEAD_DIM)
    m1 = jnp.asarray(np.concatenate([c1, s1], axis=0), F32)
    m2 = jnp.asarray(np.block([[c2, s2], [-s2, c2]]), F32)
    m3 = jnp.asarray(np.concatenate([cc, sc], axis=0), F32)
    ang = 2.0 * np.pi * ((np.arange(n2)[:, None] * np.arange(n1)[None, :]) % t) / t
    twc = jnp.broadcast_to(jnp.asarray(np.cos(ang), F32)[:, :, None], (n2, n1, HEAD_DIM))
    tws = jnp.broadcast_to(jnp.asarray(np.sin(ang), F32)[:, :, None], (n2, n1, HEAD_DIM))
    const = lambda shape: pl.BlockSpec(shape, lambda bi, g: (0,) * len(shape))
    return pl.pallas_call(
        functools.partial(_fourier_kernel, n1=n1, n2=n2, scale=float((t * HEAD_DIM) ** -0.5)),
        grid=(b, N_HEADS),
        in_specs=[
            pl.BlockSpec((1, t, HEAD_DIM), lambda bi, g: (bi, 0, C_FT // HEAD_DIM + g)),
            const(m1.shape), const(m2.shape), const(m3.shape),
            const((n2, n1, HEAD_DIM)), const((n2, n1, HEAD_DIM)),
        ],
        out_specs=pl.BlockSpec((1, t, HEAD_DIM), lambda bi, g: (bi, 0, g)),
        out_shape=SDS((b, t, W_BR), BF16),
        scratch_shapes=[pltpu.VMEM((n1 * (n2 + FT_PAD), HEAD_DIM), F32)] * 3
        + [pltpu.VMEM((n2 * (n1 + FT_PAD), HEAD_DIM), F32)],
        compiler_params=_cp("parallel", "parallel"),
        name="fourier",
    )(pm, m1, m2, m3, twc, tws)


def _sgu_kernel(u_ref, v_ref, vn_ref, ws_ref, bs_ref, o_ref):
    u = jax.nn.gelu(u_ref[0].astype(F32))
    v = jax.nn.gelu(v_ref[0].astype(F32))
    mu = jnp.mean(v, axis=-1, keepdims=True)
    var = jnp.mean(jnp.square(v - mu), axis=-1, keepdims=True)
    vb = ((v - mu) * lax.rsqrt(var + EPS) * vn_ref[...]).astype(BF16)
    out = []
    for ch in range(u.shape[0] // SG_CHUNK):
        rs = slice(ch * SG_CHUNK, (ch + 1) * SG_CHUNK)
        mix = [_dot(ws_ref[g], vb[rs, g * HEAD_DIM:(g + 1) * HEAD_DIM]) + bs_ref[:, g:g + 1] for g in range(N_HEADS)]
        out.append((u[rs, :] * jnp.concatenate(mix, axis=1)).astype(o_ref.dtype))
    for ch, y in enumerate(out):
        o_ref[0, ch * SG_CHUNK:(ch + 1) * SG_CHUNK, :] = y


def _spatial_gating(pm, ws, bs_t, vnorm):
    b, t, _ = pm.shape
    tm = _tile(t, 4 * SG_CHUNK, SG_CHUNK)
    assert tm % SG_CHUNK == 0
    return pl.pallas_call(
        _sgu_kernel,
        grid=(b, t // tm),
        in_specs=[
            pl.BlockSpec((1, tm, W_BR), lambda bi, i: (bi, i, C_SG_U // W_BR)),
            pl.BlockSpec((1, tm, W_BR), lambda bi, i: (bi, i, C_SG_V // W_BR)),
            pl.BlockSpec((1, W_BR), lambda bi, i: (0, 0)),
            pl.BlockSpec((N_HEADS, SG_CHUNK, SG_CHUNK), lambda bi, i: (0, 0, 0)),
            pl.BlockSpec((SG_CHUNK, N_HEADS), lambda bi, i: (0, 0)),
        ],
        out_specs=pl.BlockSpec((1, tm, W_BR), lambda bi, i: (bi, i, 0)),
        out_shape=SDS((b, t, W_BR), BF16),
        compiler_params=_cp("parallel", "parallel"),
        name="sgu",
    )(pm, pm, vnorm, ws, bs_t)


def _merge_kernel(h_ref, y0_ref, y1_ref, y2_ref, y3_ref, wg_ref, bg_ref, wb_ref, o_ref):
    h = h_ref[0]
    acc = None
    for i, y_ref in enumerate((y0_ref, y1_ref, y2_ref, y3_ref)):
        gate = jax.nn.sigmoid(_dot(h, wg_ref[0, i]) + bg_ref[0, i])
        term = gate * _dot(y_ref[0].astype(BF16), wb_ref[0, i].astype(BF16))
        acc = term if acc is None else acc + term
    o_ref[0] = acc.astype(o_ref.dtype)


def _merge(h, ys, wg, bg, wb, l):
    bx, t, d = h.shape
    tm, tn = _tile(t, 1024), _tile(d, 512, LANES)
    y_spec = pl.BlockSpec((1, tm, W_BR), lambda b, i, j: (b, i, 0))
    return pl.pallas_call(
        _merge_kernel,
        grid=(bx, t // tm, d // tn),
        in_specs=[
            pl.BlockSpec((1, tm, d), lambda b, i, j: (b, i, 0)),
            y_spec, y_spec, y_spec, y_spec,
            pl.BlockSpec((1, N_BRANCH, d, tn), lambda b, i, j: (l, 0, 0, j)),
            pl.BlockSpec((1, N_BRANCH, 1, tn), lambda b, i, j: (l, 0, 0, j)),
            pl.BlockSpec((1, N_BRANCH, W_BR, tn), lambda b, i, j: (l, 0, 0, j)),
        ],
        out_specs=pl.BlockSpec((1, tm, tn), lambda b, i, j: (b, i, j)),
        out_shape=SDS((bx, t, d), BF16),
        compiler_params=_cp("parallel", "parallel", "arbitrary"),
        name="merge",
    )(h, *ys, wg, bg, wb)


def _resid_kernel(a_ref, w_ref, x_ref, g_ref, o_ref, *, row):
    o_ref[0] = x_ref[0] + _mod_row(g_ref, row) * _dot(a_ref[0], w_ref[0])


def _resid_norm_kernel(a_ref, w_ref, x_ref, g_ref, nw_ref, sh_ref, sc_ref, o_ref, h_ref, *, row):
    x_new = x_ref[0] + _mod_row(g_ref, row) * _dot(a_ref[0], w_ref[0])
    o_ref[0] = x_new
    h_ref[0] = _norm_mod(x_new, nw_ref, sh_ref, sc_ref, row)


def _resid_proj_norm(a, w, l, x, mod, gate_blk, nw, norm_blk, row):
    bx, t, k = a.shape
    d = w.shape[2]
    assert k * d * 2 <= RESIDENT_WEIGHT_BYTES
    tm = _tile(t, 512)
    row_spec = lambda dt: pl.BlockSpec((1, tm, d), lambda b, i: (b, i, 0))
    return pl.pallas_call(
        functools.partial(_resid_norm_kernel, row=row),
        grid=(bx, t // tm),
        in_specs=[
            pl.BlockSpec((1, tm, k), lambda b, i: (b, i, 0)),
            pl.BlockSpec((1, k, d), lambda b, i: (l, 0, 0)),
            row_spec(F32),
            pl.BlockSpec((MOD_ROWS, d), lambda b, i: (0, gate_blk)),
            pl.BlockSpec((1, d), lambda b, i: (0, 0)),
            pl.BlockSpec((MOD_ROWS, d), lambda b, i: (0, norm_blk)),
            pl.BlockSpec((MOD_ROWS, d), lambda b, i: (0, norm_blk + 1)),
        ],
        out_specs=[row_spec(F32), row_spec(BF16)],
        out_shape=[SDS((bx, t, d), F32), SDS((bx, t, d), BF16)],
        compiler_params=_cp("parallel", "parallel"),
        name="resid_proj_norm",
    )(a, w, x, mod, nw, mod, mod)


def _resid_proj(a, w, l, x, mod, gate_blk, row):
    bx, t, k = a.shape
    d = w.shape[2]
    tm, tn = _tile(t, 256), d
    nj = d // tn
    return pl.pallas_call(
        functools.partial(_resid_kernel, row=row),
        grid=(bx, t // tm, nj),
        in_specs=[
            pl.BlockSpec((1, tm, k), lambda b, i, j: (b, i, 0)),
            pl.BlockSpec((1, k, tn), lambda b, i, j: (l, 0, j), pipeline_mode=pl.Buffered(1)),
            pl.BlockSpec((1, tm, tn), lambda b, i, j: (b, i, j)),
            pl.BlockSpec((MOD_ROWS, tn), lambda b, i, j: (0, gate_blk * nj + j)),
        ],
        out_specs=pl.BlockSpec((1, tm, tn), lambda b, i, j: (b, i, j)),
        out_shape=SDS((bx, t, d), F32),
        compiler_params=_cp("parallel", "parallel", "arbitrary"),
        name="resid_proj",
    )(a, w, x, mod)


def _ffn_up_kernel(h_ref, w1_ref, w3_ref, o_ref):
    h = h_ref[0]
    w1, w3 = w1_ref[0].astype(BF16), w3_ref[0].astype(BF16)
    o_ref[0] = (_silu(_dot(h, w1)) * _dot(h, w3)).astype(o_ref.dtype)


def _ffn_up(h, w1, w3, l):
    bx, t, d = h.shape
    f = w1.shape[2]
    tm, tn = _tile(t, FFN_UP_ROWS), _tile(f, 512, LANES)
    return pl.pallas_call(
        _ffn_up_kernel,
        grid=(bx, t // tm, f // tn),
        in_specs=[
            pl.BlockSpec((1, tm, d), lambda b, i, j: (b, i, 0)),
            pl.BlockSpec((1, d, tn), lambda b, i, j: (l, 0, j)),
            pl.BlockSpec((1, d, tn), lambda b, i, j: (l, 0, j)),
        ],
        out_specs=pl.BlockSpec((1, tm, tn), lambda b, i, j: (b, i, j)),
        out_shape=SDS((bx, t, f), BF16),
        compiler_params=_cp("parallel", "parallel", "arbitrary"),
        name="ffn_up",
    )(h, w1, w3)


def _mixers(pm, pab, pm_c, pab_c, p, need_ctx):
    b = pm.shape[0]
    y_na = _neighbourhood_attention(pm, pm_c, p["qn"], p["kn"], p["rpb"])
    zero = jnp.zeros((b, 2 * N_HEADS, HEAD_DIM, HEAD_DIM), F32)
    of_c, ob_c, s_ctx = _deltanet(_dn_conv(pm_c, p["conv"]), pab_c, p["a_log"], p["dt_bias"], zero)
    of, ob, _ = _deltanet(_dn_conv(pm, p["conv"]), pab, p["a_log"], p["dt_bias"], s_ctx)
    ys = (y_na, _dn_out(of, ob, pm, p["onorm"]), _fourier(pm), _spatial_gating(pm, p["sg_w"], p["sg_bt"], p["sg_vn"]))
    if not need_ctx:
        return ys, None
    ys_c = (_dense_attention(pm_c, p["qn"], p["kn"]), _dn_out(of_c, ob_c, pm_c, p["onorm"]), _fourier(pm_c),
            _spatial_gating(pm_c, p["sg_w"], p["sg_bt"], p["sg_vn"]))
    return ys, ys_c


def kernel(x, c, ctx, c_ctx, w_ada, b_ada, norm1_w, norm2_w, w_in, na_qnorm, na_knorm, na_rpb, dn_conv, dn_a_log, dn_dt_bias, dn_onorm, sg_w, sg_b, sg_vnorm, w_gate, b_gate, w_branch, w_out, ffn_w1, ffn_w3, ffn_w2):
    bsz, seq, d = x.shape
    ctx_len = ctx.shape[1]
    depth = w_ada.shape[0]
    assert bsz + 1 <= MOD_ROWS and d % LANES == 0 and w_ada.shape[2] == N_MOD * d
    ctx_row = bsz

    cc = jnp.zeros((MOD_ROWS, d), F32).at[:bsz].set(c).at[ctx_row].set(c_ctx)
    mod_all = _ada_mod(cc, w_ada, b_ada)
    xc = ctx.reshape(1, bsz * ctx_len, d)

    w_in_b = w_in.astype(BF16)
    w_main = jnp.concatenate([w_in_b[..., :C_AB_SRC], w_in_b[..., C_AB_SRC + N_AB:]], axis=-1)
    w_ab = jnp.pad(w_in_b[..., C_AB_SRC:C_AB_SRC + N_AB], ((0, 0), (0, 0), (0, LANES - N_AB)))
    wg, wb, wo = w_gate.astype(BF16), w_branch, w_out.astype(BF16)
    w1, w3, w2 = ffn_w1, ffn_w3, ffn_w2.astype(BF16)
    bg = b_gate.reshape(depth, N_BRANCH, 1, d)

    for l in range(depth):
        need_ctx = l < depth - 1
        mod = mod_all[l]
        p = dict(qn=na_qnorm[l][None], kn=na_knorm[l][None], rpb=na_rpb[l], conv=dn_conv[l], a_log=dn_a_log[l],
                 dt_bias=dn_dt_bias[l], onorm=dn_onorm[l][None], sg_w=sg_w[l].astype(BF16), sg_bt=sg_b[l].T,
                 sg_vn=sg_vnorm[l][None])
        n1w, n2w = norm1_w[l][None], norm2_w[l][None]

        pm, pab, h = _in_proj(x, n1w, mod, w_main, w_ab, l, None)
        pm_c, pab_c, h_c = _in_proj(xc, n1w, mod, w_main, w_ab, l, ctx_row)
        pm_c = pm_c.reshape(bsz, ctx_len, N_MAIN)
        pab_c = pab_c.reshape(bsz, ctx_len, LANES)
        ys, ys_c = _mixers(pm, pab, pm_c, pab_c, p, need_ctx)

        merged = _merge(h, ys, wg, bg, wb, l)
        x, h2 = _resid_proj_norm(merged, wo, l, x, mod, 2, n2w, 3, None)
        x = _resid_proj(_ffn_up(h2, w1, w3, l), w2, l, x, mod, 5, None)
        if need_ctx:
            ys_c = tuple(y.reshape(1, bsz * ctx_len, W_BR) for y in ys_c)
            merged_c = _merge(h_c, ys_c, wg, bg, wb, l)
            xc, h2_c = _resid_proj_norm(merged_c, wo, l, xc, mod, 2, n2w, 3, ctx_row)
            xc = _resid_proj(_ffn_up(h2_c, w1, w3, l), w2, l, xc, mod, 5, ctx_row)
    return x
```

```python
import functools

import numpy as np
import jax
import jax.numpy as jnp
from jax import lax
from jax.experimental import pallas as pl
from jax.experimental.pallas import tpu as pltpu

F32 = jnp.float32
BF16 = jnp.bfloat16
SDS = jax.ShapeDtypeStruct

LANES = 128
HEAD_DIM = 128
N_HEADS = 4
W_BR = N_HEADS * HEAD_DIM
N_BRANCH = 4
GRID_W = 64
NA_WIN_R = 8
NA_WIN_C = 16
NA_ROWS_PER_STEP = 16
DN_CHUNK = 64
DN_BATCH_PER_STEP = 4
DN_INV_BASE = 8
DN_CONV = 5
DN_CONV_PAD = 8
DN_CONV_WIDE_TILE_ROWS = 1024
SG_CHUNK = 128
FT_UNROLL = 16
FT_PAD = 4
N_MOD = 6
EPS = 1e-6
NEG_INF = -1e30
MOD_ROWS = 8
N_AB = 4 * N_HEADS
VMEM_LIMIT_BYTES = 56 * 1024 * 1024
RESIDENT_WEIGHT_BYTES = 8 * 1024 * 1024
FFN_UP_ROWS = 2048

C_NA_Q, C_NA_K, C_NA_V = 0, W_BR, 2 * W_BR
C_DN_QKV = 3 * W_BR
C_DN_Z = 6 * W_BR
C_FT = 7 * W_BR
C_SG_U = 8 * W_BR
C_SG_V = 9 * W_BR
N_MAIN = 10 * W_BR
C_AB_SRC = 7 * W_BR


def _cp(*sem):
    return pltpu.CompilerParams(dimension_semantics=sem, vmem_limit_bytes=VMEM_LIMIT_BYTES)


def _tile(n, pref, mult=8):
    if n <= pref:
        return n
    for t in range(pref - pref % mult, 0, -mult):
        if n % t == 0:
            return t
    return n


def _dot(a, b):
    return jnp.dot(a, b, preferred_element_type=F32)


def _dot_nt(a, b):
    return lax.dot_general(a, b, (((1,), (1,)), ((), ())), preferred_element_type=F32)


def _dot_tn(a, b):
    return lax.dot_general(a, b, (((0,), (0,)), ((), ())), preferred_element_type=F32)


def _silu(x):
    return x * jax.nn.sigmoid(x)


def _rms(x, w):
    return x * lax.rsqrt(jnp.mean(x * x, axis=-1, keepdims=True) + EPS) * w


def _mod_row(ref, row):
    b = pl.program_id(0) if row is None else row
    return ref[pl.ds(b, 1), :]


def _norm_mod(x, nw_ref, sh_ref, sc_ref, row):
    return (_rms(x, nw_ref[...]) * (1.0 + _mod_row(sc_ref, row)) + _mod_row(sh_ref, row)).astype(BF16)


def _ada_kernel(c_ref, w_ref, b_ref, o_ref):
    s = _silu(c_ref[...]).astype(BF16)
    o_ref[0] = _dot(s, w_ref[0].astype(BF16)) + b_ref[0]


def _ada_mod(cc, w_ada, b_ada):
    depth, d, n = w_ada.shape
    tn = _tile(n, 1024, LANES)
    return pl.pallas_call(
        _ada_kernel,
        grid=(depth, n // tn),
        in_specs=[
            pl.BlockSpec((MOD_ROWS, d), lambda l, j: (0, 0)),
            pl.BlockSpec((1, d, tn), lambda l, j: (l, 0, j)),
            pl.BlockSpec((1, 1, tn), lambda l, j: (l, 0, j)),
        ],
        out_specs=pl.BlockSpec((1, MOD_ROWS, tn), lambda l, j: (l, 0, j)),
        out_shape=SDS((depth, MOD_ROWS, n), F32),
        compiler_params=_cp("parallel", "parallel"),
        name="ada_mod",
    )(cc, w_ada, b_ada.reshape(depth, 1, n))


def _in_proj_kernel(x_ref, nw_ref, sh_ref, sc_ref, w_ref, wab_ref, o_ref, oab_ref, h_ref, hs_ref, *, row):
    @pl.when(pl.program_id(2) == 0)
    def _():
        h = _norm_mod(x_ref[0], nw_ref, sh_ref, sc_ref, row)
        hs_ref[...] = h
        h_ref[0] = h
        oab_ref[0] = _dot(h, wab_ref[0])

    o_ref[0] = _dot(hs_ref[...], w_ref[0]).astype(o_ref.dtype)


def _in_proj(x, nw, mod, w_main, w_ab, l, row):
    bx, t, d = x.shape
    n = w_main.shape[2]
    tm, tn = _tile(t, 1024), _tile(n, 1280, LANES)
    return pl.pallas_call(
        functools.partial(_in_proj_kernel, row=row),
        grid=(bx, t // tm, n // tn),
        in_specs=[
            pl.BlockSpec((1, tm, d), lambda b, i, j: (b, i, 0)),
            pl.BlockSpec((1, d), lambda b, i, j: (0, 0)),
            pl.BlockSpec((MOD_ROWS, d), lambda b, i, j: (0, 0)),
            pl.BlockSpec((MOD_ROWS, d), lambda b, i, j: (0, 1)),
            pl.BlockSpec((1, d, tn), lambda b, i, j: (l, 0, j)),
            pl.BlockSpec((1, d, LANES), lambda b, i, j: (l, 0, 0)),
        ],
        out_specs=[
            pl.BlockSpec((1, tm, tn), lambda b, i, j: (b, i, j)),
            pl.BlockSpec((1, tm, LANES), lambda b, i, j: (b, i, 0)),
            pl.BlockSpec((1, tm, d), lambda b, i, j: (b, i, 0)),
        ],
        out_shape=[SDS((bx, t, n), BF16), SDS((bx, t, LANES), F32), SDS((bx, t, d), BF16)],
        scratch_shapes=[pltpu.VMEM((tm, d), BF16)],
        compiler_params=_cp("parallel", "parallel", "arbitrary"),
        name="in_proj",
    )(x, nw, mod, mod, w_main, w_ab)


def _na_kernel(q_ref, k_ref, v_ref, kc_ref, vc_ref, qn_ref, kn_ref, bias_ref, o_ref,
               qs, ks, vs, kcs, vcs, *, rows):
    averaging = jnp.full((HEAD_DIM, HEAD_DIM), 1.0 / HEAD_DIM, BF16)

    def rms(x, w):
        return x * lax.rsqrt(_dot((x * x).astype(BF16), averaging) + EPS) * w

    qs[...] = (rms(q_ref[0].astype(F32), qn_ref[...]) * HEAD_DIM ** -0.5).astype(BF16)
    ks[...] = rms(k_ref[0].astype(F32), kn_ref[...]).astype(BF16)
    vs[...] = v_ref[0].astype(BF16)
    kcs[...] = rms(kc_ref[0].astype(F32), kn_ref[...]).astype(BF16)
    vcs[...] = vc_ref[0].astype(BF16)
    band = NA_WIN_R * GRID_W
    nr = NA_ROWS_PER_STEP if rows % NA_ROWS_PER_STEP == 0 else NA_WIN_R

    def body(g, carry):
        rr = [g * nr + i for i in range(nr)]
        start = [jnp.clip(r - NA_WIN_R // 2, 0, rows - NA_WIN_R) for r in rr]
        r0 = [pl.multiple_of(r * GRID_W, GRID_W) for r in rr]
        k0 = [pl.multiple_of(s * GRID_W, GRID_W) for s in start]
        q = [qs[pl.ds(x, GRID_W), :] for x in r0]
        kb = [ks[pl.ds(x, band), :] for x in k0]
        vb = [vs[pl.ds(x, band), :] for x in k0]
        bias = [bias_ref[0, r - s] for r, s in zip(rr, start)]
        kc, vc = kcs[...], vcs[...]
        s_loc = [_dot_nt(q[i], kb[i]) + bias[i] for i in range(nr)]
        s_ctx = [_dot_nt(q[i], kc) for i in range(nr)]
        m = [jnp.maximum(jnp.max(s_loc[i], axis=-1, keepdims=True), jnp.max(s_ctx[i], axis=-1, keepdims=True))
             for i in range(nr)]
        p = [jnp.exp(s_loc[i] - m[i]) for i in range(nr)]
        pc = [jnp.exp(s_ctx[i] - m[i]) for i in range(nr)]
        denom = [jnp.sum(p[i], axis=-1, keepdims=True) + jnp.sum(pc[i], axis=-1, keepdims=True) for i in range(nr)]
        o = [_dot(p[i].astype(BF16), vb[i]) + _dot(pc[i].astype(BF16), vc) for i in range(nr)]
        for i in range(nr):
            o_ref[0, pl.ds(r0[i], GRID_W), :] = (o[i] / denom[i]).astype(o_ref.dtype)
        return carry

    lax.fori_loop(0, rows // nr, body, 0)


def _na_bias_table(rpb):
    col = np.arange(GRID_W)
    dc = np.clip(col[None, :] - col[:, None], 1 - NA_WIN_C, NA_WIN_C - 1) + NA_WIN_C - 1
    col_start = np.clip(col - NA_WIN_C // 2, 0, GRID_W - NA_WIN_C)
    in_win = (col[None, :] >= col_start[:, None]) & (col[None, :] < col_start[:, None] + NA_WIN_C)
    rows = jnp.stack([rpb[:, NA_WIN_R - 1 - typ:2 * NA_WIN_R - 1 - typ, :] for typ in range(NA_WIN_R)], axis=1)
    onehot = (dc[None] == np.arange(2 * NA_WIN_C - 1)[:, None, None]).astype(np.float32)
    b = jnp.einsum('htjc,cqk->htqjk', rows.astype(F32), onehot, precision=lax.Precision.HIGHEST)
    b = jnp.where(in_win[None, None, :, None, :], b, NEG_INF)
    return b.reshape(rpb.shape[0], NA_WIN_R, GRID_W, NA_WIN_R * GRID_W)


def _neighbourhood_attention(pm, pm_c, qn, kn, rpb):
    b, t, _ = pm.shape
    ctx_len = pm_c.shape[1]
    rows = t // GRID_W
    assert rows >= NA_WIN_R and rows * GRID_W == t and rows % NA_WIN_R == 0
    band = NA_WIN_R * GRID_W
    hb = lambda off: (lambda bi, h: (bi, 0, off // HEAD_DIM + h))
    return pl.pallas_call(
        functools.partial(_na_kernel, rows=rows),
        grid=(b, N_HEADS),
        in_specs=[
            pl.BlockSpec((1, t, HEAD_DIM), hb(C_NA_Q)),
            pl.BlockSpec((1, t, HEAD_DIM), hb(C_NA_K)),
            pl.BlockSpec((1, t, HEAD_DIM), hb(C_NA_V)),
            pl.BlockSpec((1, ctx_len, HEAD_DIM), hb(C_NA_K)),
            pl.BlockSpec((1, ctx_len, HEAD_DIM), hb(C_NA_V)),
            pl.BlockSpec((1, HEAD_DIM), lambda bi, h: (0, 0)),
            pl.BlockSpec((1, HEAD_DIM), lambda bi, h: (0, 0)),
            pl.BlockSpec((1, NA_WIN_R, GRID_W, band), lambda bi, h: (h, 0, 0, 0)),
        ],
        out_specs=pl.BlockSpec((1, t, HEAD_DIM), lambda bi, h: (bi, 0, h)),
        out_shape=SDS((b, t, W_BR), BF16),
        scratch_shapes=[pltpu.VMEM((t, HEAD_DIM), BF16)] * 3 + [pltpu.VMEM((ctx_len, HEAD_DIM), BF16)] * 2,
        compiler_params=_cp("parallel", "parallel"),
        name="natten",
    )(pm, pm, pm, pm_c, pm_c, qn, kn, _na_bias_table(rpb))


def _ctx_attn_kernel(q_ref, k_ref, v_ref, qn_ref, kn_ref, o_ref):
    q = _rms(q_ref[0].astype(F32), qn_ref[...]).astype(BF16)
    k = _rms(k_ref[0].astype(F32), kn_ref[...]).astype(BF16)
    s = _dot_nt(q, k) * HEAD_DIM ** -0.5
    p = jnp.exp(s - jnp.max(s, axis=-1, keepdims=True))
    o = _dot(p.astype(BF16), v_ref[0].astype(BF16)) / jnp.sum(p, axis=-1, keepdims=True)
    o_ref[0] = o.astype(o_ref.dtype)


def _dense_attention(pm_c, qn, kn):
    b, ctx_len, _ = pm_c.shape
    hb = lambda off: (lambda bi, h: (bi, 0, off // HEAD_DIM + h))
    return pl.pallas_call(
        _ctx_attn_kernel,
        grid=(b, N_HEADS),
        in_specs=[
            pl.BlockSpec((1, ctx_len, HEAD_DIM), hb(C_NA_Q)),
            pl.BlockSpec((1, ctx_len, HEAD_DIM), hb(C_NA_K)),
            pl.BlockSpec((1, ctx_len, HEAD_DIM), hb(C_NA_V)),
            pl.BlockSpec((1, HEAD_DIM), lambda bi, h: (0, 0)),
            pl.BlockSpec((1, HEAD_DIM), lambda bi, h: (0, 0)),
        ],
        out_specs=pl.BlockSpec((1, ctx_len, HEAD_DIM), lambda bi, h: (bi, 0, h)),
        out_shape=SDS((b, ctx_len, W_BR), BF16),
        compiler_params=_cp("parallel", "parallel"),
        name="ctx_attn",
    )(pm_c, pm_c, pm_c, qn, kn)


def _dn_conv_kernel(x_ref, w_ref, o_ref, xp_ref, *, n_norm_tiles):
    x = x_ref[0].astype(F32)
    t = x.shape[0]
    border = jnp.zeros((DN_CONV_PAD, x.shape[1]), F32)
    xp_ref[0:DN_CONV_PAD, :] = border
    xp_ref[DN_CONV_PAD + t:, :] = border
    xp_ref[DN_CONV_PAD:DN_CONV_PAD + t, :] = x
    half = DN_CONV // 2
    acc = x * w_ref[half:half + 1, :]
    for kk in range(DN_CONV):
        d = kk - half
        if d != 0:
            acc = acc + xp_ref[DN_CONV_PAD + d:DN_CONV_PAD + d + t, :] * w_ref[kk:kk + 1, :]
    y = _silu(acc)
    normed = pl.program_id(1) < n_norm_tiles
    for h in range(y.shape[1] // HEAD_DIM):
        sl = slice(h * HEAD_DIM, (h + 1) * HEAD_DIM)
        yh = y[:, sl]
        inv_norm = lax.rsqrt(jnp.sum(yh * yh, axis=-1, keepdims=True) + EPS)
        o_ref[0, :, sl] = yh * jnp.where(normed, inv_norm, 1.0)


def _dn_conv(pm, conv_w):
    b, t, _ = pm.shape
    tw = W_BR if t <= DN_CONV_WIDE_TILE_ROWS else HEAD_DIM
    return pl.pallas_call(
        functools.partial(_dn_conv_kernel, n_norm_tiles=2 * W_BR // tw),
        grid=(b, 3 * W_BR // tw),
        in_specs=[
            pl.BlockSpec((1, t, tw), lambda bi, j: (bi, 0, C_DN_QKV // tw + j)),
            pl.BlockSpec((DN_CONV, tw), lambda bi, j: (0, j)),
        ],
        out_specs=pl.BlockSpec((1, t, tw), lambda bi, j: (bi, 0, j)),
        out_shape=SDS((b, t, 3 * W_BR), F32),
        scratch_shapes=[pltpu.VMEM((t + 2 * DN_CONV_PAD, tw), F32)],
        compiler_params=_cp("parallel", "parallel"),
        name="dn_conv",
    )(pm, conv_w)


def _softplus(x):
    return jnp.maximum(x, 0.0) + jnp.log1p(jnp.exp(-jnp.abs(x)))


def _block_masks(n):
    ii = lax.broadcasted_iota(jnp.int32, (n, n), 0)
    jj = lax.broadcasted_iota(jnp.int32, (n, n), 1)
    same = lambda s: (ii // s) == (jj // s)
    masks, s = [same(DN_INV_BASE)], DN_INV_BASE
    while s < n:
        masks.append(same(2 * s) & jnp.logical_not(same(s)))
        s *= 2
    return masks


def _unit_triangular_inverses(a_list, eye, masks):
    mm = lambda x, y: _dot(x.astype(BF16), y.astype(BF16))
    n = len(a_list)
    pw = [jnp.where(masks[0], a, 0.0) for a in a_list]
    inv = [eye - p for p in pw]
    s = 2
    while s < DN_INV_BASE:
        pw = [mm(p, p) for p in pw]
        inv = [inv[i] + mm(inv[i], pw[i]) for i in range(n)]
        s *= 2
    for m in masks[1:]:
        t = [mm(jnp.where(m, a_list[i], 0.0), inv[i]) for i in range(n)]
        inv = [inv[i] - mm(inv[i], t[i]) for i in range(n)]
    return inv


def _dn_kernel(qf_ref, kf_ref, vf_ref, qb_ref, kb_ref, vb_ref, abf_ref, abb_ref, abtf_ref, abtb_ref,
               prow_ref, pcol_ref, s0_ref, of_ref, ob_ref, sfin_ref, s_ref):
    c = pl.program_id(1)

    @pl.when(c == 0)
    def _():
        s_ref[...] = s0_ref[...].reshape(s_ref.shape)

    cs = DN_CHUNK
    ii = lax.broadcasted_iota(jnp.int32, (cs, cs), 0)
    jj = lax.broadcasted_iota(jnp.int32, (cs, cs), 1)
    eye = (ii == jj).astype(F32)
    blk_masks = _block_masks(cs)
    scale = HEAD_DIM ** -0.5
    hi = lax.Precision.HIGHEST

    nb = qf_ref.shape[0]
    nst = 2 * N_HEADS
    chains = [(bb, d, h) for bb in range(nb) for d in range(2) for h in range(N_HEADS)]
    nch = len(chains)
    incl = [(jj <= ii), (jj >= ii)]
    strict = [(jj < ii), (jj > ii)]
    gc_col, gc_row, beta_all = {}, {}, {}
    for bb in range(nb):
        for d in range(2):
            tri = incl[d].astype(F32)
            ab = (abf_ref if d == 0 else abb_ref)[bb]
            abt = (abtf_ref if d == 0 else abtb_ref)[bb, 0]
            g_col = -jnp.exp(prow_ref[0:1, :]) * _softplus(ab + prow_ref[1:2, :])
            g_row = -jnp.exp(pcol_ref[:, 0:1]) * _softplus(abt + pcol_ref[:, 1:2])
            gc_col[bb, d] = jnp.dot(tri, g_col, preferred_element_type=F32, precision=hi)
            gc_row[bb, d] = lax.dot_general(g_row, tri, (((1,), (1,)), ((), ())),
                                            preferred_element_type=F32, precision=hi)
            beta_all[bb, d] = jax.nn.sigmoid(ab)
    qkv_refs = ((qf_ref, kf_ref, vf_ref), (qb_ref, kb_ref, vb_ref))
    sl = [slice(h * HEAD_DIM, (h + 1) * HEAD_DIM) for h in range(N_HEADS)]
    col = [d * N_HEADS + h for bb, d, h in chains]
    q = [qkv_refs[d][0][bb, :, sl[h]] * scale for bb, d, h in chains]
    k = [qkv_refs[d][1][bb, :, sl[h]] for bb, d, h in chains]
    v = [qkv_refs[d][2][bb, :, sl[h]] for bb, d, h in chains]
    s_old = [s_ref[bb * nst + col[j]] for j, (bb, d, h) in enumerate(chains)]
    gcc = [gc_col[bb, d][:, col[j]:col[j] + 1] for j, (bb, d, h) in enumerate(chains)]
    gcr = [gc_row[bb, d][col[j]:col[j] + 1, :] for j, (bb, d, h) in enumerate(chains)]
    beta = [beta_all[bb, d][:, nst + col[j]:nst + col[j] + 1] for j, (bb, d, h) in enumerate(chains)]
    decay = [jnp.exp(jnp.where(incl[d], gcc[j] - gcr[j], -jnp.inf)) for j, (bb, d, h) in enumerate(chains)]
    eg = [jnp.exp(x) for x in gcc]
    g_last = [gcc[j][cs - 1:cs, :] if d == 0 else gcc[j][0:1, :] for j, (bb, d, h) in enumerate(chains)]
    kbf = [x.astype(BF16) for x in k]
    kbeta = [k[j] * beta[j] for j in range(nch)]
    kq = [_dot_nt(jnp.concatenate([kbeta[j], q[j]], axis=0).astype(BF16), kbf[j]) for j in range(nch)]
    a_mat = [jnp.where(strict[d], kq[j][:cs] * decay[j], 0.0) for j, (bb, d, h) in enumerate(chains)]
    qk = [jnp.where(incl[d], kq[j][cs:] * decay[j], 0.0) for j, (bb, d, h) in enumerate(chains)]
    inv = _unit_triangular_inverses(a_mat, eye, blk_masks)
    rhs = [jnp.concatenate([v[j] * beta[j], kbeta[j] * eg[j]], axis=1).astype(BF16) for j in range(nch)]
    uw = [_dot(inv[j].astype(BF16), rhs[j]) for j in range(nch)]
    ws_qs = [_dot(jnp.concatenate([uw[j][:, HEAD_DIM:], q[j] * eg[j]], axis=0).astype(BF16), s_old[j].astype(BF16))
             for j in range(nch)]
    v_new = [uw[j][:, :HEAD_DIM] - ws_qs[j][:cs] for j in range(nch)]
    vnb = [x.astype(BF16) for x in v_new]
    o = [ws_qs[j][cs:] + _dot(qk[j].astype(BF16), vnb[j]) for j in range(nch)]
    kd = [(k[j] * jnp.exp(g_last[j] - gcc[j])).astype(BF16) for j in range(nch)]
    s_new = [s_old[j] * jnp.exp(g_last[j]) + _dot_tn(kd[j], vnb[j]) for j in range(nch)]
    for j, (bb, d, h) in enumerate(chains):
        s_ref[bb * nst + col[j]] = s_new[j]
        (of_ref if d == 0 else ob_ref)[bb, :, sl[h]] = o[j]

    @pl.when(c == pl.num_programs(1) - 1)
    def _():
        sfin_ref[...] = s_ref[...].reshape(sfin_ref.shape)


def _deltanet(act, pab, a_log, dt_bias, s0):
    b, t, _ = act.shape
    n = t // DN_CHUNK
    assert n * DN_CHUNK == t
    abt = jnp.swapaxes(pab[..., :N_AB].reshape(b, n, DN_CHUNK, N_AB), 2, 3)
    al, dt = a_log.reshape(-1).astype(F32), dt_bias.reshape(-1).astype(F32)
    prow = jnp.zeros((2, LANES), F32).at[0, :al.size].set(al).at[1, :dt.size].set(dt)
    pcol = jnp.zeros((N_AB, 2), F32).at[:al.size, 0].set(al).at[:dt.size, 1].set(dt)
    nb = DN_BATCH_PER_STEP if b % DN_BATCH_PER_STEP == 0 else 1
    fwd = lambda part: (lambda bi, c: (bi, c, part))
    bwd = lambda part: (lambda bi, c: (bi, n - 1 - c, part))
    qkv_spec = lambda imap: pl.BlockSpec((nb, DN_CHUNK, W_BR), imap)
    n_state = 2 * N_HEADS
    state_spec = pl.BlockSpec((nb, n_state, HEAD_DIM, HEAD_DIM), lambda bi, c: (bi, 0, 0, 0))
    return pl.pallas_call(
        _dn_kernel,
        grid=(b // nb, n),
        in_specs=[
            qkv_spec(fwd(0)), qkv_spec(fwd(1)), qkv_spec(fwd(2)),
            qkv_spec(bwd(0)), qkv_spec(bwd(1)), qkv_spec(bwd(2)),
            pl.BlockSpec((nb, DN_CHUNK, LANES), fwd(0)),
            pl.BlockSpec((nb, DN_CHUNK, LANES), bwd(0)),
            pl.BlockSpec((nb, 1, N_AB, DN_CHUNK), lambda bi, c: (bi, c, 0, 0)),
            pl.BlockSpec((nb, 1, N_AB, DN_CHUNK), lambda bi, c: (bi, n - 1 - c, 0, 0)),
            pl.BlockSpec((2, LANES), lambda bi, c: (0, 0)),
            pl.BlockSpec((N_AB, 2), lambda bi, c: (0, 0)),
            state_spec,
        ],
        out_specs=[
            pl.BlockSpec((nb, DN_CHUNK, W_BR), fwd(0)),
            pl.BlockSpec((nb, DN_CHUNK, W_BR), bwd(0)),
            state_spec,
        ],
        out_shape=[SDS((b, t, W_BR), F32), SDS((b, t, W_BR), F32),
                   SDS((b, n_state, HEAD_DIM, HEAD_DIM), F32)],
        scratch_shapes=[pltpu.VMEM((nb * n_state, HEAD_DIM, HEAD_DIM), F32)],
        compiler_params=_cp("parallel", "arbitrary"),
        name="deltanet",
    )(act, act, act, act, act, act, pab, pab, abt, abt, prow, pcol, s0)


def _dn_out_kernel(of_ref, ob_ref, z_ref, w_ref, y_ref):
    o = of_ref[0] + ob_ref[0]
    z = z_ref[0].astype(F32)
    for h in range(N_HEADS):
        sl = slice(h * HEAD_DIM, (h + 1) * HEAD_DIM)
        y_ref[0, :, sl] = (_rms(o[:, sl], w_ref[...]) * _silu(z[:, sl])).astype(y_ref.dtype)


def _dn_out(o_f, o_b, pm, onorm):
    b, t, _ = o_f.shape
    tm = _tile(t, 1024)
    return pl.pallas_call(
        _dn_out_kernel,
        grid=(b, t // tm),
        in_specs=[
            pl.BlockSpec((1, tm, W_BR), lambda bi, i: (bi, i, 0)),
            pl.BlockSpec((1, tm, W_BR), lambda bi, i: (bi, i, 0)),
            pl.BlockSpec((1, tm, W_BR), lambda bi, i: (bi, i, C_DN_Z // W_BR)),
            pl.BlockSpec((1, HEAD_DIM), lambda bi, i: (0, 0)),
        ],
        out_specs=pl.BlockSpec((1, tm, W_BR), lambda bi, i: (bi, i, 0)),
        out_shape=SDS((b, t, W_BR), BF16),
        compiler_params=_cp("parallel", "parallel"),
        name="dn_out",
    )(o_f, o_b, pm, onorm)


def _fourier_kernel(x_ref, m1_ref, m2_ref, m3_ref, twc_ref, tws_ref, o_ref, xf_ref, bre_ref, bim_ref, zo_ref,
                    *, n1, n2, scale):
    m1, m2, m3 = m1_ref[...].astype(BF16), m2_ref[...].astype(BF16), m3_ref[...].astype(BF16)
    nu = FT_UNROLL
    p2, p1 = n2 + FT_PAD, n1 + FT_PAD
    for t1 in range(n1):
        xf_ref[t1 * p2:t1 * p2 + n2, :] = x_ref[0, t1 * n2:(t1 + 1) * n2, :].astype(F32)

    def stage1(i, carry):
        t2 = [i * nu + u for u in range(nu)]
        xs = [xf_ref[pl.ds(t, n1, stride=p2), :].astype(BF16) for t in t2]
        a = [_dot(m1, x) for x in xs]
        cw = [twc_ref[t] for t in t2]
        sw = [tws_ref[t] for t in t2]
        bre = [a[u][:n1] * cw[u] - a[u][n1:] * sw[u] for u in range(nu)]
        bim = [-(a[u][n1:] * cw[u] + a[u][:n1] * sw[u]) for u in range(nu)]
        for u in range(nu):
            bre_ref[pl.ds(t2[u], n1, stride=p2), :] = bre[u]
            bim_ref[pl.ds(t2[u], n1, stride=p2), :] = bim[u]
        return carry

    lax.fori_loop(0, n2 // nu, stage1, 0)

    def stage2(i, carry):
        f1 = [i * nu + u for u in range(nu)]
        r0 = [pl.multiple_of(f * p2, FT_PAD) for f in f1]
        b = [jnp.concatenate([bre_ref[pl.ds(r, n2), :], bim_ref[pl.ds(r, n2), :]], axis=0).astype(BF16) for r in r0]
        y = [_dot(m2, x) for x in b]
        z = [_dot(jnp.concatenate([yy[:n2], yy[n2:]], axis=1).astype(BF16), m3) for yy in y]
        for u in range(nu):
            zo_ref[pl.ds(f1[u], n2, stride=p1), :] = z[u] * scale
        return carry

    lax.fori_loop(0, n1 // nu, stage2, 0)
    for f2 in range(n2):
        o_ref[0, f2 * n1:(f2 + 1) * n1, :] = zo_ref[f2 * p1:f2 * p1 + n1, :].astype(o_ref.dtype)


def _dft_mats(n):
    ang = 2.0 * np.pi * ((np.arange(n)[:, None] * np.arange(n)[None, :]) % n) / n
    return np.cos(ang), np.sin(ang)


def _fourier(pm):
    b, t, _ = pm.shape
    n1 = 1 << ((t.bit_length() - 1 + 1) // 2)
    n2 = t // n1
    assert n1 * n2 == t and n1 % FT_UNROLL == 0 and n2 % FT_UNROLL == 0
    c1, s1 = _dft_mats(n1)
    c2, s2 = _dft_mats(n2)
    cc, sc = _dft_mats(HEAD_DIM)
    m1 = jnp.asarray(np.concatenate([c1, s1], axis=0), F32)
    m2 = jnp.asarray(np.block([[c2, s2], [-s2, c2]]), F32)
    m3 = jnp.asarray(np.concatenate([cc, sc], axis=0), F32)
    ang = 2.0 * np.pi * ((np.arange(n2)[:, None] * np.arange(n1)[None, :]) % t) / t
    twc = jnp.broadcast_to(jnp.asarray(np.cos(ang), F32)[:, :, None], (n2, n1, HEAD_DIM))
    tws = jnp.broadcast_to(jnp.asarray(np.sin(ang), F32)[:, :, None], (n2, n1, HEAD_DIM))
    const = lambda shape: pl.BlockSpec(shape, lambda bi, g: (0,) * len(shape))
    return pl.pallas_call(
        functools.partial(_fourier_kernel, n1=n1, n2=n2, scale=float((t * HEAD_DIM) ** -0.5)),
        grid=(b, N_HEADS),
        in_specs=[
            pl.BlockSpec((1, t, HEAD_DIM), lambda bi, g: (bi, 0, C_FT // HEAD_DIM + g)),
            const(m1.shape), const(m2.shape), const(m3.shape),
            const((n2, n1, HEAD_DIM)), const((n2, n1, HEAD_DIM)),
        ],
        out_specs=pl.BlockSpec((1, t, HEAD_DIM), lambda bi, g: (bi, 0, g)),
        out_shape=SDS((b, t, W_BR), BF16),
        scratch_shapes=[pltpu.VMEM((n1 * (n2 + FT_PAD), HEAD_DIM), F32)] * 3
        + [pltpu.VMEM((n2 * (n1 + FT_PAD), HEAD_DIM), F32)],
        compiler_params=_cp("parallel", "parallel"),
        name="fourier",
    )(pm, m1, m2, m3, twc, tws)


def _sgu_kernel(u_ref, v_ref, vn_ref, ws_ref, bs_ref, o_ref):
    u = jax.nn.gelu(u_ref[0].astype(F32))
    v = jax.nn.gelu(v_ref[0].astype(F32))
    mu = jnp.mean(v, axis=-1, keepdims=True)
    var = jnp.mean(jnp.square(v - mu), axis=-1, keepdims=True)
    vb = ((v - mu) * lax.rsqrt(var + EPS) * vn_ref[...]).astype(BF16)
    out = []
    for ch in range(u.shape[0] // SG_CHUNK):
        rs = slice(ch * SG_CHUNK, (ch + 1) * SG_CHUNK)
        mix = [_dot(ws_ref[g], vb[rs, g * HEAD_DIM:(g + 1) * HEAD_DIM]) + bs_ref[:, g:g + 1] for g in range(N_HEADS)]
        out.append((u[rs, :] * jnp.concatenate(mix, axis=1)).astype(o_ref.dtype))
    for ch, y in enumerate(out):
        o_ref[0, ch * SG_CHUNK:(ch + 1) * SG_CHUNK, :] = y


def _spatial_gating(pm, ws, bs_t, vnorm):
    b, t, _ = pm.shape
    tm = _tile(t, 4 * SG_CHUNK, SG_CHUNK)
    assert tm % SG_CHUNK == 0
    return pl.pallas_call(
        _sgu_kernel,
        grid=(b, t // tm),
        in_specs=[
            pl.BlockSpec((1, tm, W_BR), lambda bi, i: (bi, i, C_SG_U // W_BR)),
            pl.BlockSpec((1, tm, W_BR), lambda bi, i: (bi, i, C_SG_V // W_BR)),
            pl.BlockSpec((1, W_BR), lambda bi, i: (0, 0)),
            pl.BlockSpec((N_HEADS, SG_CHUNK, SG_CHUNK), lambda bi, i: (0, 0, 0)),
            pl.BlockSpec((SG_CHUNK, N_HEADS), lambda bi, i: (0, 0)),
        ],
        out_specs=pl.BlockSpec((1, tm, W_BR), lambda bi, i: (bi, i, 0)),
        out_shape=SDS((b, t, W_BR), BF16),
        compiler_params=_cp("parallel", "parallel"),
        name="sgu",
    )(pm, pm, vnorm, ws, bs_t)


def _merge_kernel(h_ref, y0_ref, y1_ref, y2_ref, y3_ref, wg_ref, bg_ref, wb_ref, o_ref, wgs_ref, wbs_ref):
    @pl.when(jnp.logical_and(pl.program_id(1) == 0, pl.program_id(2) == 0))
    def _():
        wgs_ref[...] = wg_ref[0].astype(BF16)
        wbs_ref[...] = wb_ref[0].astype(BF16)

    h = h_ref[0]
    acc = None
    for i, y_ref in enumerate((y0_ref, y1_ref, y2_ref, y3_ref)):
        gate = jax.nn.sigmoid(_dot(h, wgs_ref[i]) + bg_ref[0, i])
        term = gate * _dot(y_ref[0].astype(BF16), wbs_ref[i])
        acc = term if acc is None else acc + term
    o_ref[0] = acc.astype(o_ref.dtype)


def _merge(h, ys, wg, bg, wb, l):
    bx, t, d = h.shape
    tm, tn = _tile(t, 1024), _tile(d, 256, LANES)
    y_spec = pl.BlockSpec((1, tm, W_BR), lambda j, b, i: (b, i, 0))
    return pl.pallas_call(
        _merge_kernel,
        grid=(d // tn, bx, t // tm),
        in_specs=[
            pl.BlockSpec((1, tm, d), lambda j, b, i: (b, i, 0)),
            y_spec, y_spec, y_spec, y_spec,
            pl.BlockSpec((1, N_BRANCH, d, tn), lambda j, b, i: (l, 0, 0, j)),
            pl.BlockSpec((1, N_BRANCH, 1, tn), lambda j, b, i: (l, 0, 0, j)),
            pl.BlockSpec((1, N_BRANCH, W_BR, tn), lambda j, b, i: (l, 0, 0, j)),
        ],
        out_specs=pl.BlockSpec((1, tm, tn), lambda j, b, i: (b, i, j)),
        out_shape=SDS((bx, t, d), BF16),
        scratch_shapes=[pltpu.VMEM((N_BRANCH, d, tn), BF16), pltpu.VMEM((N_BRANCH, W_BR, tn), BF16)],
        compiler_params=_cp("arbitrary", "arbitrary", "arbitrary"),
        name="merge",
    )(h, *ys, wg, bg, wb)


def _resid_kernel(a_ref, w_ref, x_ref, g_ref, o_ref, *, row):
    o_ref[0] = x_ref[0] + _mod_row(g_ref, row) * _dot(a_ref[0], w_ref[0])


def _resid_norm_kernel(a_ref, w_ref, x_ref, g_ref, nw_ref, sh_ref, sc_ref, o_ref, h_ref, *, row):
    x_new = x_ref[0] + _mod_row(g_ref, row) * _dot(a_ref[0], w_ref[0])
    o_ref[0] = x_new
    h_ref[0] = _norm_mod(x_new, nw_ref, sh_ref, sc_ref, row)


def _resid_proj_norm(a, w, l, x, mod, gate_blk, nw, norm_blk, row):
    bx, t, k = a.shape
    d = w.shape[2]
    assert k * d * 2 <= RESIDENT_WEIGHT_BYTES
    tm = _tile(t, 512)
    row_spec = lambda dt: pl.BlockSpec((1, tm, d), lambda b, i: (b, i, 0))
    return pl.pallas_call(
        functools.partial(_resid_norm_kernel, row=row),
        grid=(bx, t // tm),
        in_specs=[
            pl.BlockSpec((1, tm, k), lambda b, i: (b, i, 0)),
            pl.BlockSpec((1, k, d), lambda b, i: (l, 0, 0)),
            row_spec(F32),
            pl.BlockSpec((MOD_ROWS, d), lambda b, i: (0, gate_blk)),
            pl.BlockSpec((1, d), lambda b, i: (0, 0)),
            pl.BlockSpec((MOD_ROWS, d), lambda b, i: (0, norm_blk)),
            pl.BlockSpec((MOD_ROWS, d), lambda b, i: (0, norm_blk + 1)),
        ],
        out_specs=[row_spec(F32), row_spec(BF16)],
        out_shape=[SDS((bx, t, d), F32), SDS((bx, t, d), BF16)],
        compiler_params=_cp("parallel", "parallel"),
        name="resid_proj_norm",
    )(a, w, x, mod, nw, mod, mod)


def _resid_proj(a, w, l, x, mod, gate_blk, row):
    bx, t, k = a.shape
    d = w.shape[2]
    tm, tn = _tile(t, 256), d
    nj = d // tn
    return pl.pallas_call(
        functools.partial(_resid_kernel, row=row),
        grid=(bx, t // tm, nj),
        in_specs=[
            pl.BlockSpec((1, tm, k), lambda b, i, j: (b, i, 0)),
            pl.BlockSpec((1, k, tn), lambda b, i, j: (l, 0, j), pipeline_mode=pl.Buffered(1)),
            pl.BlockSpec((1, tm, tn), lambda b, i, j: (b, i, j)),
            pl.BlockSpec((MOD_ROWS, tn), lambda b, i, j: (0, gate_blk * nj + j)),
        ],
        out_specs=pl.BlockSpec((1, tm, tn), lambda b, i, j: (b, i, j)),
        out_shape=SDS((bx, t, d), F32),
        compiler_params=_cp("parallel", "parallel", "arbitrary"),
        name="resid_proj",
    )(a, w, x, mod)


def _ffn_up_kernel(h_ref, w1_ref, w3_ref, o_ref):
    h = h_ref[0]
    w1, w3 = w1_ref[0].astype(BF16), w3_ref[0].astype(BF16)
    o_ref[0] = (_silu(_dot(h, w1)) * _dot(h, w3)).astype(o_ref.dtype)


def _ffn_up(h, w1, w3, l):
    bx, t, d = h.shape
    f = w1.shape[2]
    tm, tn = _tile(t, FFN_UP_ROWS), _tile(f, 512, LANES)
    return pl.pallas_call(
        _ffn_up_kernel,
        grid=(bx, t // tm, f // tn),
        in_specs=[
            pl.BlockSpec((1, tm, d), lambda b, i, j: (b, i, 0)),
            pl.BlockSpec((1, d, tn), lambda b, i, j: (l, 0, j)),
            pl.BlockSpec((1, d, tn), lambda b, i, j: (l, 0, j)),
        ],
        out_specs=pl.BlockSpec((1, tm, tn), lambda b, i, j: (b, i, j)),
        out_shape=SDS((bx, t, f), BF16),
        compiler_params=_cp("parallel", "parallel", "arbitrary"),
        name="ffn_up",
    )(h, w1, w3)


def _mixers(pm, pab, pm_c, pab_c, p, need_ctx):
    b = pm.shape[0]
    y_na = _neighbourhood_attention(pm, pm_c, p["qn"], p["kn"], p["rpb"])
    zero = jnp.zeros((b, 2 * N_HEADS, HEAD_DIM, HEAD_DIM), F32)
    of_c, ob_c, s_ctx = _deltanet(_dn_conv(pm_c, p["conv"]), pab_c, p["a_log"], p["dt_bias"], zero)
    of, ob, _ = _deltanet(_dn_conv(pm, p["conv"]), pab, p["a_log"], p["dt_bias"], s_ctx)
    ys = (y_na, _dn_out(of, ob, pm, p["onorm"]), _fourier(pm), _spatial_gating(pm, p["sg_w"], p["sg_bt"], p["sg_vn"]))
    if not need_ctx:
        return ys, None
    ys_c = (_dense_attention(pm_c, p["qn"], p["kn"]), _dn_out(of_c, ob_c, pm_c, p["onorm"]), _fourier(pm_c),
            _spatial_gating(pm_c, p["sg_w"], p["sg_bt"], p["sg_vn"]))
    return ys, ys_c


def kernel(x, c, ctx, c_ctx, w_ada, b_ada, norm1_w, norm2_w, w_in, na_qnorm, na_knorm, na_rpb, dn_conv, dn_a_log, dn_dt_bias, dn_onorm, sg_w, sg_b, sg_vnorm, w_gate, b_gate, w_branch, w_out, ffn_w1, ffn_w3, ffn_w2):
    bsz, seq, d = x.shape
    ctx_len = ctx.shape[1]
    depth = w_ada.shape[0]
    assert bsz + 1 <= MOD_ROWS and d % LANES == 0 and w_ada.shape[2] == N_MOD * d
    ctx_row = bsz

    cc = jnp.zeros((MOD_ROWS, d), F32).at[:bsz].set(c).at[ctx_row].set(c_ctx)
    mod_all = _ada_mod(cc, w_ada, b_ada)
    xc = ctx.reshape(1, bsz * ctx_len, d)

    w_in_b = w_in.astype(BF16)
    w_main = jnp.concatenate([w_in_b[..., :C_AB_SRC], w_in_b[..., C_AB_SRC + N_AB:]], axis=-1)
    w_ab = jnp.pad(w_in_b[..., C_AB_SRC:C_AB_SRC + N_AB], ((0, 0), (0, 0), (0, LANES - N_AB)))
    wg, wb, wo = w_gate, w_branch, w_out.astype(BF16)
    w1, w3, w2 = ffn_w1, ffn_w3, ffn_w2.astype(BF16)
    bg = b_gate.reshape(depth, N_BRANCH, 1, d)

    for l in range(depth):
        need_ctx = l < depth - 1
        mod = mod_all[l]
        p = dict(qn=na_qnorm[l][None], kn=na_knorm[l][None], rpb=na_rpb[l], conv=dn_conv[l], a_log=dn_a_log[l],
                 dt_bias=dn_dt_bias[l], onorm=dn_onorm[l][None], sg_w=sg_w[l].astype(BF16), sg_bt=sg_b[l].T,
                 sg_vn=sg_vnorm[l][None])
        n1w, n2w = norm1_w[l][None], norm2_w[l][None]

        pm, pab, h = _in_proj(x, n1w, mod, w_main, w_ab, l, None)
        pm_c, pab_c, h_c = _in_proj(xc, n1w, mod, w_main, w_ab, l, ctx_row)
        pm_c = pm_c.reshape(bsz, ctx_len, N_MAIN)
        pab_c = pab_c.reshape(bsz, ctx_len, LANES)
        ys, ys_c = _mixers(pm, pab, pm_c, pab_c, p, need_ctx)

        merged = _merge(h, ys, wg, bg, wb, l)
        x, h2 = _resid_proj_norm(merged, wo, l, x, mod, 2, n2w, 3, None)
        x = _resid_proj(_ffn_up(h2, w1, w3, l), w2, l, x, mod, 5, None)
        if need_ctx:
            ys_c = tuple(y.reshape(1, bsz * ctx_len, W_BR) for y in ys_c)
            merged_c = _merge(h_c, ys_c, wg, bg, wb, l)
            xc, h2_c = _resid_proj_norm(merged_c, wo, l, xc, mod, 2, n2w, 3, ctx_row)
            xc = _resid_proj(_ffn_up(h2_c, w1, w3, l), w2, l, xc, mod, 5, ctx_row)
    return x
```
